```python
import jax, jax.numpy as jnp
from jax import lax
import numpy as np

D_MODEL = 2048
BATCH = 4
SEQ = 2048
DEPTH = 2
DEC_BATCH = 128
DEC_SEQ = 1
PAST_LEN = 16384
PAGE_SIZE = 128

MIX_DIM = D_MODEL
CONV_DIM = MIX_DIM // 2
CONV_GROUPS = 16
CONV_WIDTH = 3
GLA_HEADS = 4
GLA_V_DIM = MIX_DIM - CONV_DIM
GLA_QK_DIM = GLA_V_DIM // 2
GLA_DK = GLA_QK_DIM // GLA_HEADS
GLA_DV = GLA_V_DIM // GLA_HEADS
GATE_RANK = 16
GATE_TAU = 16.0
GLA_CHUNK = 64
D_FF = 5632
EPS = 1e-6

_S1 = CONV_DIM
_S2 = _S1 + CONV_DIM
_S3 = _S2 + CONV_DIM
_S4 = _S3 + GLA_QK_DIM
_S5 = _S4 + GLA_QK_DIM
_S6 = _S5 + GLA_V_DIM
_S7 = _S6 + GLA_V_DIM
IN_COLS = _S7 + GATE_RANK

kernel_name = "hybrid_shortconv_gla_convffn_step"


def rmsnorm(x, g):
    xf = x.astype(jnp.float32)
    var = jnp.mean(xf * xf, axis=-1, keepdims=True)
    return (xf * lax.rsqrt(var + EPS) * g.astype(jnp.float32)).astype(x.dtype)


def causal_dwconv(u, prev, w):
    full = jnp.concatenate([prev.astype(u.dtype), u], axis=1)
    L = u.shape[1]
    y = full[:, 0:L] * w[0]
    for tap in range(1, CONV_WIDTH):
        y = y + full[:, tap:tap + L] * w[tap]
    return y, full[:, -(CONV_WIDTH - 1):]


def pick_chunk(L):
    return GLA_CHUNK if L % GLA_CHUNK == 0 else L


def gla_chunked(q, k, v, logf, s0, chunk):
    B, L, H, DK = q.shape
    DV = v.shape[-1]
    n = L // chunk

    def blocks(t):
        return jnp.moveaxis(t.reshape((B, n, chunk) + t.shape[2:]), 1, 0)

    mask = jnp.tril(jnp.ones((chunk, chunk), dtype=bool))[None, :, :, None, None]

    def step(S, inp):
        qc, kc, vc, gc = inp
        b = jnp.cumsum(gc, axis=1)
        inter = jnp.einsum('bihk,bhkv->bihv', qc * jnp.exp(b), S)
        diff = b[:, :, None] - b[:, None, :]
        decay = jnp.exp(jnp.where(mask, diff, -jnp.inf))
        A = jnp.einsum('bihk,bjhk,bijhk->bhij', qc, kc, decay)
        intra = jnp.einsum('bhij,bjhv->bihv', A, vc)
        bC = b[:, -1]
        kdec = kc * jnp.exp(bC[:, None] - b)
        S_new = jnp.exp(bC)[..., None] * S + jnp.einsum('bjhk,bjhv->bhkv', kdec, vc)
        return S_new, inter + intra

    s_fin, o = lax.scan(step, s0, (blocks(q), blocks(k), blocks(v), blocks(logf)))
    o = jnp.moveaxis(o, 0, 1).reshape(B, L, H, DV)
    return o, s_fin


def mixer(xn, conv_prev, gla_prev, w_in, conv_w, gate_w2, gate_b, gla_norm_g, w_out, chunk):
    Bn, L, _ = xn.shape
    proj = xn @ w_in
    bg, cg, hin, q, k, v, og, glr = jnp.split(proj, [_S1, _S2, _S3, _S4, _S5, _S6, _S7], axis=-1)
    cu, conv_new = causal_dwconv(cg * hin, conv_prev, conv_w)
    ya = bg * cu
    f32 = jnp.float32
    logf = jax.nn.log_sigmoid((glr @ gate_w2 + gate_b).astype(f32)) / GATE_TAU
    qh = q.astype(f32).reshape(Bn, L, GLA_HEADS, GLA_DK) * (GLA_DK ** -0.5)
    kh = k.astype(f32).reshape(Bn, L, GLA_HEADS, GLA_DK)
    vh = v.astype(f32).reshape(Bn, L, GLA_HEADS, GLA_DV)
    lf = logf.reshape(Bn, L, GLA_HEADS, GLA_DK)
    o, s_new = gla_chunked(qh, kh, vh, lf, gla_prev.astype(f32), chunk)
    var = jnp.mean(o * o, axis=-1, keepdims=True)
    o = o * lax.rsqrt(var + EPS) * gla_norm_g.astype(f32).reshape(GLA_HEADS, GLA_DV)
    yb = (o.reshape(Bn, L, GLA_V_DIM) * jax.nn.silu(og.astype(f32))).astype(xn.dtype)
    out = jnp.concatenate([ya, yb], axis=-1) @ w_out
    return out, conv_new, s_new


def conv_ffn(xn, prev, w_up, conv_w, conv_b, w_down):
    u, v = jnp.split(xn @ w_up, 2, axis=-1)
    cu, new = causal_dwconv(u, prev, conv_w)
    h = jax.nn.silu(cu + conv_b) * v
    return h @ w_down, new


def trunk(x, conv_prev, gla_prev, ffn_prev, norm_mix_g, w_in, conv_w, gate_w2, gate_b,
          gla_norm_g, w_out, norm_ffn_g, w_up, ffn_conv_w, ffn_conv_b, w_down, final_norm_g):
    chunk = pick_chunk(x.shape[1])
    convs, glas, ffns = [], [], []
    for l in range(DEPTH):
        m, c_new, s_new = mixer(rmsnorm(x, norm_mix_g[l]), conv_prev[l], gla_prev[l], w_in[l],
                                conv_w[l], gate_w2[l], gate_b[l], gla_norm_g[l], w_out[l], chunk)
        x = x + m
        f, f_new = conv_ffn(rmsnorm(x, norm_ffn_g[l]), ffn_prev[l], w_up[l], ffn_conv_w[l],
                            ffn_conv_b[l], w_down[l])
        x = x + f
        convs.append(c_new)
        glas.append(s_new)
        ffns.append(f_new)
    return rmsnorm(x, final_norm_g), jnp.stack(convs), jnp.stack(glas), jnp.stack(ffns)


def setup_inputs(seed: int = 0) -> dict:
    key = jax.random.key(seed)
    ks = jax.random.split(key, 20)
    nrm = jax.random.normal
    f32 = jnp.float32
    return {
        "x_prompt": nrm(ks[0], (BATCH, SEQ, D_MODEL), f32),
        "x_sample": nrm(ks[1], (DEC_BATCH, DEC_SEQ, D_MODEL), f32),
        "state_conv": nrm(ks[2], (DEPTH, DEC_BATCH, CONV_WIDTH - 1, CONV_DIM), f32),
        "state_gla": 0.5 * nrm(ks[3], (DEPTH, DEC_BATCH, GLA_HEADS, GLA_DK, GLA_DV), f32),
        "state_ffn_conv": nrm(ks[4], (DEPTH, DEC_BATCH, CONV_WIDTH - 1, D_FF), f32),
        "norm_mix_g": 1.0 + 0.02 * nrm(ks[5], (DEPTH, D_MODEL), f32),
        "w_in": nrm(ks[6], (DEPTH, D_MODEL, IN_COLS), f32) * D_MODEL ** -0.5,
        "conv_w": nrm(ks[7], (DEPTH, CONV_WIDTH, CONV_DIM), f32) * CONV_WIDTH ** -0.5,
        "gate_w2": nrm(ks[8], (DEPTH, GATE_RANK, GLA_QK_DIM), f32) * GATE_RANK ** -0.5,
        "gate_b": 0.1 * nrm(ks[9], (DEPTH, GLA_QK_DIM), f32),
        "gla_norm_g": 1.0 + 0.02 * nrm(ks[10], (DEPTH, GLA_V_DIM), f32),
        "w_out": nrm(ks[11], (DEPTH, MIX_DIM, D_MODEL), f32) * MIX_DIM ** -0.5,
        "norm_ffn_g": 1.0 + 0.02 * nrm(ks[12], (DEPTH, D_MODEL), f32),
        "w_up": nrm(ks[13], (DEPTH, D_MODEL, 2 * D_FF), f32) * D_MODEL ** -0.5,
        "ffn_conv_w": nrm(ks[14], (DEPTH, CONV_WIDTH, D_FF), f32) * CONV_WIDTH ** -0.5,
        "ffn_conv_b": 0.02 * nrm(ks[15], (DEPTH, D_FF), f32),
        "w_down": nrm(ks[16], (DEPTH, D_FF, D_MODEL), f32) * D_FF ** -0.5,
        "final_norm_g": 1.0 + 0.02 * nrm(ks[17], (D_MODEL,), f32),
    }


def reference(x_prompt, x_sample, state_conv, state_gla, state_ffn_conv, norm_mix_g, w_in,
              conv_w, gate_w2, gate_b, gla_norm_g, w_out, norm_ffn_g, w_up, ffn_conv_w,
              ffn_conv_b, w_down, final_norm_g):
    dt = x_prompt.dtype
    zero_conv = jnp.zeros((DEPTH, BATCH, CONV_WIDTH - 1, CONV_DIM), dt)
    zero_gla = jnp.zeros((DEPTH, BATCH, GLA_HEADS, GLA_DK, GLA_DV), jnp.float32)
    zero_ffn = jnp.zeros((DEPTH, BATCH, CONV_WIDTH - 1, D_FF), dt)
    y_prompt, conv_p, gla_p, ffn_p = trunk(
        x_prompt, zero_conv, zero_gla, zero_ffn, norm_mix_g, w_in, conv_w, gate_w2, gate_b,
        gla_norm_g, w_out, norm_ffn_g, w_up, ffn_conv_w, ffn_conv_b, w_down, final_norm_g)
    y_sample, conv_s, gla_s, ffn_s = trunk(
        x_sample, state_conv, state_gla, state_ffn_conv, norm_mix_g, w_in, conv_w, gate_w2,
        gate_b, gla_norm_g, w_out, norm_ffn_g, w_up, ffn_conv_w, ffn_conv_b, w_down, final_norm_g)
    return (y_prompt, y_sample, conv_p, gla_p, ffn_p, conv_s, gla_s, ffn_s)
```

```python
import functools

import numpy as np
import jax
import jax.numpy as jnp
from jax import lax
from jax.experimental import pallas as pl
from jax.experimental.pallas import tpu as pltpu

F32 = jnp.float32
BF16 = jnp.bfloat16

D_MODEL = 2048
DEPTH = 2
CONV_DIM = 1024
CONV_WIDTH = 3
GLA_HEADS = 4
GLA_QK_DIM = 512
GLA_V_DIM = 1024
GLA_DK = 128
GLA_DV = 256
GATE_RANK = 16
GATE_TAU = 16.0
D_FF = 5632
EPS = 1e-6
MAIN_COLS = 3 * CONV_DIM + 2 * GLA_QK_DIM + 2 * GLA_V_DIM
GATE_PAD = 128

GLA_CHUNK = 128
GLA_LEVELS = tuple(1 << i for i in range(GLA_CHUNK.bit_length() - 1))

VMEM_LIMIT = 56 * 1024 * 1024


def _params(n_axes):
    return pltpu.CompilerParams(dimension_semantics=("arbitrary",) * n_axes,
                                vmem_limit_bytes=VMEM_LIMIT)


def _dot(a, b):
    return jnp.dot(a, b, preferred_element_type=F32)


def _dot_nt(a, b):
    return lax.dot_general(a, b, (((1,), (1,)), ((), ())), preferred_element_type=F32)


def _split3(x):
    hi = x.astype(BF16)
    r1 = x - hi.astype(F32)
    mid = r1.astype(BF16)
    lo = (r1 - mid.astype(F32)).astype(BF16)
    return hi, mid, lo


def _sigmoid(x):
    return 1.0 / (1.0 + jnp.exp(-x))


def _log_sigmoid(x):
    return jnp.minimum(x, 0.0) - jnp.log1p(jnp.exp(-jnp.abs(x)))


def _rms_proj_kernel(x_ref, g_ref, w_ref, o_ref, xn_ref):
    @pl.when(pl.program_id(1) == 0)
    def _():
        x = x_ref[...]
        var = jnp.mean(x * x, axis=-1, keepdims=True)
        xn_ref[...] = (x * lax.rsqrt(var + EPS) * g_ref[...]).astype(BF16)

    o_ref[...] = _dot(xn_ref[...], w_ref[...].astype(BF16))


def _rms_proj_gate_kernel(x_ref, g_ref, w_ref, wg_ref, o_ref, og_ref, xn_ref):
    @pl.when(pl.program_id(1) == 0)
    def _():
        x = x_ref[...]
        var = jnp.mean(x * x, axis=-1, keepdims=True)
        xn = (x * lax.rsqrt(var + EPS) * g_ref[...]).astype(BF16)
        xn_ref[...] = xn
        og_ref[...] = _dot(xn, wg_ref[...].astype(BF16))

    o_ref[...] = _dot(xn_ref[...], w_ref[...].astype(BF16))


def _rms_proj(x, g, w, layer, n_cols, tm, tn, wg=None):
    m = x.shape[0]
    grid = (m // tm, n_cols // tn)
    x_spec = pl.BlockSpec((tm, D_MODEL), lambda i, j: (i, 0))
    g_spec = pl.BlockSpec((None, 1, D_MODEL), lambda i, j: (layer, 0, 0))
    w_spec = pl.BlockSpec((None, D_MODEL, tn), lambda i, j: (layer, 0, j))
    o_spec = pl.BlockSpec((tm, tn), lambda i, j: (i, j))
    scratch = [pltpu.VMEM((tm, D_MODEL), BF16)]
    if wg is None:
        return pl.pallas_call(
            _rms_proj_kernel, grid=grid, in_specs=[x_spec, g_spec, w_spec], out_specs=o_spec,
            out_shape=jax.ShapeDtypeStruct((m, n_cols), F32), scratch_shapes=scratch,
            compiler_params=_params(2), name="rms_proj")(x, g, w)
    wg_spec = pl.BlockSpec((None, D_MODEL, GATE_PAD), lambda i, j: (layer, 0, 0))
    og_spec = pl.BlockSpec((tm, GATE_PAD), lambda i, j: (i, 0))
    return pl.pallas_call(
        _rms_proj_gate_kernel, grid=grid, in_specs=[x_spec, g_spec, w_spec, wg_spec],
        out_specs=[o_spec, og_spec],
        out_shape=[jax.ShapeDtypeStruct((m, n_cols), F32), jax.ShapeDtypeStruct((m, GATE_PAD), F32)],
        scratch_shapes=scratch, compiler_params=_params(2), name="rms_proj_gate")(x, g, w, wg)


def _out_proj_kernel(ya_ref, yb_ref, wa_ref, wb_ref, x_ref, o_ref):
    acc = _dot(ya_ref[...], wa_ref[...].astype(BF16))
    acc = acc + _dot(yb_ref[...], wb_ref[...].astype(BF16))
    o_ref[...] = x_ref[...] + acc


def _out_proj(ya, yb, w_out, x, layer, tm, tn):
    m = x.shape[0]
    grid = (D_MODEL // tn, m // tm)
    a_spec = pl.BlockSpec((tm, CONV_DIM), lambda j, i: (i, 0))
    wa_spec = pl.BlockSpec((None, CONV_DIM, tn), lambda j, i: (layer, 0, j))
    wb_spec = pl.BlockSpec((None, GLA_V_DIM, tn), lambda j, i: (layer, 1, j))
    x_spec = pl.BlockSpec((tm, tn), lambda j, i: (i, j))
    return pl.pallas_call(
        _out_proj_kernel, grid=grid, in_specs=[a_spec, a_spec, wa_spec, wb_spec, x_spec],
        out_specs=x_spec, out_shape=jax.ShapeDtypeStruct((m, D_MODEL), F32),
        compiler_params=_params(2), name="out_proj")(ya, yb, w_out, w_out, x)


def _down_proj_kernel(h_ref, w_ref, x_ref, o_ref):
    o_ref[...] = x_ref[...] + _dot(h_ref[...], w_ref[...].astype(BF16))


def _down_proj(h, w_down, x, layer, tm, tn):
    m = x.shape[0]
    grid = (D_MODEL // tn, m // tm)
    h_spec = pl.BlockSpec((tm, D_FF), lambda j, i: (i, 0))
    w_spec = pl.BlockSpec((None, D_FF, tn), lambda j, i: (layer, 0, j))
    x_spec = pl.BlockSpec((tm, tn), lambda j, i: (i, j))
    return pl.pallas_call(
        _down_proj_kernel, grid=grid, in_specs=[h_spec, w_spec, x_spec], out_specs=x_spec,
        out_shape=jax.ShapeDtypeStruct((m, D_MODEL), F32),
        compiler_params=_params(2), name="down_proj")(h, w_down, x)


def _final_norm_kernel(x_ref, g_ref, o_ref):
    x = x_ref[...]
    var = jnp.mean(x * x, axis=-1, keepdims=True)
    o_ref[...] = x * lax.rsqrt(var + EPS) * g_ref[...]


def _final_norm(x, g, tm):
    m = x.shape[0]
    x_spec = pl.BlockSpec((tm, D_MODEL), lambda i: (i, 0))
    g_spec = pl.BlockSpec((1, D_MODEL), lambda i: (0, 0))
    return pl.pallas_call(
        _final_norm_kernel, grid=(m // tm,), in_specs=[x_spec, g_spec], out_specs=x_spec,
        out_shape=jax.ShapeDtypeStruct((m, D_MODEL), F32),
        compiler_params=_params(1), name="final_norm")(x, g)


def _shifted_taps(z, zbuf_ref, t):
    tl = z.shape[0]

    @pl.when(t == 0)
    def _():
        zbuf_ref[0:8, :] = jnp.zeros((8, z.shape[1]), F32)

    zbuf_ref[8:8 + tl, :] = z
    z2 = zbuf_ref[6:6 + tl, :]
    z1 = zbuf_ref[7:7 + tl, :]
    zbuf_ref[0:8, :] = z[tl - 8:tl, :]
    return z2, z1


def _conv_mixer_kernel(bg_ref, cg_ref, h_ref, w_ref, ya_ref, st_ref, zbuf_ref):
    t = pl.program_id(2)
    z = cg_ref[...] * h_ref[...]
    tl = z.shape[0]
    z2, z1 = _shifted_taps(z, zbuf_ref, t)
    cu = z2 * w_ref[0:1, :]
    cu = cu + z1 * w_ref[1:2, :]
    cu = cu + z * w_ref[2:3, :]
    ya_ref[...] = (bg_ref[...] * cu).astype(BF16)

    @pl.when(t == pl.num_programs(2) - 1)
    def _():
        st_ref[...] = z[tl - (CONV_WIDTH - 1):tl, :]


def _conv_mixer_prompt(proj, conv_w, layer, batch, seq, tl, tc):
    nt, ncol = seq // tl, CONV_DIM // tc
    grid = (batch, ncol, nt)

    def col_spec(group):
        return pl.BlockSpec((tl, tc), lambda b, c, t: (b * nt + t, group * ncol + c))

    w_spec = pl.BlockSpec((None, CONV_WIDTH, tc), lambda b, c, t: (layer, 0, c))
    ya_spec = pl.BlockSpec((tl, tc), lambda b, c, t: (b * nt + t, c))
    st_spec = pl.BlockSpec((None, CONV_WIDTH - 1, tc), lambda b, c, t: (b, 0, c))
    return pl.pallas_call(
        _conv_mixer_kernel, grid=grid,
        in_specs=[col_spec(0), col_spec(1), col_spec(2), w_spec],
        out_specs=[ya_spec, st_spec],
        out_shape=[jax.ShapeDtypeStruct((batch * seq, CONV_DIM), BF16),
                   jax.ShapeDtypeStruct((batch, CONV_WIDTH - 1, CONV_DIM), F32)],
        scratch_shapes=[pltpu.VMEM((tl + 8, tc), F32)],
        compiler_params=_params(3), name="conv_mixer")(proj, proj, proj, conv_w)


def _conv_ffn_kernel(u_ref, v_ref, w_ref, b_ref, h_ref, st_ref, zbuf_ref):
    t = pl.program_id(2)
    u = u_ref[...]
    tl = u.shape[0]
    u2, u1 = _shifted_taps(u, zbuf_ref, t)
    cu = u2 * w_ref[0:1, :]
    cu = cu + u1 * w_ref[1:2, :]
    cu = cu + u * w_ref[2:3, :]
    pre = cu + b_ref[...]
    h_ref[...] = (pre * _sigmoid(pre) * v_ref[...]).astype(BF16)

    @pl.when(t == pl.num_programs(2) - 1)
    def _():
        st_ref[...] = u[tl - (CONV_WIDTH - 1):tl, :]


def _conv_ffn_prompt(uv, conv_w, conv_b, layer, batch, seq, tl, tc):
    nt, ncol = seq // tl, D_FF // tc
    grid = (batch, ncol, nt)
    u_spec = pl.BlockSpec((tl, tc), lambda b, c, t: (b * nt + t, c))
    v_spec = pl.BlockSpec((tl, tc), lambda b, c, t: (b * nt + t, ncol + c))
    w_spec = pl.BlockSpec((None, CONV_WIDTH, tc), lambda b, c, t: (layer, 0, c))
    b_spec = pl.BlockSpec((None, 1, tc), lambda b, c, t: (layer, 0, c))
    st_spec = pl.BlockSpec((None, CONV_WIDTH - 1, tc), lambda b, c, t: (b, 0, c))
    return pl.pallas_call(
        _conv_ffn_kernel, grid=grid, in_specs=[u_spec, v_spec, w_spec, b_spec],
        out_specs=[u_spec, st_spec],
        out_shape=[jax.ShapeDtypeStruct((batch * seq, D_FF), BF16),
                   jax.ShapeDtypeStruct((batch, CONV_WIDTH - 1, D_FF), F32)],
        scratch_shapes=[pltpu.VMEM((tl + 8, tc), F32)],
        compiler_params=_params(3), name="conv_ffn")(uv, uv, conv_w, conv_b)


def _conv_mixer_step_kernel(bg_ref, cg_ref, h_ref, p0_ref, p1_ref, w_ref, ya_ref, n0_ref, n1_ref):
    z = cg_ref[...] * h_ref[...]
    p1 = p1_ref[...]
    cu = p0_ref[...] * w_ref[0:1, :]
    cu = cu + p1 * w_ref[1:2, :]
    cu = cu + z * w_ref[2:3, :]
    ya_ref[...] = (bg_ref[...] * cu).astype(BF16)
    n0_ref[...] = p1
    n1_ref[...] = z


def _conv_mixer_step(proj, state, conv_w, layer, tc):
    nb = proj.shape[0]
    ncol = CONV_DIM // tc

    def col_spec(group):
        return pl.BlockSpec((nb, tc), lambda c: (0, group * ncol + c))

    def st_spec(row):
        return pl.BlockSpec((None, nb, tc), lambda c: (layer, 0, row * ncol + c))

    w_spec = pl.BlockSpec((None, CONV_WIDTH, tc), lambda c: (layer, 0, c))
    return pl.pallas_call(
        _conv_mixer_step_kernel, grid=(ncol,),
        in_specs=[col_spec(0), col_spec(1), col_spec(2), st_spec(0), st_spec(1), w_spec],
        out_specs=[col_spec(0), col_spec(0), col_spec(0)],
        out_shape=[jax.ShapeDtypeStruct((nb, CONV_DIM), BF16),
                   jax.ShapeDtypeStruct((nb, CONV_DIM), F32),
                   jax.ShapeDtypeStruct((nb, CONV_DIM), F32)],
        compiler_params=_params(1), name="conv_mixer_step")(proj, proj, proj, state, state, conv_w)


def _conv_ffn_step_kernel(u_ref, v_ref, p0_ref, p1_ref, w_ref, b_ref, h_ref, n0_ref, n1_ref):
    u = u_ref[...]
    p1 = p1_ref[...]
    cu = p0_ref[...] * w_ref[0:1, :]
    cu = cu + p1 * w_ref[1:2, :]
    cu = cu + u * w_ref[2:3, :]
    pre = cu + b_ref[...]
    h_ref[...] = (pre * _sigmoid(pre) * v_ref[...]).astype(BF16)
    n0_ref[...] = p1
    n1_ref[...] = u


def _conv_ffn_step(uv, state, conv_w, conv_b, layer, tc):
    nb = uv.shape[0]
    ncol = D_FF // tc

    def col_spec(group):
        return pl.BlockSpec((nb, tc), lambda c: (0, group * ncol + c))

    def st_spec(row):
        return pl.BlockSpec((None, nb, tc), lambda c: (layer, 0, row * ncol + c))

    w_spec = pl.BlockSpec((None, CONV_WIDTH, tc), lambda c: (layer, 0, c))
    b_spec = pl.BlockSpec((None, 1, tc), lambda c: (layer, 0, c))
    return pl.pallas_call(
        _conv_ffn_step_kernel, grid=(ncol,),
        in_specs=[col_spec(0), col_spec(1), st_spec(0), st_spec(1), w_spec, b_spec],
        out_specs=[col_spec(0), col_spec(0), col_spec(0)],
        out_shape=[jax.ShapeDtypeStruct((nb, D_FF), BF16),
                   jax.ShapeDtypeStruct((nb, D_FF), F32),
                   jax.ShapeDtypeStruct((nb, D_FF), F32)],
        compiler_params=_params(1), name="conv_ffn_step")(uv, uv, state, state, conv_w, conv_b)


def _gla_constants(chunk):
    rows = np.arange(chunk)
    tri = (rows[:, None] >= rows[None, :]).astype(np.float32)
    sels = []
    for s in GLA_LEVELS:
        ref_row = (rows // (2 * s)) * (2 * s) + s - 1
        sels.append((ref_row[:, None] == rows[None, :]).astype(np.float32))
    return jnp.asarray(tri, BF16), jnp.asarray(np.concatenate(sels, axis=0), BF16)


def _gla_out(o, gn, og):
    var = jnp.mean(o * o, axis=-1, keepdims=True)
    return o * lax.rsqrt(var + EPS) * gn * (og * _sigmoid(og))


def _gla_prompt_kernel(q_ref, k_ref, v_ref, og_ref, glr_ref, w2_ref, gb_ref, gn_ref, tri_ref, sel_ref,
                       yb_ref, sfin_ref, s_ref):
    c_idx = pl.program_id(1)
    chunk = q_ref.shape[0]

    @pl.when(c_idx == 0)
    def _():
        s_ref[...] = jnp.zeros(s_ref.shape, F32)

    gate = _dot(glr_ref[...].astype(BF16), w2_ref[...].astype(BF16)) + gb_ref[...]
    g_all = _log_sigmoid(gate) * (1.0 / GATE_TAU)
    tri = tri_ref[...]
    g1, g2, g3 = _split3(g_all)
    b_all = _dot(tri, g1) + _dot(tri, g2) + _dot(tri, g3)
    sel = sel_ref[...]
    b_hi = b_all.astype(BF16)
    b_lo = (b_all - b_hi.astype(F32)).astype(BF16)
    beta_all = _dot(sel, b_hi) + _dot(sel, b_lo)

    row = lax.broadcasted_iota(jnp.int32, (chunk, chunk), 0)
    col = lax.broadcasted_iota(jnp.int32, (chunk, chunk), 1)
    diff_bits = row ^ col
    row_k = lax.broadcasted_iota(jnp.int32, (chunk, GLA_DK), 0)
    scale = GLA_DK ** -0.5

    for h in range(GLA_HEADS):
        ks = slice(h * GLA_DK, (h + 1) * GLA_DK)
        vs = slice(h * GLA_DV, (h + 1) * GLA_DV)
        b = b_all[:, ks]
        b_last = b[chunk - 1:chunk, :]
        q = q_ref[:, ks] * scale
        k = k_ref[:, ks]
        v = v_ref[:, vs].astype(BF16)
        s_prev = s_ref[h]

        inter = _dot((q * jnp.exp(b)).astype(BF16), s_prev.astype(BF16))

        acc = _dot_nt(q.astype(BF16), k.astype(BF16))
        for li, s in enumerate(GLA_LEVELS):
            beta = beta_all[li * chunk:(li + 1) * chunk, ks]
            d = b - beta
            arg = jnp.where((row_k & s) != 0, d, -d)
            e = jnp.exp(jnp.minimum(arg, 0.0))
            p = _dot_nt((q * e).astype(BF16), (k * e).astype(BF16))
            acc = jnp.where(diff_bits >= s, p, acc)
        a = jnp.where(row >= col, acc, 0.0)
        o = inter + _dot(a.astype(BF16), v)

        kdec_t = (k * jnp.exp(b_last - b)).T
        decay_t = jnp.broadcast_to(jnp.exp(b_last), (GLA_DK, GLA_DK)).T
        s_new = jnp.concatenate([decay_t, decay_t], axis=1) * s_prev + _dot(kdec_t.astype(BF16), v)
        s_ref[h] = s_new

        yb_ref[:, vs] = _gla_out(o, gn_ref[:, vs], og_ref[:, vs]).astype(BF16)

    @pl.when(c_idx == pl.num_programs(1) - 1)
    def _():
        sfin_ref[...] = s_ref[...]


def _gla_prompt(proj, glr, gate_w2p, gate_b, gla_norm_g, layer, batch, seq):
    chunk = GLA_CHUNK
    nc = seq // chunk
    tri, sel = _gla_constants(chunk)
    q_blk = (3 * CONV_DIM) // GLA_QK_DIM
    v_blk = (3 * CONV_DIM + 2 * GLA_QK_DIM) // GLA_V_DIM

    def col_spec(width, blk):
        return pl.BlockSpec((chunk, width), lambda b, c: (b * nc + c, blk))

    def whole(shape):
        return pl.BlockSpec((None,) + shape, lambda b, c: (layer,) + (0,) * len(shape))

    const_spec = lambda arr: pl.BlockSpec(arr.shape, lambda b, c: (0, 0))
    yb_spec = pl.BlockSpec((chunk, GLA_V_DIM), lambda b, c: (b * nc + c, 0))
    s_spec = pl.BlockSpec((None, GLA_HEADS, GLA_DK, GLA_DV), lambda b, c: (b, 0, 0, 0))
    return pl.pallas_call(
        _gla_prompt_kernel, grid=(batch, nc),
        in_specs=[col_spec(GLA_QK_DIM, q_blk), col_spec(GLA_QK_DIM, q_blk + 1),
                  col_spec(GLA_V_DIM, v_blk), col_spec(GLA_V_DIM, v_blk + 1),
                  col_spec(GATE_PAD, 0), whole((GATE_PAD, GLA_QK_DIM)), whole((1, GLA_QK_DIM)),
                  whole((1, GLA_V_DIM)), const_spec(tri), const_spec(sel)],
        out_specs=[yb_spec, s_spec],
        out_shape=[jax.ShapeDtypeStruct((batch * seq, GLA_V_DIM), BF16),
                   jax.ShapeDtypeStruct((batch, GLA_HEADS, GLA_DK, GLA_DV), F32)],
        scratch_shapes=[pltpu.VMEM((GLA_HEADS, GLA_DK, GLA_DV), F32)],
        compiler_params=_params(2), name="gla_prompt")(
            proj, proj, proj, proj, glr, gate_w2p, gate_b, gla_norm_g, tri, sel)


GLA_STEP_ROWS = 8


def _columns(slab):
    pad = jnp.zeros((GLA_DK - GLA_STEP_ROWS, GLA_DK), F32)
    return jnp.concatenate([slab, pad], axis=0).T


def _gla_step_kernel(q_ref, k_ref, v_ref, og_ref, glr_ref, w2_ref, gb_ref, gn_ref, s_ref,
                     yb_ref, snew_ref):
    gate = _dot(glr_ref[...].astype(BF16), w2_ref[...].astype(BF16)) + gb_ref[...]
    g = _log_sigmoid(gate) * (1.0 / GATE_TAU)
    decay = jnp.exp(g)
    q = q_ref[...] * (GLA_DK ** -0.5)
    k = k_ref[...]
    v = v_ref[...]
    qk = jnp.sum(q * k, axis=-1, keepdims=True)
    decay_c = _columns(decay)
    k_c = _columns(k)
    qd_c = _columns(q * decay)
    rows = []
    for i in range(GLA_STEP_ROWS):
        s_prev = s_ref[i]
        inter = jnp.sum(qd_c[:, i:i + 1] * s_prev, axis=0, keepdims=True)
        rows.append(inter + qk[i:i + 1, :] * v[i:i + 1, :])
        snew_ref[i] = decay_c[:, i:i + 1] * s_prev + k_c[:, i:i + 1] * v[i:i + 1, :]
    o = jnp.concatenate(rows, axis=0)
    yb_ref[...] = _gla_out(o, gn_ref[...], og_ref[...]).astype(BF16)


def _gla_step(proj, glr, state, gate_w2p, gate_b, gla_norm_g, layer):
    nb = proj.shape[0]
    rows = GLA_STEP_ROWS
    q_blk = (3 * CONV_DIM) // GLA_DK
    k_blk = q_blk + GLA_HEADS
    v_blk = (3 * CONV_DIM + 2 * GLA_QK_DIM) // GLA_DV
    og_blk = v_blk + GLA_HEADS
    s_spec_in = pl.BlockSpec((None, rows, None, GLA_DK, GLA_DV), lambda t, h: (layer, t, h, 0, 0))
    s_spec_out = pl.BlockSpec((rows, None, GLA_DK, GLA_DV), lambda t, h: (t, h, 0, 0))
    return pl.pallas_call(
        _gla_step_kernel, grid=(nb // rows, GLA_HEADS),
        in_specs=[pl.BlockSpec((rows, GLA_DK), lambda t, h: (t, q_blk + h)),
                  pl.BlockSpec((rows, GLA_DK), lambda t, h: (t, k_blk + h)),
                  pl.BlockSpec((rows, GLA_DV), lambda t, h: (t, v_blk + h)),
                  pl.BlockSpec((rows, GLA_DV), lambda t, h: (t, og_blk + h)),
                  pl.BlockSpec((rows, GATE_PAD), lambda t, h: (t, 0)),
                  pl.BlockSpec((None, GATE_PAD, GLA_DK), lambda t, h: (layer, 0, h)),
                  pl.BlockSpec((None, 1, GLA_DK), lambda t, h: (layer, 0, h)),
                  pl.BlockSpec((None, 1, GLA_DV), lambda t, h: (layer, 0, h)),
                  s_spec_in],
        out_specs=[pl.BlockSpec((rows, GLA_DV), lambda t, h: (t, h)), s_spec_out],
        out_shape=[jax.ShapeDtypeStruct((nb, GLA_V_DIM), BF16),
                   jax.ShapeDtypeStruct((nb, GLA_HEADS, GLA_DK, GLA_DV), F32)],
        compiler_params=_params(2), name="gla_step")(
            proj, proj, proj, proj, glr, gate_w2p, gate_b, gla_norm_g, state)


def _prepare(norm_mix_g, w_in, gate_w2, gate_b, gla_norm_g, norm_ffn_g, ffn_conv_b, final_norm_g):
    w_gate = jnp.pad(w_in[:, :, MAIN_COLS:], ((0, 0), (0, 0), (0, GATE_PAD - GATE_RANK)))
    gate_w2p = jnp.pad(gate_w2, ((0, 0), (0, GATE_PAD - GATE_RANK), (0, 0)))
    return dict(
        norm_mix_g=norm_mix_g[:, None, :], norm_ffn_g=norm_ffn_g[:, None, :], w_gate=w_gate,
        gate_w2p=gate_w2p, gate_b=gate_b[:, None, :], gla_norm_g=gla_norm_g[:, None, :],
        ffn_conv_b=ffn_conv_b[:, None, :], final_norm_g=final_norm_g[None, :])


def _prompt_trunk(x, p, w_in, conv_w, w_out, w_up, ffn_conv_w, w_down, batch, seq):
    convs, glas, ffns = [], [], []
    for l in range(DEPTH):
        proj, glr = _rms_proj(x, p["norm_mix_g"], w_in, l, MAIN_COLS, 1024, 512, wg=p["w_gate"])
        ya, c_new = _conv_mixer_prompt(proj, conv_w, l, batch, seq, 512, 512)
        yb, s_new = _gla_prompt(proj, glr, p["gate_w2p"], p["gate_b"], p["gla_norm_g"], l, batch, seq)
        x = _out_proj(ya, yb, w_out, x, l, 1024, 512)
        uv = _rms_proj(x, p["norm_ffn_g"], w_up, l, 2 * D_FF, 1024, 512)
        h, f_new = _conv_ffn_prompt(uv, ffn_conv_w, p["ffn_conv_b"], l, batch, seq, 512, 512)
        x = _down_proj(h, w_down, x, l, 512, 512)
        convs.append(c_new)
        glas.append(s_new)
        ffns.append(f_new)
    y = _final_norm(x, p["final_norm_g"], 512)
    return y, jnp.stack(convs), jnp.stack(glas), jnp.stack(ffns)


def _sample_trunk(x, state_conv, state_gla, state_ffn, p, w_in, conv_w, w_out, w_up, ffn_conv_w, w_down):
    nb = x.shape[0]
    convs, glas, ffns = [], [], []
    for l in range(DEPTH):
        proj, glr = _rms_proj(x, p["norm_mix_g"], w_in, l, MAIN_COLS, nb, 512, wg=p["w_gate"])
        ya, c0, c1 = _conv_mixer_step(proj, state_conv, conv_w, l, 512)
        yb, s_new = _gla_step(proj, glr, state_gla, p["gate_w2p"], p["gate_b"], p["gla_norm_g"], l)
        x = _out_proj(ya, yb, w_out, x, l, nb, 512)
        uv = _rms_proj(x, p["norm_ffn_g"], w_up, l, 2 * D_FF, nb, 512)
        h, f0, f1 = _conv_ffn_step(uv, state_ffn, ffn_conv_w, p["ffn_conv_b"], l, 512)
        x = _down_proj(h, w_down, x, l, nb, 512)
        convs.append((c0, c1))
        glas.append(s_new)
        ffns.append((f0, f1))
    y = _final_norm(x, p["final_norm_g"], nb)
    return y, convs, glas, ffns


def kernel(x_prompt, x_sample, state_conv, state_gla, state_ffn_conv, norm_mix_g, w_in, conv_w, gate_w2, gate_b, gla_norm_g, w_out, norm_ffn_g, w_up, ffn_conv_w, ffn_conv_b, w_down, final_norm_g):
    batch, seq, _ = x_prompt.shape
    nb = x_sample.shape[0]
    p = _prepare(norm_mix_g, w_in, gate_w2, gate_b, gla_norm_g, norm_ffn_g, ffn_conv_b, final_norm_g)

    y_p, conv_p, gla_p, ffn_p = _prompt_trunk(
        x_prompt.reshape(batch * seq, D_MODEL), p, w_in, conv_w, w_out, w_up, ffn_conv_w, w_down,
        batch, seq)

    y_s, convs, glas, ffns = _sample_trunk(
        x_sample.reshape(nb, D_MODEL),
        state_conv.reshape(DEPTH, nb, (CONV_WIDTH - 1) * CONV_DIM), state_gla,
        state_ffn_conv.reshape(DEPTH, nb, (CONV_WIDTH - 1) * D_FF),
        p, w_in, conv_w, w_out, w_up, ffn_conv_w, w_down)

    def restack(pairs):
        return jnp.stack([jnp.stack(pair, axis=1) for pair in pairs])

    return (y_p.reshape(batch, seq, D_MODEL), y_s.reshape(nb, 1, D_MODEL), conv_p, gla_p, ffn_p,
            restack(convs), jnp.stack(glas), restack(ffns))
```

```python
import functools

import numpy as np
import jax
import jax.numpy as jnp
from jax import lax
from jax.experimental import pallas as pl
from jax.experimental.pallas import tpu as pltpu

F32 = jnp.float32
BF16 = jnp.bfloat16

D_MODEL = 2048
DEPTH = 2
CONV_DIM = 1024
CONV_WIDTH = 3
GLA_HEADS = 4
GLA_QK_DIM = 512
GLA_V_DIM = 1024
GLA_DK = 128
GLA_DV = 256
GATE_RANK = 16
GATE_TAU = 16.0
D_FF = 5632
EPS = 1e-6
MAIN_COLS = 3 * CONV_DIM + 2 * GLA_QK_DIM + 2 * GLA_V_DIM

GLA_CHUNK = 128
GLA_LEVELS = tuple(1 << i for i in range(GLA_CHUNK.bit_length() - 1))

VMEM_LIMIT = 56 * 1024 * 1024


def _params(n_axes):
    return pltpu.CompilerParams(dimension_semantics=("arbitrary",) * n_axes,
                                vmem_limit_bytes=VMEM_LIMIT)


def _dot(a, b):
    return jnp.dot(a, b, preferred_element_type=F32)


def _dot_nt(a, b):
    return lax.dot_general(a, b, (((1,), (1,)), ((), ())), preferred_element_type=F32)


def _split3(x):
    hi = x.astype(BF16)
    r1 = x - hi.astype(F32)
    mid = r1.astype(BF16)
    lo = (r1 - mid.astype(F32)).astype(BF16)
    return hi, mid, lo


def _sigmoid(x):
    return 1.0 / (1.0 + jnp.exp(-x))


def _log_sigmoid(x):
    return jnp.minimum(x, 0.0) - jnp.log1p(jnp.exp(-jnp.abs(x)))


def _rms_proj_kernel(x_ref, g_ref, w_ref, o_ref, xn_ref):
    @pl.when(pl.program_id(1) == 0)
    def _():
        x = x_ref[...]
        var = jnp.mean(x * x, axis=-1, keepdims=True)
        xn_ref[...] = (x * lax.rsqrt(var + EPS) * g_ref[...]).astype(BF16)

    o_ref[...] = _dot(xn_ref[...], w_ref[...].astype(BF16))


def _in_proj_kernel(x_ref, g_ref, wt_ref, wgt_ref, o_ref, og_ref, xn_ref):
    @pl.when(pl.program_id(1) == 0)
    def _():
        x = x_ref[...]
        var = jnp.mean(x * x, axis=-1, keepdims=True)
        xn = (x * lax.rsqrt(var + EPS) * g_ref[...]).astype(BF16)
        xn_ref[...] = xn
        og_ref[...] = _dot_nt(xn, wgt_ref[...].astype(BF16))

    o_ref[...] = _dot_nt(xn_ref[...], wt_ref[...].astype(BF16))


def _up_proj(x, g, w, layer, tm, tn):
    m = x.shape[0]
    x_spec = pl.BlockSpec((tm, D_MODEL), lambda i, j: (i, 0))
    g_spec = pl.BlockSpec((None, 1, D_MODEL), lambda i, j: (layer, 0, 0))
    w_spec = pl.BlockSpec((None, D_MODEL, tn), lambda i, j: (layer, 0, j))
    o_spec = pl.BlockSpec((tm, tn), lambda i, j: (i, j))
    return pl.pallas_call(
        _rms_proj_kernel, grid=(m // tm, 2 * D_FF // tn), in_specs=[x_spec, g_spec, w_spec],
        out_specs=o_spec, out_shape=jax.ShapeDtypeStruct((m, 2 * D_FF), F32),
        scratch_shapes=[pltpu.VMEM((tm, D_MODEL), BF16)],
        compiler_params=_params(2), name="up_proj")(x, g, w)


def _in_proj(x, g, w_in_t, layer, tm, tn):
    m = x.shape[0]
    x_spec = pl.BlockSpec((tm, D_MODEL), lambda i, j: (i, 0))
    g_spec = pl.BlockSpec((None, 1, D_MODEL), lambda i, j: (layer, 0, 0))
    wt_spec = pl.BlockSpec((None, tn, D_MODEL), lambda i, j: (layer, j, 0))
    wgt_spec = pl.BlockSpec((None, GATE_RANK, D_MODEL), lambda i, j: (layer, MAIN_COLS // GATE_RANK, 0))
    o_spec = pl.BlockSpec((tm, tn), lambda i, j: (i, j))
    og_spec = pl.BlockSpec((tm, GATE_RANK), lambda i, j: (i, 0))
    return pl.pallas_call(
        _in_proj_kernel, grid=(m // tm, MAIN_COLS // tn), in_specs=[x_spec, g_spec, wt_spec, wgt_spec],
        out_specs=[o_spec, og_spec],
        out_shape=[jax.ShapeDtypeStruct((m, MAIN_COLS), F32), jax.ShapeDtypeStruct((m, GATE_RANK), F32)],
        scratch_shapes=[pltpu.VMEM((tm, D_MODEL), BF16)],
        compiler_params=_params(2), name="in_proj")(x, g, w_in_t, w_in_t)


def _out_proj_kernel(ya_ref, yb_ref, wa_ref, wb_ref, x_ref, o_ref):
    acc = _dot(ya_ref[...], wa_ref[...].astype(BF16))
    acc = acc + _dot(yb_ref[...], wb_ref[...].astype(BF16))
    o_ref[...] = x_ref[...] + acc


def _out_proj(ya, yb, w_out, x, layer, tm, tn):
    m = x.shape[0]
    grid = (D_MODEL // tn, m // tm)
    a_spec = pl.BlockSpec((tm, CONV_DIM), lambda j, i: (i, 0))
    wa_spec = pl.BlockSpec((None, CONV_DIM, tn), lambda j, i: (layer, 0, j))
    wb_spec = pl.BlockSpec((None, GLA_V_DIM, tn), lambda j, i: (layer, 1, j))
    x_spec = pl.BlockSpec((tm, tn), lambda j, i: (i, j))
    return pl.pallas_call(
        _out_proj_kernel, grid=grid, in_specs=[a_spec, a_spec, wa_spec, wb_spec, x_spec],
        out_specs=x_spec, out_shape=jax.ShapeDtypeStruct((m, D_MODEL), F32),
        compiler_params=_params(2), name="out_proj")(ya, yb, w_out, w_out, x)


def _down_proj_kernel(h_ref, w_ref, x_ref, o_ref):
    o_ref[...] = x_ref[...] + _dot(h_ref[...], w_ref[...].astype(BF16))


def _down_proj(h, w_down, x, layer, tm, tn):
    m = x.shape[0]
    grid = (D_MODEL // tn, m // tm)
    h_spec = pl.BlockSpec((tm, D_FF), lambda j, i: (i, 0))
    w_spec = pl.BlockSpec((None, D_FF, tn), lambda j, i: (layer, 0, j))
    x_spec = pl.BlockSpec((tm, tn), lambda j, i: (i, j))
    return pl.pallas_call(
        _down_proj_kernel, grid=grid, in_specs=[h_spec, w_spec, x_spec], out_specs=x_spec,
        out_shape=jax.ShapeDtypeStruct((m, D_MODEL), F32),
        compiler_params=_params(2), name="down_proj")(h, w_down, x)


def _final_norm_kernel(x_ref, g_ref, o_ref):
    x = x_ref[...]
    var = jnp.mean(x * x, axis=-1, keepdims=True)
    o_ref[...] = x * lax.rsqrt(var + EPS) * g_ref[...]


def _final_norm(x, g, tm):
    m = x.shape[0]
    x_spec = pl.BlockSpec((tm, D_MODEL), lambda i: (i, 0))
    g_spec = pl.BlockSpec((1, D_MODEL), lambda i: (0, 0))
    return pl.pallas_call(
        _final_norm_kernel, grid=(m // tm,), in_specs=[x_spec, g_spec], out_specs=x_spec,
        out_shape=jax.ShapeDtypeStruct((m, D_MODEL), F32),
        compiler_params=_params(1), name="final_norm")(x, g)


def _shifted_taps(z, zbuf_ref, t):
    tl = z.shape[0]

    @pl.when(t == 0)
    def _():
        zbuf_ref[0:8, :] = jnp.zeros((8, z.shape[1]), F32)

    zbuf_ref[8:8 + tl, :] = z
    z2 = zbuf_ref[6:6 + tl, :]
    z1 = zbuf_ref[7:7 + tl, :]
    zbuf_ref[0:8, :] = z[tl - 8:tl, :]
    return z2, z1


def _conv_mixer_kernel(bg_ref, cg_ref, h_ref, w_ref, ya_ref, st_ref, zbuf_ref):
    t = pl.program_id(2)
    z = cg_ref[...] * h_ref[...]
    tl = z.shape[0]
    z2, z1 = _shifted_taps(z, zbuf_ref, t)
    cu = z2 * w_ref[0:1, :]
    cu = cu + z1 * w_ref[1:2, :]
    cu = cu + z * w_ref[2:3, :]
    ya_ref[...] = (bg_ref[...] * cu).astype(BF16)

    @pl.when(t == pl.num_programs(2) - 1)
    def _():
        st_ref[...] = z[tl - (CONV_WIDTH - 1):tl, :]


def _conv_mixer_prompt(proj, conv_w, layer, batch, seq, tl, tc):
    nt, ncol = seq // tl, CONV_DIM // tc
    grid = (batch, ncol, nt)

    def col_spec(group):
        return pl.BlockSpec((tl, tc), lambda b, c, t: (b * nt + t, group * ncol + c))

    w_spec = pl.BlockSpec((None, CONV_WIDTH, tc), lambda b, c, t: (layer, 0, c))
    ya_spec = pl.BlockSpec((tl, tc), lambda b, c, t: (b * nt + t, c))
    st_spec = pl.BlockSpec((None, CONV_WIDTH - 1, tc), lambda b, c, t: (b, 0, c))
    return pl.pallas_call(
        _conv_mixer_kernel, grid=grid,
        in_specs=[col_spec(0), col_spec(1), col_spec(2), w_spec],
        out_specs=[ya_spec, st_spec],
        out_shape=[jax.ShapeDtypeStruct((batch * seq, CONV_DIM), BF16),
                   jax.ShapeDtypeStruct((batch, CONV_WIDTH - 1, CONV_DIM), F32)],
        scratch_shapes=[pltpu.VMEM((tl + 8, tc), F32)],
        compiler_params=_params(3), name="conv_mixer")(proj, proj, proj, conv_w)


def _ffn_prompt_kernel(x_ref, g_ref, wu_ref, wv_ref, cw_ref, cb_ref, wd_ref, fg_ref,
                       o_ref, st_ref, xn_ref, zbuf_ref, carry_ref, *, tiles_per_seq, final_norm):
    i = pl.program_id(0)
    f = pl.program_id(1)
    tm = x_ref.shape[0]

    @pl.when(f == 0)
    def _():
        x = x_ref[...]
        var = jnp.mean(x * x, axis=-1, keepdims=True)
        xn_ref[...] = (x * lax.rsqrt(var + EPS) * g_ref[...]).astype(BF16)
        o_ref[...] = x

    first = lax.rem(i, tiles_per_seq) == 0

    @pl.when(first)
    def _():
        zbuf_ref[0:8, :] = jnp.zeros((8, zbuf_ref.shape[1]), F32)

    @pl.when(jnp.logical_not(first))
    def _():
        zbuf_ref[0:8, :] = carry_ref[f]

    xn = xn_ref[...]
    u = _dot(xn, wu_ref[...].astype(BF16))
    v = _dot(xn, wv_ref[...].astype(BF16))
    zbuf_ref[8:8 + tm, :] = u
    cu = zbuf_ref[6:6 + tm, :] * cw_ref[0:1, :]
    cu = cu + zbuf_ref[7:7 + tm, :] * cw_ref[1:2, :]
    cu = cu + u * cw_ref[2:3, :]
    carry_ref[f] = u[tm - 8:tm, :]
    st_ref[...] = u[tm - (CONV_WIDTH - 1):tm, :]

    pre = cu + cb_ref[...]
    h = (pre * _sigmoid(pre) * v).astype(BF16)
    o_ref[...] += _dot(h, wd_ref[...].astype(BF16))

    if final_norm:
        @pl.when(f == pl.num_programs(1) - 1)
        def _():
            y = o_ref[...]
            var = jnp.mean(y * y, axis=-1, keepdims=True)
            o_ref[...] = y * lax.rsqrt(var + EPS) * fg_ref[...]


def _ffn_prompt(x, g, w_up, conv_w, conv_b, w_down, final_g, layer, batch, seq, tm, tf, final_norm):
    m = x.shape[0]
    tiles_per_seq = seq // tm
    nf = D_FF // tf
    row_spec = pl.BlockSpec((tm, D_MODEL), lambda i, f: (i, 0), pipeline_mode=pl.Buffered(1))
    g_spec = pl.BlockSpec((None, 1, D_MODEL), lambda i, f: (layer, 0, 0))
    wu_spec = pl.BlockSpec((None, D_MODEL, tf), lambda i, f: (layer, 0, f))
    wv_spec = pl.BlockSpec((None, D_MODEL, tf), lambda i, f: (layer, 0, nf + f))
    cw_spec = pl.BlockSpec((None, CONV_WIDTH, tf), lambda i, f: (layer, 0, f))
    cb_spec = pl.BlockSpec((None, 1, tf), lambda i, f: (layer, 0, f))
    wd_spec = pl.BlockSpec((None, tf, D_MODEL), lambda i, f: (layer, f, 0))
    fg_spec = pl.BlockSpec((1, D_MODEL), lambda i, f: (0, 0))
    st_spec = pl.BlockSpec((None, CONV_WIDTH - 1, tf), lambda i, f: (i, 0, f))
    body = functools.partial(_ffn_prompt_kernel, tiles_per_seq=tiles_per_seq, final_norm=final_norm)
    out, tails = pl.pallas_call(
        body, grid=(m // tm, nf),
        in_specs=[row_spec, g_spec, wu_spec, wv_spec, cw_spec, cb_spec, wd_spec, fg_spec],
        out_specs=[row_spec, st_spec],
        out_shape=[jax.ShapeDtypeStruct((m, D_MODEL), F32),
                   jax.ShapeDtypeStruct((m // tm, CONV_WIDTH - 1, D_FF), F32)],
        scratch_shapes=[pltpu.VMEM((tm, D_MODEL), BF16), pltpu.VMEM((tm + 8, tf), F32),
                        pltpu.VMEM((nf, 8, tf), F32)],
        compiler_params=_params(2), name="ffn_prompt")(
            x, g, w_up, w_up, conv_w, conv_b, w_down, final_g)
    return out, tails[tiles_per_seq - 1::tiles_per_seq]


def _conv_mixer_step_kernel(bg_ref, cg_ref, h_ref, p0_ref, p1_ref, w_ref, ya_ref, n0_ref, n1_ref):
    z = cg_ref[...] * h_ref[...]
    p1 = p1_ref[...]
    cu = p0_ref[...] * w_ref[0:1, :]
    cu = cu + p1 * w_ref[1:2, :]
    cu = cu + z * w_ref[2:3, :]
    ya_ref[...] = (bg_ref[...] * cu).astype(BF16)
    n0_ref[...] = p1
    n1_ref[...] = z


def _conv_mixer_step(proj, state, conv_w, layer, tc):
    nb = proj.shape[0]
    ncol = CONV_DIM // tc

    def col_spec(group):
        return pl.BlockSpec((nb, tc), lambda c: (0, group * ncol + c))

    def st_spec(row):
        return pl.BlockSpec((None, nb, tc), lambda c: (layer, 0, row * ncol + c))

    w_spec = pl.BlockSpec((None, CONV_WIDTH, tc), lambda c: (layer, 0, c))
    return pl.pallas_call(
        _conv_mixer_step_kernel, grid=(ncol,),
        in_specs=[col_spec(0), col_spec(1), col_spec(2), st_spec(0), st_spec(1), w_spec],
        out_specs=[col_spec(0), col_spec(0), col_spec(0)],
        out_shape=[jax.ShapeDtypeStruct((nb, CONV_DIM), BF16),
                   jax.ShapeDtypeStruct((nb, CONV_DIM), F32),
                   jax.ShapeDtypeStruct((nb, CONV_DIM), F32)],
        compiler_params=_params(1), name="conv_mixer_step")(proj, proj, proj, state, state, conv_w)


def _conv_ffn_step_kernel(u_ref, v_ref, p0_ref, p1_ref, w_ref, b_ref, h_ref, n0_ref, n1_ref):
    u = u_ref[...]
    p1 = p1_ref[...]
    cu = p0_ref[...] * w_ref[0:1, :]
    cu = cu + p1 * w_ref[1:2, :]
    cu = cu + u * w_ref[2:3, :]
    pre = cu + b_ref[...]
    h_ref[...] = (pre * _sigmoid(pre) * v_ref[...]).astype(BF16)
    n0_ref[...] = p1
    n1_ref[...] = u


def _conv_ffn_step(uv, state, conv_w, conv_b, layer, tc):
    nb = uv.shape[0]
    ncol = D_FF // tc

    def col_spec(group):
        return pl.BlockSpec((nb, tc), lambda c: (0, group * ncol + c))

    def st_spec(row):
        return pl.BlockSpec((None, nb, tc), lambda c: (layer, 0, row * ncol + c))

    w_spec = pl.BlockSpec((None, CONV_WIDTH, tc), lambda c: (layer, 0, c))
    b_spec = pl.BlockSpec((None, 1, tc), lambda c: (layer, 0, c))
    return pl.pallas_call(
        _conv_ffn_step_kernel, grid=(ncol,),
        in_specs=[col_spec(0), col_spec(1), st_spec(0), st_spec(1), w_spec, b_spec],
        out_specs=[col_spec(0), col_spec(0), col_spec(0)],
        out_shape=[jax.ShapeDtypeStruct((nb, D_FF), BF16),
                   jax.ShapeDtypeStruct((nb, D_FF), F32),
                   jax.ShapeDtypeStruct((nb, D_FF), F32)],
        compiler_params=_params(1), name="conv_ffn_step")(uv, uv, state, state, conv_w, conv_b)


def _gla_constants(chunk):
    rows = np.arange(chunk)
    tri = (rows[:, None] >= rows[None, :]).astype(np.float32)
    sels = []
    for s in GLA_LEVELS:
        ref_row = (rows // (2 * s)) * (2 * s) + s - 1
        sels.append((ref_row[:, None] == rows[None, :]).astype(np.float32))
    return jnp.asarray(tri, BF16), jnp.asarray(np.concatenate(sels, axis=0), BF16)


def _gla_out(o, gn, og):
    var = jnp.mean(o * o, axis=-1, keepdims=True)
    return o * lax.rsqrt(var + EPS) * gn * (og * _sigmoid(og))


def _gla_prompt_kernel(q_ref, k_ref, v_ref, og_ref, glr_ref, w2_ref, gb_ref, gn_ref, tri_ref, sel_ref,
                       yb_ref, sfin_ref, s_ref):
    c_idx = pl.program_id(1)
    chunk = q_ref.shape[0]

    @pl.when(c_idx == 0)
    def _():
        s_ref[...] = jnp.zeros(s_ref.shape, F32)

    gate = _dot(glr_ref[...].astype(BF16), w2_ref[...].astype(BF16)) + gb_ref[...]
    g_all = _log_sigmoid(gate) * (1.0 / GATE_TAU)
    tri = tri_ref[...]
    g1, g2, g3 = _split3(g_all)
    b_all = _dot(tri, g1) + _dot(tri, g2) + _dot(tri, g3)
    sel = sel_ref[...]
    b_hi = b_all.astype(BF16)
    b_lo = (b_all - b_hi.astype(F32)).astype(BF16)
    beta_all = _dot(sel, b_hi) + _dot(sel, b_lo)

    row = lax.broadcasted_iota(jnp.int32, (chunk, chunk), 0)
    col = lax.broadcasted_iota(jnp.int32, (chunk, chunk), 1)
    diff_bits = row ^ col
    row_k = lax.broadcasted_iota(jnp.int32, (chunk, GLA_DK), 0)
    scale = GLA_DK ** -0.5

    for h in range(GLA_HEADS):
        ks = slice(h * GLA_DK, (h + 1) * GLA_DK)
        vs = slice(h * GLA_DV, (h + 1) * GLA_DV)
        b = b_all[:, ks]
        b_last = b[chunk - 1:chunk, :]
        q = q_ref[:, ks] * scale
        k = k_ref[:, ks]
        v = v_ref[:, vs].astype(BF16)
        s_prev = s_ref[h]

        inter = _dot((q * jnp.exp(b)).astype(BF16), s_prev.astype(BF16))

        acc = _dot_nt(q.astype(BF16), k.astype(BF16))
        for li, s in enumerate(GLA_LEVELS):
            beta = beta_all[li * chunk:(li + 1) * chunk, ks]
            d = b - beta
            arg = jnp.where((row_k & s) != 0, d, -d)
            e = jnp.exp(jnp.minimum(arg, 0.0))
            p = _dot_nt((q * e).astype(BF16), (k * e).astype(BF16))
            acc = jnp.where(diff_bits >= s, p, acc)
        a = jnp.where(row >= col, acc, 0.0)
        o = inter + _dot(a.astype(BF16), v)

        kdec_t = (k * jnp.exp(b_last - b)).T
        decay_t = jnp.broadcast_to(jnp.exp(b_last), (GLA_DK, GLA_DK)).T
        s_new = jnp.concatenate([decay_t, decay_t], axis=1) * s_prev + _dot(kdec_t.astype(BF16), v)
        s_ref[h] = s_new

        yb_ref[:, vs] = _gla_out(o, gn_ref[:, vs], og_ref[:, vs]).astype(BF16)

    @pl.when(c_idx == pl.num_programs(1) - 1)
    def _():
        sfin_ref[...] = s_ref[...]


def _gla_prompt(proj, glr, gate_w2p, gate_b, gla_norm_g, layer, batch, seq):
    chunk = GLA_CHUNK
    nc = seq // chunk
    tri, sel = _gla_constants(chunk)
    q_blk = (3 * CONV_DIM) // GLA_QK_DIM
    v_blk = (3 * CONV_DIM + 2 * GLA_QK_DIM) // GLA_V_DIM

    def col_spec(width, blk):
        return pl.BlockSpec((chunk, width), lambda b, c: (b * nc + c, blk))

    def whole(shape):
        return pl.BlockSpec((None,) + shape, lambda b, c: (layer,) + (0,) * len(shape))

    const_spec = lambda arr: pl.BlockSpec(arr.shape, lambda b, c: (0, 0))
    yb_spec = pl.BlockSpec((chunk, GLA_V_DIM), lambda b, c: (b * nc + c, 0))
    s_spec = pl.BlockSpec((None, GLA_HEADS, GLA_DK, GLA_DV), lambda b, c: (b, 0, 0, 0))
    return pl.pallas_call(
        _gla_prompt_kernel, grid=(batch, nc),
        in_specs=[col_spec(GLA_QK_DIM, q_blk), col_spec(GLA_QK_DIM, q_blk + 1),
                  col_spec(GLA_V_DIM, v_blk), col_spec(GLA_V_DIM, v_blk + 1),
                  col_spec(GATE_RANK, 0), whole((GATE_RANK, GLA_QK_DIM)), whole((1, GLA_QK_DIM)),
                  whole((1, GLA_V_DIM)), const_spec(tri), const_spec(sel)],
        out_specs=[yb_spec, s_spec],
        out_shape=[jax.ShapeDtypeStruct((batch * seq, GLA_V_DIM), BF16),
                   jax.ShapeDtypeStruct((batch, GLA_HEADS, GLA_DK, GLA_DV), F32)],
        scratch_shapes=[pltpu.VMEM((GLA_HEADS, GLA_DK, GLA_DV), F32)],
        compiler_params=_params(2), name="gla_prompt")(
            proj, proj, proj, proj, glr, gate_w2p, gate_b, gla_norm_g, tri, sel)


GLA_STEP_ROWS = 8


def _columns(slab):
    pad = jnp.zeros((GLA_DK - GLA_STEP_ROWS, GLA_DK), F32)
    return jnp.concatenate([slab, pad], axis=0).T


def _gla_step_kernel(q_ref, k_ref, v_ref, og_ref, glr_ref, w2_ref, gb_ref, gn_ref, s_ref,
                     yb_ref, snew_ref):
    gate = _dot(glr_ref[...].astype(BF16), w2_ref[...].astype(BF16)) + gb_ref[...]
    g = _log_sigmoid(gate) * (1.0 / GATE_TAU)
    decay = jnp.exp(g)
    q = q_ref[...] * (GLA_DK ** -0.5)
    k = k_ref[...]
    v = v_ref[...]
    qk = jnp.sum(q * k, axis=-1, keepdims=True)
    decay_c = _columns(decay)
    k_c = _columns(k)
    qd_c = _columns(q * decay)
    rows = []
    for i in range(GLA_STEP_ROWS):
        s_prev = s_ref[i]
        inter = jnp.sum(qd_c[:, i:i + 1] * s_prev, axis=0, keepdims=True)
        rows.append(inter + qk[i:i + 1, :] * v[i:i + 1, :])
        snew_ref[i] = decay_c[:, i:i + 1] * s_prev + k_c[:, i:i + 1] * v[i:i + 1, :]
    o = jnp.concatenate(rows, axis=0)
    yb_ref[...] = _gla_out(o, gn_ref[...], og_ref[...]).astype(BF16)


def _gla_step_into_kernel(q_ref, k_ref, v_ref, og_ref, glr_ref, w2_ref, gb_ref, gn_ref, s_ref,
                          stacked_ref, yb_ref, snew_ref):
    del stacked_ref
    _gla_step_kernel(q_ref, k_ref, v_ref, og_ref, glr_ref, w2_ref, gb_ref, gn_ref, s_ref,
                     yb_ref, snew_ref)


def _gla_step(proj, glr, state, gate_w2, gate_b, gla_norm_g, layer, stacked):
    nb = proj.shape[0]
    rows = GLA_STEP_ROWS
    q_blk = (3 * CONV_DIM) // GLA_DK
    k_blk = q_blk + GLA_HEADS
    v_blk = (3 * CONV_DIM + 2 * GLA_QK_DIM) // GLA_DV
    og_blk = v_blk + GLA_HEADS
    s_spec = pl.BlockSpec((None, rows, None, GLA_DK, GLA_DV), lambda t, h: (layer, t, h, 0, 0))
    in_specs = [pl.BlockSpec((rows, GLA_DK), lambda t, h: (t, q_blk + h)),
                pl.BlockSpec((rows, GLA_DK), lambda t, h: (t, k_blk + h)),
                pl.BlockSpec((rows, GLA_DV), lambda t, h: (t, v_blk + h)),
                pl.BlockSpec((rows, GLA_DV), lambda t, h: (t, og_blk + h)),
                pl.BlockSpec((rows, GATE_RANK), lambda t, h: (t, 0)),
                pl.BlockSpec((None, GATE_RANK, GLA_DK), lambda t, h: (layer, 0, h)),
                pl.BlockSpec((None, 1, GLA_DK), lambda t, h: (layer, 0, h)),
                pl.BlockSpec((None, 1, GLA_DV), lambda t, h: (layer, 0, h)),
                s_spec]
    args = [proj, proj, proj, proj, glr, gate_w2, gate_b, gla_norm_g, state]
    body, aliases = _gla_step_kernel, {}
    if stacked is not None:
        in_specs.append(pl.BlockSpec(memory_space=pl.ANY))
        args.append(stacked)
        body, aliases = _gla_step_into_kernel, {len(args) - 1: 1}
    return pl.pallas_call(
        body, grid=(nb // rows, GLA_HEADS), in_specs=in_specs,
        out_specs=[pl.BlockSpec((rows, GLA_DV), lambda t, h: (t, h)), s_spec],
        out_shape=[jax.ShapeDtypeStruct((nb, GLA_V_DIM), BF16),
                   jax.ShapeDtypeStruct(state.shape, F32)],
        input_output_aliases=aliases,
        compiler_params=_params(2), name="gla_step")(*args)


def _prepare(norm_mix_g, w_in, gate_w2, gate_b, gla_norm_g, norm_ffn_g, ffn_conv_b, final_norm_g):
    return dict(
        norm_mix_g=norm_mix_g[:, None, :], norm_ffn_g=norm_ffn_g[:, None, :],
        w_in_t=jnp.swapaxes(w_in, 1, 2), gate_w2=gate_w2, gate_b=gate_b[:, None, :],
        gla_norm_g=gla_norm_g[:, None, :], ffn_conv_b=ffn_conv_b[:, None, :],
        final_norm_g=final_norm_g[None, :])


def _prompt_trunk(x, p, conv_w, w_out, w_up, ffn_conv_w, w_down, batch, seq):
    convs, glas, ffns = [], [], []
    for l in range(DEPTH):
        proj, glr = _in_proj(x, p["norm_mix_g"], p["w_in_t"], l, 1024, 512)
        ya, c_new = _conv_mixer_prompt(proj, conv_w, l, batch, seq, 512, 512)
        yb, s_new = _gla_prompt(proj, glr, p["gate_w2"], p["gate_b"], p["gla_norm_g"], l, batch, seq)
        x = _out_proj(ya, yb, w_out, x, l, 1024, 512)
        x, f_new = _ffn_prompt(x, p["norm_ffn_g"], w_up, ffn_conv_w, p["ffn_conv_b"], w_down,
                               p["final_norm_g"], l, batch, seq, 1024, 512, final_norm=(l == DEPTH - 1))
        convs.append(c_new)
        glas.append(s_new)
        ffns.append(f_new)
    return x, jnp.stack(convs), jnp.stack(glas), jnp.stack(ffns)


def _sample_trunk(x, state_conv, state_gla, state_ffn, p, conv_w, w_out, w_up, ffn_conv_w, w_down):
    nb = x.shape[0]
    convs, ffns, gla_new = [], [], None
    for l in range(DEPTH):
        proj, glr = _in_proj(x, p["norm_mix_g"], p["w_in_t"], l, nb, 512)
        ya, c0, c1 = _conv_mixer_step(proj, state_conv, conv_w, l, 512)
        yb, gla_new = _gla_step(proj, glr, state_gla, p["gate_w2"], p["gate_b"], p["gla_norm_g"], l, gla_new)
        x = _out_proj(ya, yb, w_out, x, l, nb, 512)
        uv = _up_proj(x, p["norm_ffn_g"], w_up, l, nb, 512)
        h, f0, f1 = _conv_ffn_step(uv, state_ffn, ffn_conv_w, p["ffn_conv_b"], l, 512)
        x = _down_proj(h, w_down, x, l, nb, 512)
        convs.append((c0, c1))
        ffns.append((f0, f1))
    y = _final_norm(x, p["final_norm_g"], nb)
    return y, convs, gla_new, ffns


def kernel(x_prompt, x_sample, state_conv, state_gla, state_ffn_conv, norm_mix_g, w_in, conv_w, gate_w2, gate_b, gla_norm_g, w_out, norm_ffn_g, w_up, ffn_conv_w, ffn_conv_b, w_down, final_norm_g):
    batch, seq, _ = x_prompt.shape
    nb = x_sample.shape[0]
    p = _prepare(norm_mix_g, w_in, gate_w2, gate_b, gla_norm_g, norm_ffn_g, ffn_conv_b, final_norm_g)

    y_p, conv_p, gla_p, ffn_p = _prompt_trunk(
        x_prompt.reshape(batch * seq, D_MODEL), p, conv_w, w_out, w_up, ffn_conv_w, w_down, batch, seq)

    y_s, convs, gla_s, ffns = _sample_trunk(
        x_sample.reshape(nb, D_MODEL),
        state_conv.reshape(DEPTH, nb, (CONV_WIDTH - 1) * CONV_DIM), state_gla,
        state_ffn_conv.reshape(DEPTH, nb, (CONV_WIDTH - 1) * D_FF),
        p, conv_w, w_out, w_up, ffn_conv_w, w_down)

    def restack(pairs):
        return jnp.stack([jnp.stack(pair, axis=1) for pair in pairs])

    return (y_p.reshape(batch, seq, D_MODEL), y_s.reshape(nb, 1, D_MODEL), conv_p, gla_p, ffn_p,
            restack(convs), gla_s, restack(ffns))
```

```python
import functools

import numpy as np
import jax
import jax.numpy as jnp
from jax import lax
from jax.experimental import pallas as pl
from jax.experimental.pallas import tpu as pltpu

F32 = jnp.float32
BF16 = jnp.bfloat16

D_MODEL = 2048
DEPTH = 2
CONV_DIM = 1024
CONV_WIDTH = 3
GLA_HEADS = 4
GLA_QK_DIM = 512
GLA_V_DIM = 1024
GLA_DK = 128
GLA_DV = 256
GATE_RANK = 16
GATE_TAU = 16.0
D_FF = 5632
EPS = 1e-6
MAIN_COLS = 3 * CONV_DIM + 2 * GLA_QK_DIM + 2 * GLA_V_DIM

GLA_CHUNK = 128
GLA_LEVELS = tuple(1 << i for i in range(GLA_CHUNK.bit_length() - 1))

VMEM_LIMIT = 56 * 1024 * 1024


def _params(n_axes):
    return pltpu.CompilerParams(dimension_semantics=("arbitrary",) * n_axes,
                                vmem_limit_bytes=VMEM_LIMIT)


def _dot(a, b):
    return jnp.dot(a, b, preferred_element_type=F32)


def _dot_nt(a, b):
    return lax.dot_general(a, b, (((1,), (1,)), ((), ())), preferred_element_type=F32)


def _split3(x):
    hi = x.astype(BF16)
    r1 = x - hi.astype(F32)
    mid = r1.astype(BF16)
    lo = (r1 - mid.astype(F32)).astype(BF16)
    return hi, mid, lo


def _sigmoid(x):
    return 1.0 / (1.0 + jnp.exp(-x))


def _log_sigmoid(x):
    return jnp.minimum(x, 0.0) - jnp.log1p(jnp.exp(-jnp.abs(x)))


def _rms_proj_kernel(x_ref, g_ref, w_ref, o_ref, xn_ref):
    @pl.when(pl.program_id(1) == 0)
    def _():
        x = x_ref[...]
        var = jnp.mean(x * x, axis=-1, keepdims=True)
        xn_ref[...] = (x * lax.rsqrt(var + EPS) * g_ref[...]).astype(BF16)

    o_ref[...] = _dot(xn_ref[...], w_ref[...].astype(BF16))


def _in_proj_kernel(x_ref, g_ref, wt_ref, wgt_ref, o_ref, og_ref, xn_ref):
    @pl.when(pl.program_id(1) == 0)
    def _():
        x = x_ref[...]
        var = jnp.mean(x * x, axis=-1, keepdims=True)
        xn = (x * lax.rsqrt(var + EPS) * g_ref[...]).astype(BF16)
        xn_ref[...] = xn
        og_ref[...] = _dot_nt(xn, wgt_ref[...].astype(BF16))

    o_ref[...] = _dot_nt(xn_ref[...], wt_ref[...].astype(BF16))


def _up_proj(x, g, w, layer, tm, tn):
    m = x.shape[0]
    x_spec = pl.BlockSpec((tm, D_MODEL), lambda i, j: (i, 0))
    g_spec = pl.BlockSpec((None, 1, D_MODEL), lambda i, j: (layer, 0, 0))
    w_spec = pl.BlockSpec((None, D_MODEL, tn), lambda i, j: (layer, 0, j))
    o_spec = pl.BlockSpec((tm, tn), lambda i, j: (i, j))
    return pl.pallas_call(
        _rms_proj_kernel, grid=(m // tm, 2 * D_FF // tn), in_specs=[x_spec, g_spec, w_spec],
        out_specs=o_spec, out_shape=jax.ShapeDtypeStruct((m, 2 * D_FF), F32),
        scratch_shapes=[pltpu.VMEM((tm, D_MODEL), BF16)],
        compiler_params=_params(2), name="up_proj")(x, g, w)


def _in_proj(x, g, w_in_t, layer, tm, tn):
    m = x.shape[0]
    x_spec = pl.BlockSpec((tm, D_MODEL), lambda i, j: (i, 0), pipeline_mode=pl.Buffered(1))
    g_spec = pl.BlockSpec((None, 1, D_MODEL), lambda i, j: (layer, 0, 0))
    wt_spec = pl.BlockSpec((None, tn, D_MODEL), lambda i, j: (layer, j, 0))
    wgt_spec = pl.BlockSpec((None, GATE_RANK, D_MODEL), lambda i, j: (layer, MAIN_COLS // GATE_RANK, 0))
    o_spec = pl.BlockSpec((tm, tn), lambda i, j: (i, j))
    og_spec = pl.BlockSpec((tm, GATE_RANK), lambda i, j: (i, 0))
    return pl.pallas_call(
        _in_proj_kernel, grid=(m // tm, MAIN_COLS // tn), in_specs=[x_spec, g_spec, wt_spec, wgt_spec],
        out_specs=[o_spec, og_spec],
        out_shape=[jax.ShapeDtypeStruct((m, MAIN_COLS), F32), jax.ShapeDtypeStruct((m, GATE_RANK), F32)],
        scratch_shapes=[pltpu.VMEM((tm, D_MODEL), BF16)],
        compiler_params=_params(2), name="in_proj")(x, g, w_in_t, w_in_t)


def _out_proj_kernel(ya_ref, yb_ref, wa_ref, wb_ref, x_ref, o_ref):
    acc = _dot(ya_ref[...], wa_ref[...].astype(BF16))
    acc = acc + _dot(yb_ref[...], wb_ref[...].astype(BF16))
    o_ref[...] = x_ref[...] + acc


def _out_proj(ya, yb, w_out, x, layer, tm, tn):
    m = x.shape[0]
    grid = (D_MODEL // tn, m // tm)
    a_spec = pl.BlockSpec((tm, CONV_DIM), lambda j, i: (i, 0))
    w_mode = pl.Buffered(1) if tn == D_MODEL else None
    wa_spec = pl.BlockSpec((None, CONV_DIM, tn), lambda j, i: (layer, 0, j), pipeline_mode=w_mode)
    wb_spec = pl.BlockSpec((None, GLA_V_DIM, tn), lambda j, i: (layer, 1, j), pipeline_mode=w_mode)
    x_spec = pl.BlockSpec((tm, tn), lambda j, i: (i, j))
    return pl.pallas_call(
        _out_proj_kernel, grid=grid, in_specs=[a_spec, a_spec, wa_spec, wb_spec, x_spec],
        out_specs=x_spec, out_shape=jax.ShapeDtypeStruct((m, D_MODEL), F32),
        compiler_params=_params(2), name="out_proj")(ya, yb, w_out, w_out, x)


def _down_proj_kernel(h_ref, w_ref, x_ref, o_ref):
    o_ref[...] = x_ref[...] + _dot(h_ref[...], w_ref[...].astype(BF16))


def _down_proj(h, w_down, x, layer, tm, tn):
    m = x.shape[0]
    grid = (D_MODEL // tn, m // tm)
    h_spec = pl.BlockSpec((tm, D_FF), lambda j, i: (i, 0))
    w_spec = pl.BlockSpec((None, D_FF, tn), lambda j, i: (layer, 0, j))
    x_spec = pl.BlockSpec((tm, tn), lambda j, i: (i, j))
    return pl.pallas_call(
        _down_proj_kernel, grid=grid, in_specs=[h_spec, w_spec, x_spec], out_specs=x_spec,
        out_shape=jax.ShapeDtypeStruct((m, D_MODEL), F32),
        compiler_params=_params(2), name="down_proj")(h, w_down, x)


def _final_norm_kernel(x_ref, g_ref, o_ref):
    x = x_ref[...]
    var = jnp.mean(x * x, axis=-1, keepdims=True)
    o_ref[...] = x * lax.rsqrt(var + EPS) * g_ref[...]


def _final_norm(x, g, tm):
    m = x.shape[0]
    x_spec = pl.BlockSpec((tm, D_MODEL), lambda i: (i, 0))
    g_spec = pl.BlockSpec((1, D_MODEL), lambda i: (0, 0))
    return pl.pallas_call(
        _final_norm_kernel, grid=(m // tm,), in_specs=[x_spec, g_spec], out_specs=x_spec,
        out_shape=jax.ShapeDtypeStruct((m, D_MODEL), F32),
        compiler_params=_params(1), name="final_norm")(x, g)


def _shifted_taps(z, zbuf_ref, t):
    tl = z.shape[0]

    @pl.when(t == 0)
    def _():
        zbuf_ref[0:8, :] = jnp.zeros((8, z.shape[1]), F32)

    zbuf_ref[8:8 + tl, :] = z
    z2 = zbuf_ref[6:6 + tl, :]
    z1 = zbuf_ref[7:7 + tl, :]
    zbuf_ref[0:8, :] = z[tl - 8:tl, :]
    return z2, z1


def _conv_mixer_kernel(bg_ref, cg_ref, h_ref, w_ref, ya_ref, st_ref, zbuf_ref):
    t = pl.program_id(2)
    z = cg_ref[...] * h_ref[...]
    tl = z.shape[0]
    z2, z1 = _shifted_taps(z, zbuf_ref, t)
    cu = z2 * w_ref[0:1, :]
    cu = cu + z1 * w_ref[1:2, :]
    cu = cu + z * w_ref[2:3, :]
    ya_ref[...] = (bg_ref[...] * cu).astype(BF16)

    @pl.when(t == pl.num_programs(2) - 1)
    def _():
        st_ref[...] = z[tl - (CONV_WIDTH - 1):tl, :]


def _conv_mixer_prompt(proj, conv_w, layer, batch, seq, tl, tc):
    nt, ncol = seq // tl, CONV_DIM // tc
    grid = (batch, ncol, nt)

    def col_spec(group):
        return pl.BlockSpec((tl, tc), lambda b, c, t: (b * nt + t, group * ncol + c))

    w_spec = pl.BlockSpec((None, CONV_WIDTH, tc), lambda b, c, t: (layer, 0, c))
    ya_spec = pl.BlockSpec((tl, tc), lambda b, c, t: (b * nt + t, c))
    st_spec = pl.BlockSpec((None, CONV_WIDTH - 1, tc), lambda b, c, t: (b, 0, c))
    return pl.pallas_call(
        _conv_mixer_kernel, grid=grid,
        in_specs=[col_spec(0), col_spec(1), col_spec(2), w_spec],
        out_specs=[ya_spec, st_spec],
        out_shape=[jax.ShapeDtypeStruct((batch * seq, CONV_DIM), BF16),
                   jax.ShapeDtypeStruct((batch, CONV_WIDTH - 1, CONV_DIM), F32)],
        scratch_shapes=[pltpu.VMEM((tl + 8, tc), F32)],
        compiler_params=_params(3), name="conv_mixer")(proj, proj, proj, conv_w)


def _ffn_prompt_kernel(x_ref, g_ref, wu_ref, wv_ref, cw_ref, cb_ref, wd_ref, fg_ref,
                       o_ref, st_ref, xn_ref, zbuf_ref, carry_ref, *, tiles_per_seq, final_norm):
    i = pl.program_id(0)
    f = pl.program_id(1)
    tm = x_ref.shape[0]

    @pl.when(f == 0)
    def _():
        x = x_ref[...]
        var = jnp.mean(x * x, axis=-1, keepdims=True)
        xn_ref[...] = (x * lax.rsqrt(var + EPS) * g_ref[...]).astype(BF16)
        o_ref[...] = x

    first = lax.rem(i, tiles_per_seq) == 0

    @pl.when(first)
    def _():
        zbuf_ref[0:8, :] = jnp.zeros((8, zbuf_ref.shape[1]), F32)

    @pl.when(jnp.logical_not(first))
    def _():
        zbuf_ref[0:8, :] = carry_ref[f]

    xn = xn_ref[...]
    u = _dot(xn, wu_ref[...].astype(BF16))
    v = _dot(xn, wv_ref[...].astype(BF16))
    zbuf_ref[8:8 + tm, :] = u
    cu = zbuf_ref[6:6 + tm, :] * cw_ref[0:1, :]
    cu = cu + zbuf_ref[7:7 + tm, :] * cw_ref[1:2, :]
    cu = cu + u * cw_ref[2:3, :]
    carry_ref[f] = u[tm - 8:tm, :]
    st_ref[...] = u[tm - (CONV_WIDTH - 1):tm, :]

    pre = cu + cb_ref[...]
    h = (pre * _sigmoid(pre) * v).astype(BF16)
    o_ref[...] += _dot(h, wd_ref[...].astype(BF16))

    if final_norm:
        @pl.when(f == pl.num_programs(1) - 1)
        def _():
            y = o_ref[...]
            var = jnp.mean(y * y, axis=-1, keepdims=True)
            o_ref[...] = y * lax.rsqrt(var + EPS) * fg_ref[...]


def _ffn_prompt(x, g, w_up, conv_w, conv_b, w_down, final_g, layer, batch, seq, tm, tf, final_norm):
    m = x.shape[0]
    tiles_per_seq = seq // tm
    nf = D_FF // tf
    row_spec = pl.BlockSpec((tm, D_MODEL), lambda i, f: (i, 0), pipeline_mode=pl.Buffered(1))
    g_spec = pl.BlockSpec((None, 1, D_MODEL), lambda i, f: (layer, 0, 0))
    wu_spec = pl.BlockSpec((None, D_MODEL, tf), lambda i, f: (layer, 0, f))
    wv_spec = pl.BlockSpec((None, D_MODEL, tf), lambda i, f: (layer, 0, nf + f))
    cw_spec = pl.BlockSpec((None, CONV_WIDTH, tf), lambda i, f: (layer, 0, f))
    cb_spec = pl.BlockSpec((None, 1, tf), lambda i, f: (layer, 0, f))
    wd_spec = pl.BlockSpec((None, tf, D_MODEL), lambda i, f: (layer, f, 0))
    fg_spec = pl.BlockSpec((1, D_MODEL), lambda i, f: (0, 0))
    st_spec = pl.BlockSpec((None, CONV_WIDTH - 1, tf), lambda i, f: (i, 0, f))
    body = functools.partial(_ffn_prompt_kernel, tiles_per_seq=tiles_per_seq, final_norm=final_norm)
    out, tails = pl.pallas_call(
        body, grid=(m // tm, nf),
        in_specs=[row_spec, g_spec, wu_spec, wv_spec, cw_spec, cb_spec, wd_spec, fg_spec],
        out_specs=[row_spec, st_spec],
        out_shape=[jax.ShapeDtypeStruct((m, D_MODEL), F32),
                   jax.ShapeDtypeStruct((m // tm, CONV_WIDTH - 1, D_FF), F32)],
        scratch_shapes=[pltpu.VMEM((tm, D_MODEL), BF16), pltpu.VMEM((tm + 8, tf), F32),
                        pltpu.VMEM((nf, 8, tf), F32)],
        compiler_params=_params(2), name="ffn_prompt")(
            x, g, w_up, w_up, conv_w, conv_b, w_down, final_g)
    return out, tails[tiles_per_seq - 1::tiles_per_seq]


def _conv_tap_step(new_row, st_ref, w_ref, new_st_ref):
    p1 = st_ref[:, 1, :]
    cu = st_ref[:, 0, :] * w_ref[0:1, :]
    cu = cu + p1 * w_ref[1:2, :]
    cu = cu + new_row * w_ref[2:3, :]
    new_st_ref[:, 0, :] = p1
    new_st_ref[:, 1, :] = new_row
    return cu


def _conv_mixer_step_kernel(bg_ref, cg_ref, h_ref, st_ref, w_ref, *rest):
    ya_ref, new_st_ref = rest[-2:]
    cu = _conv_tap_step(cg_ref[...] * h_ref[...], st_ref, w_ref, new_st_ref)
    ya_ref[...] = (bg_ref[...] * cu).astype(BF16)


def _conv_ffn_step_kernel(u_ref, v_ref, st_ref, w_ref, b_ref, *rest):
    h_ref, new_st_ref = rest[-2:]
    pre = _conv_tap_step(u_ref[...], st_ref, w_ref, new_st_ref) + b_ref[...]
    h_ref[...] = (pre * _sigmoid(pre) * v_ref[...]).astype(BF16)


def _conv_step_call(body, name, cols, groups, width, state, small, layer, tc, stacked):
    nb = cols.shape[0]
    ncol = width // tc
    st_spec = pl.BlockSpec((None, nb, CONV_WIDTH - 1, tc), lambda c: (layer, 0, 0, c))
    in_specs = [pl.BlockSpec((nb, tc), functools.partial(lambda grp, c: (0, grp * ncol + c), grp))
                for grp in range(groups)]
    in_specs.append(st_spec)
    in_specs += [pl.BlockSpec((None, a.shape[1], tc), lambda c: (layer, 0, c)) for a in small]
    args = [cols] * groups + [state] + list(small)
    aliases = {}
    if stacked is not None:
        in_specs.append(pl.BlockSpec(memory_space=pl.ANY))
        args.append(stacked)
        aliases = {len(args) - 1: 1}
    return pl.pallas_call(
        body, grid=(ncol,), in_specs=in_specs,
        out_specs=[pl.BlockSpec((nb, tc), lambda c: (0, c)), st_spec],
        out_shape=[jax.ShapeDtypeStruct((nb, width), BF16), jax.ShapeDtypeStruct(state.shape, F32)],
        input_output_aliases=aliases, compiler_params=_params(1), name=name)(*args)


def _conv_mixer_step(proj, state, conv_w, layer, tc, stacked):
    return _conv_step_call(_conv_mixer_step_kernel, "conv_mixer_step", proj, 3, CONV_DIM, state,
                           [conv_w], layer, tc, stacked)


def _conv_ffn_step(uv, state, conv_w, conv_b, layer, tc, stacked):
    return _conv_step_call(_conv_ffn_step_kernel, "conv_ffn_step", uv, 2, D_FF, state,
                           [conv_w, conv_b], layer, tc, stacked)


GLA_FINE_LEVELS = tuple(s for s in GLA_LEVELS if s < 4)
GLA_COARSE_LEVELS = tuple(s for s in GLA_LEVELS if s >= 4)


def _ref_rows(s, chunk):
    rows = np.arange(chunk)
    return (rows // (2 * s)) * (2 * s) + s - 1


def _gla_constants(chunk):
    rows = np.arange(chunk)
    tri = (rows[:, None] >= rows[None, :]).astype(np.float32)
    sels = [(_ref_rows(s, chunk)[:, None] == rows[None, :]).astype(np.float32) for s in GLA_FINE_LEVELS]
    return jnp.asarray(tri, BF16), jnp.asarray(np.concatenate(sels, axis=0), BF16)


def _exact_dot01(m01, x):
    x1, x2, x3 = _split3(x)
    return _dot(m01, x1) + _dot(m01, x2) + _dot(m01, x3)


def _gla_decay_kernel(glr_ref, w2_ref, gb_ref, tri_ref, b_ref):
    chunk = tri_ref.shape[0]
    gate = _dot(glr_ref[...].astype(BF16), w2_ref[...].astype(BF16)) + gb_ref[...]
    g = _log_sigmoid(gate) * (1.0 / GATE_TAU)
    tri = tri_ref[...]
    for c in range(glr_ref.shape[0] // chunk):
        rows = slice(c * chunk, (c + 1) * chunk)
        b_ref[rows, :] = _exact_dot01(tri, g[rows, :])


def _gla_decay(glr, gate_w2, gate_b, tri, layer, tg):
    m = glr.shape[0]
    return pl.pallas_call(
        _gla_decay_kernel, grid=(m // tg,),
        in_specs=[pl.BlockSpec((tg, GATE_RANK), lambda i: (i, 0)),
                  pl.BlockSpec((None, GATE_RANK, GLA_QK_DIM), lambda i: (layer, 0, 0)),
                  pl.BlockSpec((None, 1, GLA_QK_DIM), lambda i: (layer, 0, 0)),
                  pl.BlockSpec(tri.shape, lambda i: (0, 0))],
        out_specs=pl.BlockSpec((tg, GLA_QK_DIM), lambda i: (i, 0)),
        out_shape=jax.ShapeDtypeStruct((m, GLA_QK_DIM), F32),
        compiler_params=_params(1), name="gla_decay")(glr, gate_w2, gate_b, tri)


def _gla_out(o, gn, og):
    var = jnp.mean(o * o, axis=-1, keepdims=True)
    return o * lax.rsqrt(var + EPS) * gn * (og * _sigmoid(og))


def _gla_prompt_kernel(q_ref, k_ref, v_ref, og_ref, b_ref, gn_ref, sel_ref, yb_ref, sfin_ref, s_ref):
    c_idx = pl.program_id(1)
    chunk = q_ref.shape[0]

    @pl.when(c_idx == 0)
    def _():
        s_ref[...] = jnp.zeros(s_ref.shape, F32)

    b_all = b_ref[...]
    beta_fine = _exact_dot01(sel_ref[...], b_all)

    row = lax.broadcasted_iota(jnp.int32, (chunk, chunk), 0)
    col = lax.broadcasted_iota(jnp.int32, (chunk, chunk), 1)
    diff_bits = row ^ col
    scale = GLA_DK ** -0.5

    for h in range(GLA_HEADS):
        ks = slice(h * GLA_DK, (h + 1) * GLA_DK)
        vs = slice(h * GLA_DV, (h + 1) * GLA_DV)
        b = b_all[:, ks]
        b_last = b_ref[chunk - 1:chunk, ks]
        q = q_ref[:, ks] * scale
        k = k_ref[:, ks]
        v = v_ref[:, vs].astype(BF16)
        s_prev = s_ref[h]

        inter = _dot((q * jnp.exp(b)).astype(BF16), s_prev.astype(BF16))

        acc = _dot_nt(q.astype(BF16), k.astype(BF16))
        for s in GLA_LEVELS:
            if s in GLA_FINE_LEVELS:
                li = GLA_FINE_LEVELS.index(s)
                beta = beta_fine[li * chunk:(li + 1) * chunk, ks]
            else:
                beta = jnp.concatenate(
                    [jnp.broadcast_to(b_ref[int(r):int(r) + 1, ks], (2 * s, GLA_DK))
                     for r in _ref_rows(s, chunk)[::2 * s]], axis=0)
            e = jnp.exp(-jnp.abs(b - beta))
            p = _dot_nt((q * e).astype(BF16), (k * e).astype(BF16))
            acc = jnp.where(diff_bits >= s, p, acc)
        a = jnp.where(row >= col, acc, 0.0)
        o = inter + _dot(a.astype(BF16), v)

        kdec_t = (k * jnp.exp(b_last - b)).T
        decay_t = jnp.broadcast_to(jnp.exp(b_last), (GLA_DK, GLA_DK)).T
        s_new = jnp.concatenate([decay_t, decay_t], axis=1) * s_prev + _dot(kdec_t.astype(BF16), v)
        s_ref[h] = s_new

        yb_ref[:, vs] = _gla_out(o, gn_ref[:, vs], og_ref[:, vs]).astype(BF16)

    @pl.when(c_idx == pl.num_programs(1) - 1)
    def _():
        sfin_ref[...] = s_ref[...]


def _gla_prompt(proj, glr, gate_w2, gate_b, gla_norm_g, layer, batch, seq):
    chunk = GLA_CHUNK
    nc = seq // chunk
    tri, sel = _gla_constants(chunk)
    b_all = _gla_decay(glr, gate_w2, gate_b, tri, layer, 4 * chunk)
    q_blk = (3 * CONV_DIM) // GLA_QK_DIM
    v_blk = (3 * CONV_DIM + 2 * GLA_QK_DIM) // GLA_V_DIM

    def col_spec(width, blk):
        return pl.BlockSpec((chunk, width), lambda b, c: (b * nc + c, blk))

    gn_spec = pl.BlockSpec((None, 1, GLA_V_DIM), lambda b, c: (layer, 0, 0))
    sel_spec = pl.BlockSpec(sel.shape, lambda b, c: (0, 0))
    yb_spec = pl.BlockSpec((chunk, GLA_V_DIM), lambda b, c: (b * nc + c, 0))
    s_spec = pl.BlockSpec((None, GLA_HEADS, GLA_DK, GLA_DV), lambda b, c: (b, 0, 0, 0))
    return pl.pallas_call(
        _gla_prompt_kernel, grid=(batch, nc),
        in_specs=[col_spec(GLA_QK_DIM, q_blk), col_spec(GLA_QK_DIM, q_blk + 1),
                  col_spec(GLA_V_DIM, v_blk), col_spec(GLA_V_DIM, v_blk + 1),
                  col_spec(GLA_QK_DIM, 0), gn_spec, sel_spec],
        out_specs=[yb_spec, s_spec],
        out_shape=[jax.ShapeDtypeStruct((batch * seq, GLA_V_DIM), BF16),
                   jax.ShapeDtypeStruct((batch, GLA_HEADS, GLA_DK, GLA_DV), F32)],
        scratch_shapes=[pltpu.VMEM((GLA_HEADS, GLA_DK, GLA_DV), F32)],
        compiler_params=_params(2), name="gla_prompt")(
            proj, proj, proj, proj, b_all, gla_norm_g, sel)


GLA_STEP_ROWS = 8


def _columns(slab):
    pad = jnp.zeros((GLA_DK - GLA_STEP_ROWS, GLA_DK), F32)
    return jnp.concatenate([slab, pad], axis=0).T


def _gla_step_kernel(q_ref, k_ref, v_ref, og_ref, glr_ref, w2_ref, gb_ref, gn_ref, s_ref,
                     yb_ref, snew_ref):
    gate = _dot(glr_ref[...].astype(BF16), w2_ref[...].astype(BF16)) + gb_ref[...]
    g = _log_sigmoid(gate) * (1.0 / GATE_TAU)
    decay = jnp.exp(g)
    q = q_ref[...] * (GLA_DK ** -0.5)
    k = k_ref[...]
    v = v_ref[...]
    qk = jnp.sum(q * k, axis=-1, keepdims=True)
    decay_c = _columns(decay)
    k_c = _columns(k)
    qd_c = _columns(q * decay)
    rows = []
    for i in range(GLA_STEP_ROWS):
        s_prev = s_ref[i]
        inter = jnp.sum(qd_c[:, i:i + 1] * s_prev, axis=0, keepdims=True)
        rows.append(inter + qk[i:i + 1, :] * v[i:i + 1, :])
        snew_ref[i] = decay_c[:, i:i + 1] * s_prev + k_c[:, i:i + 1] * v[i:i + 1, :]
    o = jnp.concatenate(rows, axis=0)
    yb_ref[...] = _gla_out(o, gn_ref[...], og_ref[...]).astype(BF16)


def _gla_step_into_kernel(q_ref, k_ref, v_ref, og_ref, glr_ref, w2_ref, gb_ref, gn_ref, s_ref,
                          stacked_ref, yb_ref, snew_ref):
    del stacked_ref
    _gla_step_kernel(q_ref, k_ref, v_ref, og_ref, glr_ref, w2_ref, gb_ref, gn_ref, s_ref,
                     yb_ref, snew_ref)


def _gla_step(proj, glr, state, gate_w2, gate_b, gla_norm_g, layer, stacked):
    nb = proj.shape[0]
    rows = GLA_STEP_ROWS
    q_blk = (3 * CONV_DIM) // GLA_DK
    k_blk = q_blk + GLA_HEADS
    v_blk = (3 * CONV_DIM + 2 * GLA_QK_DIM) // GLA_DV
    og_blk = v_blk + GLA_HEADS
    s_spec = pl.BlockSpec((None, rows, None, GLA_DK, GLA_DV), lambda t, h: (layer, t, h, 0, 0))
    in_specs = [pl.BlockSpec((rows, GLA_DK), lambda t, h: (t, q_blk + h)),
                pl.BlockSpec((rows, GLA_DK), lambda t, h: (t, k_blk + h)),
                pl.BlockSpec((rows, GLA_DV), lambda t, h: (t, v_blk + h)),
                pl.BlockSpec((rows, GLA_DV), lambda t, h: (t, og_blk + h)),
                pl.BlockSpec((rows, GATE_RANK), lambda t, h: (t, 0)),
                pl.BlockSpec((None, GATE_RANK, GLA_DK), lambda t, h: (layer, 0, h)),
                pl.BlockSpec((None, 1, GLA_DK), lambda t, h: (layer, 0, h)),
                pl.BlockSpec((None, 1, GLA_DV), lambda t, h: (layer, 0, h)),
                s_spec]
    args = [proj, proj, proj, proj, glr, gate_w2, gate_b, gla_norm_g, state]
    body, aliases = _gla_step_kernel, {}
    if stacked is not None:
        in_specs.append(pl.BlockSpec(memory_space=pl.ANY))
        args.append(stacked)
        body, aliases = _gla_step_into_kernel, {len(args) - 1: 1}
    return pl.pallas_call(
        body, grid=(nb // rows, GLA_HEADS), in_specs=in_specs,
        out_specs=[pl.BlockSpec((rows, GLA_DV), lambda t, h: (t, h)), s_spec],
        out_shape=[jax.ShapeDtypeStruct((nb, GLA_V_DIM), BF16),
                   jax.ShapeDtypeStruct(state.shape, F32)],
        input_output_aliases=aliases,
        compiler_params=_params(2), name="gla_step")(*args)


def _prepare(norm_mix_g, w_in, gate_w2, gate_b, gla_norm_g, norm_ffn_g, ffn_conv_b, final_norm_g):
    return dict(
        norm_mix_g=norm_mix_g[:, None, :], norm_ffn_g=norm_ffn_g[:, None, :],
        w_in_t=jnp.swapaxes(w_in, 1, 2), gate_w2=gate_w2, gate_b=gate_b[:, None, :],
        gla_norm_g=gla_norm_g[:, None, :], ffn_conv_b=ffn_conv_b[:, None, :],
        final_norm_g=final_norm_g[None, :])


def _prompt_trunk(x, p, conv_w, w_out, w_up, ffn_conv_w, w_down, batch, seq):
    convs, glas, ffns = [], [], []
    for l in range(DEPTH):
        proj, glr = _in_proj(x, p["norm_mix_g"], p["w_in_t"], l, 2048, 512)
        ya, c_new = _conv_mixer_prompt(proj, conv_w, l, batch, seq, 512, 512)
        yb, s_new = _gla_prompt(proj, glr, p["gate_w2"], p["gate_b"], p["gla_norm_g"], l, batch, seq)
        x = _out_proj(ya, yb, w_out, x, l, 512, D_MODEL)
        x, f_new = _ffn_prompt(x, p["norm_ffn_g"], w_up, ffn_conv_w, p["ffn_conv_b"], w_down,
                               p["final_norm_g"], l, batch, seq, 1024, 512, final_norm=(l == DEPTH - 1))
        convs.append(c_new)
        glas.append(s_new)
        ffns.append(f_new)
    return x, jnp.stack(convs), jnp.stack(glas), jnp.stack(ffns)


def _sample_trunk(x, state_conv, state_gla, state_ffn, p, conv_w, w_out, w_up, ffn_conv_w, w_down):
    nb = x.shape[0]
    conv_new, gla_new, ffn_new = None, None, None
    for l in range(DEPTH):
        proj, glr = _in_proj(x, p["norm_mix_g"], p["w_in_t"], l, nb, 512)
        ya, conv_new = _conv_mixer_step(proj, state_conv, conv_w, l, 512, conv_new)
        yb, gla_new = _gla_step(proj, glr, state_gla, p["gate_w2"], p["gate_b"], p["gla_norm_g"], l, gla_new)
        x = _out_proj(ya, yb, w_out, x, l, nb, D_MODEL)
        uv = _up_proj(x, p["norm_ffn_g"], w_up, l, nb, 512)
        h, ffn_new = _conv_ffn_step(uv, state_ffn, ffn_conv_w, p["ffn_conv_b"], l, 512, ffn_new)
        x = _down_proj(h, w_down, x, l, nb, 512)
    y = _final_norm(x, p["final_norm_g"], nb)
    return y, conv_new, gla_new, ffn_new


def kernel(x_prompt, x_sample, state_conv, state_gla, state_ffn_conv, norm_mix_g, w_in, conv_w, gate_w2, gate_b, gla_norm_g, w_out, norm_ffn_g, w_up, ffn_conv_w, ffn_conv_b, w_down, final_norm_g):
    batch, seq, _ = x_prompt.shape
    nb = x_sample.shape[0]
    p = _prepare(norm_mix_g, w_in, gate_w2, gate_b, gla_norm_g, norm_ffn_g, ffn_conv_b, final_norm_g)

    y_p, conv_p, gla_p, ffn_p = _prompt_trunk(
        x_prompt.reshape(batch * seq, D_MODEL), p, conv_w, w_out, w_up, ffn_conv_w, w_down, batch, seq)

    y_s, conv_s, gla_s, ffn_s = _sample_trunk(
        x_sample.reshape(nb, D_MODEL), state_conv, state_gla, state_ffn_conv,
        p, conv_w, w_out, w_up, ffn_conv_w, w_down)

    return (y_p.reshape(batch, seq, D_MODEL), y_s.reshape(nb, 1, D_MODEL), conv_p, gla_p, ffn_p,
            conv_s, gla_s, ffn_s)
```

```python
import functools

import numpy as np
import jax
import jax.numpy as jnp
from jax import lax
from jax.experimental import pallas as pl
from jax.experimental.pallas import tpu as pltpu

F32 = jnp.float32
BF16 = jnp.bfloat16

D_MODEL = 2048
DEPTH = 2
CONV_DIM = 1024
CONV_WIDTH = 3
GLA_HEADS = 4
GLA_QK_DIM = 512
GLA_V_DIM = 1024
GLA_DK = 128
GLA_DV = 256
GATE_RANK = 16
GATE_TAU = 16.0
D_FF = 5632
EPS = 1e-6
MAIN_COLS = 3 * CONV_DIM + 2 * GLA_QK_DIM + 2 * GLA_V_DIM

GLA_CHUNK = 128
GLA_LEVELS = tuple(1 << i for i in range(GLA_CHUNK.bit_length() - 1))
GLA_FINE_LEVELS = tuple(s for s in GLA_LEVELS if s < 4)
GLA_STEP_ROWS = 8

VMEM_LIMIT = 56 * 1024 * 1024


def _params(n_axes):
    return pltpu.CompilerParams(dimension_semantics=("arbitrary",) * n_axes,
                                vmem_limit_bytes=VMEM_LIMIT)


def _dot(a, b):
    return jnp.dot(a, b, preferred_element_type=F32)


def _dot_nt(a, b):
    return lax.dot_general(a, b, (((1,), (1,)), ((), ())), preferred_element_type=F32)


def _split3(x):
    hi = x.astype(BF16)
    r1 = x - hi.astype(F32)
    mid = r1.astype(BF16)
    lo = (r1 - mid.astype(F32)).astype(BF16)
    return hi, mid, lo


def _exact_dot01(m01, x):
    x1, x2, x3 = _split3(x)
    return _dot(m01, x1) + _dot(m01, x2) + _dot(m01, x3)


def _sigmoid(x):
    return 1.0 / (1.0 + jnp.exp(-x))


def _log_sigmoid(x):
    return jnp.minimum(x, 0.0) - jnp.log1p(jnp.exp(-jnp.abs(x)))


def _rms(x, g):
    var = jnp.mean(x * x, axis=-1, keepdims=True)
    return x * lax.rsqrt(var + EPS) * g


def _stacked_operand(in_specs, args, stacked, out_index):
    if stacked is None:
        return {}
    in_specs.append(pl.BlockSpec(memory_space=pl.ANY))
    args.append(stacked)
    return {len(args) - 1: out_index}


def _in_proj_kernel(x_ref, xs_ref, g_ref, wt_ref, wgt_ref, o_ref, os_ref, og_ref, ogs_ref, xn_ref):
    tm = x_ref.shape[0]
    ns = xs_ref.shape[0]

    @pl.when(pl.program_id(1) == 0)
    def _():
        xn_ref[0:tm, :] = _rms(x_ref[...], g_ref[...]).astype(BF16)
        xn_ref[tm:tm + ns, :] = _rms(xs_ref[...], g_ref[...]).astype(BF16)
        gate = _dot_nt(xn_ref[...], wgt_ref[...].astype(BF16))
        og_ref[...] = gate[0:tm, :]
        ogs_ref[...] = gate[tm:tm + ns, :]

    res = _dot_nt(xn_ref[...], wt_ref[...].astype(BF16))
    o_ref[...] = res[0:tm, :]
    os_ref[...] = res[tm:tm + ns, :]


def _in_proj(x, xs, g, w_in_t, layer, tm, tn):
    m = x.shape[0]
    ns = xs.shape[0] // (m // tm)
    x_spec = pl.BlockSpec((tm, D_MODEL), lambda i, j: (i, 0), pipeline_mode=pl.Buffered(1))
    xs_spec = pl.BlockSpec((ns, D_MODEL), lambda i, j: (i, 0))
    g_spec = pl.BlockSpec((None, 1, D_MODEL), lambda i, j: (layer, 0, 0))
    wt_spec = pl.BlockSpec((None, tn, D_MODEL), lambda i, j: (layer, j, 0))
    wgt_spec = pl.BlockSpec((None, GATE_RANK, D_MODEL), lambda i, j: (layer, MAIN_COLS // GATE_RANK, 0))
    return pl.pallas_call(
        _in_proj_kernel, grid=(m // tm, MAIN_COLS // tn),
        in_specs=[x_spec, xs_spec, g_spec, wt_spec, wgt_spec],
        out_specs=[pl.BlockSpec((tm, tn), lambda i, j: (i, j)), pl.BlockSpec((ns, tn), lambda i, j: (i, j)),
                   pl.BlockSpec((tm, GATE_RANK), lambda i, j: (i, 0)),
                   pl.BlockSpec((ns, GATE_RANK), lambda i, j: (i, 0))],
        out_shape=[jax.ShapeDtypeStruct((m, MAIN_COLS), F32),
                   jax.ShapeDtypeStruct((xs.shape[0], MAIN_COLS), F32),
                   jax.ShapeDtypeStruct((m, GATE_RANK), F32),
                   jax.ShapeDtypeStruct((xs.shape[0], GATE_RANK), F32)],
        scratch_shapes=[pltpu.VMEM((tm + ns, D_MODEL), BF16)],
        compiler_params=_params(2), name="in_proj")(x, xs, g, w_in_t, w_in_t)


def _out_proj_kernel(ya_ref, yb_ref, wa_ref, wb_ref, x_ref, o_ref):
    acc = _dot(ya_ref[...], wa_ref[...].astype(BF16))
    acc = acc + _dot(yb_ref[...], wb_ref[...].astype(BF16))
    o_ref[...] = x_ref[...] + acc


def _out_proj(ya, yb, w_out, x, layer, tm):
    m = x.shape[0]
    a_spec = pl.BlockSpec((tm, CONV_DIM), lambda i: (i, 0))
    wa_spec = pl.BlockSpec((None, CONV_DIM, D_MODEL), lambda i: (layer, 0, 0), pipeline_mode=pl.Buffered(1))
    wb_spec = pl.BlockSpec((None, GLA_V_DIM, D_MODEL), lambda i: (layer, 1, 0), pipeline_mode=pl.Buffered(1))
    x_spec = pl.BlockSpec((tm, D_MODEL), lambda i: (i, 0))
    return pl.pallas_call(
        _out_proj_kernel, grid=(m // tm,), in_specs=[a_spec, a_spec, wa_spec, wb_spec, x_spec],
        out_specs=x_spec, out_shape=jax.ShapeDtypeStruct((m, D_MODEL), F32),
        compiler_params=_params(1), name="out_proj")(ya, yb, w_out, w_out, x)


def _conv_mixer_kernel(bg_ref, cg_ref, h_ref, w_ref, ya_ref, st_ref, zbuf_ref):
    t = pl.program_id(2)
    z = cg_ref[...] * h_ref[...]
    tl = z.shape[0]

    @pl.when(t == 0)
    def _():
        zbuf_ref[0:8, :] = jnp.zeros((8, z.shape[1]), F32)

    zbuf_ref[8:8 + tl, :] = z
    cu = zbuf_ref[6:6 + tl, :] * w_ref[0:1, :]
    cu = cu + zbuf_ref[7:7 + tl, :] * w_ref[1:2, :]
    cu = cu + z * w_ref[2:3, :]
    zbuf_ref[0:8, :] = z[tl - 8:tl, :]
    ya_ref[...] = (bg_ref[...] * cu).astype(BF16)

    @pl.when(t == pl.num_programs(2) - 1)
    def _():
        st_ref[...] = z[tl - (CONV_WIDTH - 1):tl, :]


def _conv_mixer_prompt(proj, conv_w, layer, batch, seq, tl, tc):
    nt, ncol = seq // tl, CONV_DIM // tc

    def col_spec(group):
        return pl.BlockSpec((tl, tc), lambda b, c, t: (b * nt + t, group * ncol + c))

    w_spec = pl.BlockSpec((None, CONV_WIDTH, tc), lambda b, c, t: (layer, 0, c))
    ya_spec = pl.BlockSpec((tl, tc), lambda b, c, t: (b * nt + t, c))
    st_spec = pl.BlockSpec((None, CONV_WIDTH - 1, tc), lambda b, c, t: (b, 0, c))
    return pl.pallas_call(
        _conv_mixer_kernel, grid=(batch, ncol, nt),
        in_specs=[col_spec(0), col_spec(1), col_spec(2), w_spec],
        out_specs=[ya_spec, st_spec],
        out_shape=[jax.ShapeDtypeStruct((batch * seq, CONV_DIM), BF16),
                   jax.ShapeDtypeStruct((batch, CONV_WIDTH - 1, CONV_DIM), F32)],
        scratch_shapes=[pltpu.VMEM((tl + 8, tc), F32)],
        compiler_params=_params(3), name="conv_mixer")(proj, proj, proj, conv_w)


def _conv_tap_step(new_row, st_ref, w_ref, new_st_ref):
    p1 = st_ref[:, 1, :]
    cu = st_ref[:, 0, :] * w_ref[0:1, :]
    cu = cu + p1 * w_ref[1:2, :]
    cu = cu + new_row * w_ref[2:3, :]
    new_st_ref[:, 0, :] = p1
    new_st_ref[:, 1, :] = new_row
    return cu


def _conv_mixer_step_kernel(bg_ref, cg_ref, h_ref, st_ref, w_ref, *rest):
    ya_ref, new_st_ref = rest[-2:]
    cu = _conv_tap_step(cg_ref[...] * h_ref[...], st_ref, w_ref, new_st_ref)
    ya_ref[...] = (bg_ref[...] * cu).astype(BF16)


def _conv_mixer_step(proj, state, conv_w, layer, tc, stacked):
    nb = proj.shape[0]
    ncol = CONV_DIM // tc
    st_spec = pl.BlockSpec((None, nb, CONV_WIDTH - 1, tc), lambda c: (layer, 0, 0, c))
    in_specs = [pl.BlockSpec((nb, tc), functools.partial(lambda grp, c: (0, grp * ncol + c), grp))
                for grp in range(3)]
    in_specs += [st_spec, pl.BlockSpec((None, CONV_WIDTH, tc), lambda c: (layer, 0, c))]
    args = [proj, proj, proj, state, conv_w]
    aliases = _stacked_operand(in_specs, args, stacked, 1)
    return pl.pallas_call(
        _conv_mixer_step_kernel, grid=(ncol,), in_specs=in_specs,
        out_specs=[pl.BlockSpec((nb, tc), lambda c: (0, c)), st_spec],
        out_shape=[jax.ShapeDtypeStruct((nb, CONV_DIM), BF16), jax.ShapeDtypeStruct(state.shape, F32)],
        input_output_aliases=aliases, compiler_params=_params(1), name="conv_mixer_step")(*args)


def _ffn_kernel(x_ref, xs_ref, sst_ref, g_ref, wu_ref, wv_ref, cw_ref, cb_ref, wd_ref, fg_ref, *rest,
                tiles_per_seq, final_norm):
    o_ref, os_ref, st_ref, sst_new_ref, xn_ref, zbuf_ref, carry_ref = rest[-7:]
    i = pl.program_id(0)
    f = pl.program_id(1)
    tm = x_ref.shape[0]
    ns = xs_ref.shape[0]

    @pl.when(f == 0)
    def _():
        x = x_ref[...]
        xs = xs_ref[...]
        xn_ref[0:tm, :] = _rms(x, g_ref[...]).astype(BF16)
        xn_ref[tm:tm + ns, :] = _rms(xs, g_ref[...]).astype(BF16)
        o_ref[...] = x
        os_ref[...] = xs

    first = lax.rem(i, tiles_per_seq) == 0

    @pl.when(first)
    def _():
        zbuf_ref[0:8, :] = jnp.zeros((8, zbuf_ref.shape[1]), F32)

    @pl.when(jnp.logical_not(first))
    def _():
        zbuf_ref[0:8, :] = carry_ref[f]

    xn = xn_ref[...]
    u_all = _dot(xn, wu_ref[...].astype(BF16))
    v_all = _dot(xn, wv_ref[...].astype(BF16))
    u = u_all[0:tm, :]
    zbuf_ref[8:8 + tm, :] = u
    cu = zbuf_ref[6:6 + tm, :] * cw_ref[0:1, :]
    cu = cu + zbuf_ref[7:7 + tm, :] * cw_ref[1:2, :]
    cu = cu + u * cw_ref[2:3, :]
    carry_ref[f] = u[tm - 8:tm, :]
    st_ref[...] = u[tm - (CONV_WIDTH - 1):tm, :]
    cu_s = _conv_tap_step(u_all[tm:tm + ns, :], sst_ref, cw_ref, sst_new_ref)

    pre = jnp.concatenate([cu, cu_s], axis=0) + cb_ref[...]
    h = (pre * _sigmoid(pre) * v_all).astype(BF16)
    acc = _dot(h, wd_ref[...].astype(BF16))
    o_ref[...] += acc[0:tm, :]
    os_ref[...] += acc[tm:tm + ns, :]

    if final_norm:
        @pl.when(f == pl.num_programs(1) - 1)
        def _():
            o_ref[...] = _rms(o_ref[...], fg_ref[...])
            os_ref[...] = _rms(os_ref[...], fg_ref[...])


def _ffn(x, xs, state_s, g, w_up, conv_w, conv_b, w_down, final_g, layer, seq, tm, tf, final_norm, stacked):
    m = x.shape[0]
    n_tiles = m // tm
    ns = xs.shape[0] // n_tiles
    tiles_per_seq = seq // tm
    nf = D_FF // tf
    row_spec = pl.BlockSpec((tm, D_MODEL), lambda i, f: (i, 0), pipeline_mode=pl.Buffered(1))
    srow_spec = pl.BlockSpec((ns, D_MODEL), lambda i, f: (i, 0))
    sst_spec = pl.BlockSpec((None, ns, CONV_WIDTH - 1, tf), lambda i, f: (layer, i, 0, f))
    g_spec = pl.BlockSpec((None, 1, D_MODEL), lambda i, f: (layer, 0, 0))
    wu_spec = pl.BlockSpec((None, D_MODEL, tf), lambda i, f: (layer, 0, f))
    wv_spec = pl.BlockSpec((None, D_MODEL, tf), lambda i, f: (layer, 0, nf + f))
    cw_spec = pl.BlockSpec((None, CONV_WIDTH, tf), lambda i, f: (layer, 0, f))
    cb_spec = pl.BlockSpec((None, 1, tf), lambda i, f: (layer, 0, f))
    wd_spec = pl.BlockSpec((None, tf, D_MODEL), lambda i, f: (layer, f, 0))
    fg_spec = pl.BlockSpec((1, D_MODEL), lambda i, f: (0, 0))
    st_spec = pl.BlockSpec((None, CONV_WIDTH - 1, tf), lambda i, f: (i, 0, f))
    in_specs = [row_spec, srow_spec, sst_spec, g_spec, wu_spec, wv_spec, cw_spec, cb_spec, wd_spec, fg_spec]
    args = [x, xs, state_s, g, w_up, w_up, conv_w, conv_b, w_down, final_g]
    aliases = _stacked_operand(in_specs, args, stacked, 3)
    body = functools.partial(_ffn_kernel, tiles_per_seq=tiles_per_seq, final_norm=final_norm)
    out, out_s, tails, state_new = pl.pallas_call(
        body, grid=(n_tiles, nf), in_specs=in_specs,
        out_specs=[row_spec, srow_spec, st_spec, sst_spec],
        out_shape=[jax.ShapeDtypeStruct((m, D_MODEL), F32),
                   jax.ShapeDtypeStruct(xs.shape, F32),
                   jax.ShapeDtypeStruct((n_tiles, CONV_WIDTH - 1, D_FF), F32),
                   jax.ShapeDtypeStruct(state_s.shape, F32)],
        scratch_shapes=[pltpu.VMEM((tm + ns, D_MODEL), BF16), pltpu.VMEM((tm + 8, tf), F32),
                        pltpu.VMEM((nf, 8, tf), F32)],
        input_output_aliases=aliases, compiler_params=_params(2), name="ffn")(*args)
    return out, out_s, tails[tiles_per_seq - 1::tiles_per_seq], state_new


def _ref_rows(s, chunk):
    rows = np.arange(chunk)
    return (rows // (2 * s)) * (2 * s) + s - 1


def _gla_constants(chunk):
    rows = np.arange(chunk)
    tri = (rows[:, None] >= rows[None, :]).astype(np.float32)
    sels = [(_ref_rows(s, chunk)[:, None] == rows[None, :]).astype(np.float32) for s in GLA_FINE_LEVELS]
    return jnp.asarray(tri, BF16), jnp.asarray(np.concatenate(sels, axis=0), BF16)


def _gla_log_decay(glr, w2, gb):
    gate = _dot(glr.astype(BF16), w2.astype(BF16)) + gb
    return _log_sigmoid(gate) * (1.0 / GATE_TAU)


def _gla_decay_kernel(glr_ref, w2_ref, gb_ref, tri_ref, b_ref):
    chunk = tri_ref.shape[0]
    g = _gla_log_decay(glr_ref[...], w2_ref[...], gb_ref[...])
    tri = tri_ref[...]
    for c in range(glr_ref.shape[0] // chunk):
        rows = slice(c * chunk, (c + 1) * chunk)
        b_ref[rows, :] = _exact_dot01(tri, g[rows, :])


def _gla_decay(glr, gate_w2, gate_b, tri, layer, tg):
    m = glr.shape[0]
    return pl.pallas_call(
        _gla_decay_kernel, grid=(m // tg,),
        in_specs=[pl.BlockSpec((tg, GATE_RANK), lambda i: (i, 0)),
                  pl.BlockSpec((None, GATE_RANK, GLA_QK_DIM), lambda i: (layer, 0, 0)),
                  pl.BlockSpec((None, 1, GLA_QK_DIM), lambda i: (layer, 0, 0)),
                  pl.BlockSpec(tri.shape, lambda i: (0, 0))],
        out_specs=pl.BlockSpec((tg, GLA_QK_DIM), lambda i: (i, 0)),
        out_shape=jax.ShapeDtypeStruct((m, GLA_QK_DIM), F32),
        compiler_params=_params(1), name="gla_decay")(glr, gate_w2, gate_b, tri)


def _gla_out(o, gn, og):
    var = jnp.mean(o * o, axis=-1, keepdims=True)
    return o * lax.rsqrt(var + EPS) * gn * (og * _sigmoid(og))


def _gla_prompt_kernel(q_ref, k_ref, v_ref, og_ref, b_ref, gn_ref, sel_ref, yb_ref, sfin_ref, s_ref):
    c_idx = pl.program_id(1)
    chunk = q_ref.shape[0]

    @pl.when(c_idx == 0)
    def _():
        s_ref[...] = jnp.zeros(s_ref.shape, F32)

    b_all = b_ref[...]
    beta_fine = _exact_dot01(sel_ref[...], b_all)

    row = lax.broadcasted_iota(jnp.int32, (chunk, chunk), 0)
    col = lax.broadcasted_iota(jnp.int32, (chunk, chunk), 1)
    diff_bits = row ^ col
    scale = GLA_DK ** -0.5

    for h in range(GLA_HEADS):
        ks = slice(h * GLA_DK, (h + 1) * GLA_DK)
        vs = slice(h * GLA_DV, (h + 1) * GLA_DV)
        b = b_all[:, ks]
        b_last = b_ref[chunk - 1:chunk, ks]
        q = q_ref[:, ks] * scale
        k = k_ref[:, ks]
        v = v_ref[:, vs].astype(BF16)
        s_prev = s_ref[h]

        inter = _dot((q * jnp.exp(b)).astype(BF16), s_prev.astype(BF16))

        acc = _dot_nt(q.astype(BF16), k.astype(BF16))
        for s in GLA_LEVELS:
            if s in GLA_FINE_LEVELS:
                li = GLA_FINE_LEVELS.index(s)
                beta = beta_fine[li * chunk:(li + 1) * chunk, ks]
            else:
                beta = jnp.concatenate(
                    [jnp.broadcast_to(b_ref[int(r):int(r) + 1, ks], (2 * s, GLA_DK))
                     for r in _ref_rows(s, chunk)[::2 * s]], axis=0)
            e = jnp.exp(-jnp.abs(b - beta))
            p = _dot_nt((q * e).astype(BF16), (k * e).astype(BF16))
            acc = jnp.where(diff_bits >= s, p, acc)
        a = jnp.where(row >= col, acc, 0.0)
        o = inter + _dot(a.astype(BF16), v)

        kdec_t = (k * jnp.exp(b_last - b)).T
        decay_t = jnp.broadcast_to(jnp.exp(b_last), (GLA_DK, GLA_DK)).T
        s_new = jnp.concatenate([decay_t, decay_t], axis=1) * s_prev + _dot(kdec_t.astype(BF16), v)
        s_ref[h] = s_new

        yb_ref[:, vs] = _gla_out(o, gn_ref[:, vs], og_ref[:, vs]).astype(BF16)

    @pl.when(c_idx == pl.num_programs(1) - 1)
    def _():
        sfin_ref[...] = s_ref[...]


def _gla_prompt(proj, glr, gate_w2, gate_b, gla_norm_g, layer, batch, seq):
    chunk = GLA_CHUNK
    nc = seq // chunk
    tri, sel = _gla_constants(chunk)
    b_all = _gla_decay(glr, gate_w2, gate_b, tri, layer, 4 * chunk)
    q_blk = (3 * CONV_DIM) // GLA_QK_DIM
    v_blk = (3 * CONV_DIM + 2 * GLA_QK_DIM) // GLA_V_DIM

    def col_spec(width, blk):
        return pl.BlockSpec((chunk, width), lambda b, c: (b * nc + c, blk))

    gn_spec = pl.BlockSpec((None, 1, GLA_V_DIM), lambda b, c: (layer, 0, 0))
    sel_spec = pl.BlockSpec(sel.shape, lambda b, c: (0, 0))
    yb_spec = pl.BlockSpec((chunk, GLA_V_DIM), lambda b, c: (b * nc + c, 0))
    s_spec = pl.BlockSpec((None, GLA_HEADS, GLA_DK, GLA_DV), lambda b, c: (b, 0, 0, 0))
    return pl.pallas_call(
        _gla_prompt_kernel, grid=(batch, nc),
        in_specs=[col_spec(GLA_QK_DIM, q_blk), col_spec(GLA_QK_DIM, q_blk + 1),
                  col_spec(GLA_V_DIM, v_blk), col_spec(GLA_V_DIM, v_blk + 1),
                  col_spec(GLA_QK_DIM, 0), gn_spec, sel_spec],
        out_specs=[yb_spec, s_spec],
        out_shape=[jax.ShapeDtypeStruct((batch * seq, GLA_V_DIM), BF16),
                   jax.ShapeDtypeStruct((batch, GLA_HEADS, GLA_DK, GLA_DV), F32)],
        scratch_shapes=[pltpu.VMEM((GLA_HEADS, GLA_DK, GLA_DV), F32)],
        compiler_params=_params(2), name="gla_prompt")(
            proj, proj, proj, proj, b_all, gla_norm_g, sel)


def _columns(slab):
    pad = jnp.zeros((GLA_DK - GLA_STEP_ROWS, GLA_DK), F32)
    return jnp.concatenate([slab, pad], axis=0).T


def _gla_step_kernel(q_ref, k_ref, v_ref, og_ref, glr_ref, w2_ref, gb_ref, gn_ref, s_ref, *rest):
    yb_ref, snew_ref = rest[-2:]
    rows = GLA_STEP_ROWS
    decay = jnp.exp(_gla_log_decay(glr_ref[...], w2_ref[...], gb_ref[...]))
    q = q_ref[...] * (GLA_DK ** -0.5)
    k = k_ref[...]
    v = v_ref[...]
    qk = jnp.sum(q * k, axis=-1, keepdims=True)

    v_tiled = jnp.concatenate([v] * rows, axis=1)
    row_id = lax.broadcasted_iota(jnp.int32, v_tiled.shape, 0)
    lane_blk = lax.broadcasted_iota(jnp.int32, v_tiled.shape, 1) // GLA_DV
    v_diag = jnp.where(row_id == lane_blk, v_tiled, 0.0)
    v_diag = jnp.concatenate([v_diag, jnp.zeros_like(v_diag)], axis=0)
    k_t = _columns(k)[:, 0:2 * rows]
    kv = _dot(k_t.astype(BF16), v_diag.astype(BF16))

    decay_c = _columns(decay)
    qd = (q * decay).astype(BF16)
    inter_rows = []
    for i in range(rows):
        s_prev = s_ref[i]
        inter_rows.append(_dot(qd, s_prev.astype(BF16))[i:i + 1, :])
        snew_ref[i] = decay_c[:, i:i + 1] * s_prev + kv[:, i * GLA_DV:(i + 1) * GLA_DV]
    o = jnp.concatenate(inter_rows, axis=0) + qk * v
    yb_ref[...] = _gla_out(o, gn_ref[...], og_ref[...]).astype(BF16)


def _gla_step(proj, glr, state, gate_w2, gate_b, gla_norm_g, layer, stacked):
    nb = proj.shape[0]
    rows = GLA_STEP_ROWS
    q_blk = (3 * CONV_DIM) // GLA_DK
    k_blk = q_blk + GLA_HEADS
    v_blk = (3 * CONV_DIM + 2 * GLA_QK_DIM) // GLA_DV
    og_blk = v_blk + GLA_HEADS
    s_spec = pl.BlockSpec((None, rows, None, GLA_DK, GLA_DV), lambda t, h: (layer, t, h, 0, 0))
    in_specs = [pl.BlockSpec((rows, GLA_DK), lambda t, h: (t, q_blk + h)),
                pl.BlockSpec((rows, GLA_DK), lambda t, h: (t, k_blk + h)),
                pl.BlockSpec((rows, GLA_DV), lambda t, h: (t, v_blk + h)),
                pl.BlockSpec((rows, GLA_DV), lambda t, h: (t, og_blk + h)),
                pl.BlockSpec((rows, GATE_RANK), lambda t, h: (t, 0)),
                pl.BlockSpec((None, GATE_RANK, GLA_DK), lambda t, h: (layer, 0, h)),
                pl.BlockSpec((None, 1, GLA_DK), lambda t, h: (layer, 0, h)),
                pl.BlockSpec((None, 1, GLA_DV), lambda t, h: (layer, 0, h)),
                s_spec]
    args = [proj, proj, proj, proj, glr, gate_w2, gate_b, gla_norm_g, state]
    aliases = _stacked_operand(in_specs, args, stacked, 1)
    return pl.pallas_call(
        _gla_step_kernel, grid=(nb // rows, GLA_HEADS), in_specs=in_specs,
        out_specs=[pl.BlockSpec((rows, GLA_DV), lambda t, h: (t, h)), s_spec],
        out_shape=[jax.ShapeDtypeStruct((nb, GLA_V_DIM), BF16),
                   jax.ShapeDtypeStruct(state.shape, F32)],
        input_output_aliases=aliases,
        compiler_params=_params(2), name="gla_step")(*args)


def kernel(x_prompt, x_sample, state_conv, state_gla, state_ffn_conv, norm_mix_g, w_in, conv_w, gate_w2, gate_b, gla_norm_g, w_out, norm_ffn_g, w_up, ffn_conv_w, ffn_conv_b, w_down, final_norm_g):
    batch, seq, _ = x_prompt.shape
    nb = x_sample.shape[0]
    norm_mix_g, norm_ffn_g = norm_mix_g[:, None, :], norm_ffn_g[:, None, :]
    gate_b, gla_norm_g, ffn_conv_b = gate_b[:, None, :], gla_norm_g[:, None, :], ffn_conv_b[:, None, :]
    final_norm_g = final_norm_g[None, :]
    w_in_t = jnp.swapaxes(w_in, 1, 2)

    xp = x_prompt.reshape(batch * seq, D_MODEL)
    xs = x_sample.reshape(nb, D_MODEL)
    conv_p, gla_p, ffn_p = [], [], []
    conv_s, gla_s, ffn_s = None, None, None
    for l in range(DEPTH):
        proj, proj_s, glr, glr_s = _in_proj(xp, xs, norm_mix_g, w_in_t, l, 2048, 512)

        ya, c_new = _conv_mixer_prompt(proj, conv_w, l, batch, seq, 512, 512)
        yb, s_new = _gla_prompt(proj, glr, gate_w2, gate_b, gla_norm_g, l, batch, seq)
        xp = _out_proj(ya, yb, w_out, xp, l, 512)

        ya_s, conv_s = _conv_mixer_step(proj_s, state_conv, conv_w, l, 512, conv_s)
        yb_s, gla_s = _gla_step(proj_s, glr_s, state_gla, gate_w2, gate_b, gla_norm_g, l, gla_s)
        xs = _out_proj(ya_s, yb_s, w_out, xs, l, nb)

        xp, xs, f_new, ffn_s = _ffn(xp, xs, state_ffn_conv, norm_ffn_g, w_up, ffn_conv_w, ffn_conv_b, w_down,
                                    final_norm_g, l, seq, 1024, 512, l == DEPTH - 1, ffn_s)
        conv_p.append(c_new)
        gla_p.append(s_new)
        ffn_p.append(f_new)

    return (xp.reshape(batch, seq, D_MODEL), xs.reshape(nb, 1, D_MODEL),
            jnp.stack(conv_p), jnp.stack(gla_p), jnp.stack(ffn_p), conv_s, gla_s, ffn_s)
```

```python
import functools

import numpy as np
import jax
import jax.numpy as jnp
from jax import lax
from jax.experimental import pallas as pl
from jax.experimental.pallas import tpu as pltpu

F32 = jnp.float32
BF16 = jnp.bfloat16

D_MODEL = 2048
DEPTH = 2
CONV_DIM = 1024
CONV_WIDTH = 3
GLA_HEADS = 4
GLA_QK_DIM = 512
GLA_V_DIM = 1024
GLA_DK = 128
GLA_DV = 256
GATE_RANK = 16
GATE_TAU = 16.0
D_FF = 5632
EPS = 1e-6
MAIN_COLS = 3 * CONV_DIM + 2 * GLA_QK_DIM + 2 * GLA_V_DIM

GLA_CHUNK = 128
GLA_LEVELS = tuple(1 << i for i in range(GLA_CHUNK.bit_length() - 1))
GLA_FINE_LEVELS = tuple(s for s in GLA_LEVELS if s < 4)
GLA_STEP_ROWS = 8

VMEM_BYTES_V7X = 64 * 1024 * 1024
VMEM_LIMIT = VMEM_BYTES_V7X - 2 * 1024 * 1024


def _params(n_axes):
    return pltpu.CompilerParams(dimension_semantics=("arbitrary",) * n_axes,
                                vmem_limit_bytes=VMEM_LIMIT)


def _dot(a, b):
    return jnp.dot(a, b, preferred_element_type=F32)


def _dot_nt(a, b):
    return lax.dot_general(a, b, (((1,), (1,)), ((), ())), preferred_element_type=F32)


def _split3(x):
    hi = x.astype(BF16)
    r1 = x - hi.astype(F32)
    mid = r1.astype(BF16)
    lo = (r1 - mid.astype(F32)).astype(BF16)
    return hi, mid, lo


def _exact_dot01(m01, x):
    x1, x2, x3 = _split3(x)
    return _dot(m01, x1) + _dot(m01, x2) + _dot(m01, x3)


def _sigmoid(x):
    return 1.0 / (1.0 + jnp.exp(-x))


def _log_sigmoid(x):
    return jnp.minimum(x, 0.0) - jnp.log1p(jnp.exp(-jnp.abs(x)))


def _rms(x, g):
    var = jnp.mean(x * x, axis=-1, keepdims=True)
    return x * lax.rsqrt(var + EPS) * g


def _delayed_rows(u, prev):
    sub = lax.broadcasted_iota(jnp.int32, prev.shape, 0)
    r1 = pltpu.roll(u, 1, axis=0)
    r2 = pltpu.roll(u, 2, axis=0)
    head1 = jnp.where(sub < 1, pltpu.roll(prev, 1, axis=0), r1[0:8, :])
    head2 = jnp.where(sub < 2, pltpu.roll(prev, 2, axis=0), r2[0:8, :])
    return (jnp.concatenate([head1, r1[8:, :]], axis=0), jnp.concatenate([head2, r2[8:, :]], axis=0))


def _stacked_operand(in_specs, args, stacked, out_index):
    if stacked is None:
        return {}
    in_specs.append(pl.BlockSpec(memory_space=pl.ANY))
    args.append(stacked)
    return {len(args) - 1: out_index}


def _conv_tap_step(new_row, st_ref, w_ref, new_st_ref):
    p1 = st_ref[:, 1, :]
    cu = st_ref[:, 0, :] * w_ref[0:1, :]
    cu = cu + p1 * w_ref[1:2, :]
    cu = cu + new_row * w_ref[2:3, :]
    new_st_ref[:, 0, :] = p1
    new_st_ref[:, 1, :] = new_row
    return cu


def _mixer_in_kernel(x_ref, xs_ref, sconv_ref, g_ref, wa_ref, wb_ref, wc_ref, wgt_ref, cw_ref, *rest, n_conv):
    ya_ref, yas_ref, o_ref, os_ref, og_ref, ogs_ref, st_ref, sconv_new_ref, xn_ref = rest[-9:]
    j = pl.program_id(1)
    tm = x_ref.shape[0]
    ns = xs_ref.shape[0]
    tn = wa_ref.shape[0]

    @pl.when(j == 0)
    def _():
        xn_ref[0:tm, :] = _rms(x_ref[...], g_ref[...]).astype(BF16)
        xn_ref[tm:tm + ns, :] = _rms(xs_ref[...], g_ref[...]).astype(BF16)
        gate = _dot_nt(xn_ref[...], wgt_ref[...].astype(BF16))
        og_ref[...] = gate[0:tm, :]
        ogs_ref[...] = gate[tm:tm + ns, :]

    def projected():
        xn = xn_ref[...]
        return [_dot_nt(xn, w_ref[...].astype(BF16)) for w_ref in (wa_ref, wb_ref, wc_ref)]

    @pl.when(j < n_conv)
    def _():
        bg, cg, hin = projected()
        z = cg[0:tm, :] * hin[0:tm, :]
        z1, z2 = _delayed_rows(z, jnp.zeros((8, tn), F32))
        cu = z2 * cw_ref[0:1, :]
        cu = cu + z1 * cw_ref[1:2, :]
        cu = cu + z * cw_ref[2:3, :]
        ya_ref[...] = (bg[0:tm, :] * cu).astype(BF16)
        st_ref[...] = z[tm - (CONV_WIDTH - 1):tm, :]
        cu_s = _conv_tap_step(cg[tm:tm + ns, :] * hin[tm:tm + ns, :], sconv_ref, cw_ref, sconv_new_ref)
        yas_ref[...] = (bg[tm:tm + ns, :] * cu_s).astype(BF16)

    @pl.when(j >= n_conv)
    def _():
        for idx, tile in enumerate(projected()):
            cols = slice(idx * tn, (idx + 1) * tn)
            o_ref[:, cols] = tile[0:tm, :].astype(BF16)
            os_ref[:, cols] = tile[tm:tm + ns, :]


def _mixer_in(x, xs, state_conv, g, w_in_t, conv_w, layer, seq, tn, stacked):
    m, nb = x.shape[0], xs.shape[0]
    batch = m // seq
    ns = nb // batch
    n_conv = CONV_DIM // tn
    qkvg = MAIN_COLS - 3 * CONV_DIM
    n_rest = qkvg // (3 * tn)
    last_conv = n_conv - 1

    def w_spec(group):
        def index(i, j):
            conv_tile = group * n_conv + j
            rest_tile = 3 * n_conv + 3 * (j - n_conv) + group
            return (layer, jnp.where(j < n_conv, conv_tile, rest_tile), 0)
        return pl.BlockSpec((None, tn, D_MODEL), index)

    conv_col = lambda i, j: jnp.minimum(j, last_conv)
    rest_col = lambda i, j: jnp.maximum(j - n_conv, 0)
    x_spec = pl.BlockSpec((seq, D_MODEL), lambda i, j: (i, 0), pipeline_mode=pl.Buffered(1))
    xs_spec = pl.BlockSpec((ns, D_MODEL), lambda i, j: (i, 0))
    sconv_spec = pl.BlockSpec((None, ns, CONV_WIDTH - 1, tn), lambda i, j: (layer, i, 0, conv_col(i, j)))
    g_spec = pl.BlockSpec((None, 1, D_MODEL), lambda i, j: (layer, 0, 0))
    wgt_spec = pl.BlockSpec((None, GATE_RANK, D_MODEL), lambda i, j: (layer, MAIN_COLS // GATE_RANK, 0))
    cw_spec = pl.BlockSpec((None, CONV_WIDTH, tn), lambda i, j: (layer, 0, conv_col(i, j)))
    in_specs = [x_spec, xs_spec, sconv_spec, g_spec, w_spec(0), w_spec(1), w_spec(2), wgt_spec, cw_spec]
    args = [x, xs, state_conv, g, w_in_t, w_in_t, w_in_t, w_in_t, conv_w]
    aliases = _stacked_operand(in_specs, args, stacked, 7)
    out_specs = [pl.BlockSpec((seq, tn), lambda i, j: (i, conv_col(i, j))),
                 pl.BlockSpec((ns, tn), lambda i, j: (i, conv_col(i, j))),
                 pl.BlockSpec((seq, 3 * tn), lambda i, j: (i, rest_col(i, j))),
                 pl.BlockSpec((ns, 3 * tn), lambda i, j: (i, rest_col(i, j))),
                 pl.BlockSpec((seq, GATE_RANK), lambda i, j: (i, 0)),
                 pl.BlockSpec((ns, GATE_RANK), lambda i, j: (i, 0)),
                 pl.BlockSpec((None, CONV_WIDTH - 1, tn), lambda i, j: (i, 0, conv_col(i, j))),
                 sconv_spec]
    out_shape = [jax.ShapeDtypeStruct((m, CONV_DIM), BF16), jax.ShapeDtypeStruct((nb, CONV_DIM), BF16),
                 jax.ShapeDtypeStruct((m, qkvg), BF16), jax.ShapeDtypeStruct((nb, qkvg), F32),
                 jax.ShapeDtypeStruct((m, GATE_RANK), F32), jax.ShapeDtypeStruct((nb, GATE_RANK), F32),
                 jax.ShapeDtypeStruct((batch, CONV_WIDTH - 1, CONV_DIM), F32),
                 jax.ShapeDtypeStruct(state_conv.shape, F32)]
    return pl.pallas_call(
        functools.partial(_mixer_in_kernel, n_conv=n_conv), grid=(batch, n_conv + n_rest),
        in_specs=in_specs, out_specs=out_specs, out_shape=out_shape,
        scratch_shapes=[pltpu.VMEM((seq + ns, D_MODEL), BF16)],
        input_output_aliases=aliases, compiler_params=_params(2), name="mixer_in")(*args)


def _out_proj_kernel(ya_ref, yb_ref, wa_ref, wb_ref, x_ref, o_ref):
    acc = _dot(ya_ref[...], wa_ref[...].astype(BF16))
    acc = acc + _dot(yb_ref[...], wb_ref[...].astype(BF16))
    o_ref[...] = x_ref[...] + acc


def _out_proj(ya, yb, w_out, x, layer, tm):
    m = x.shape[0]
    a_spec = pl.BlockSpec((tm, CONV_DIM), lambda i: (i, 0))
    wa_spec = pl.BlockSpec((None, CONV_DIM, D_MODEL), lambda i: (layer, 0, 0), pipeline_mode=pl.Buffered(1))
    wb_spec = pl.BlockSpec((None, GLA_V_DIM, D_MODEL), lambda i: (layer, 1, 0), pipeline_mode=pl.Buffered(1))
    x_spec = pl.BlockSpec((tm, D_MODEL), lambda i: (i, 0))
    return pl.pallas_call(
        _out_proj_kernel, grid=(m // tm,), in_specs=[a_spec, a_spec, wa_spec, wb_spec, x_spec],
        out_specs=x_spec, out_shape=jax.ShapeDtypeStruct((m, D_MODEL), F32),
        compiler_params=_params(1), name="out_proj")(ya, yb, w_out, w_out, x)


def _ffn_kernel(x_ref, xs_ref, sst_ref, g_ref, wu_ref, wv_ref, cw_ref, cb_ref, wd_ref, fg_ref, *rest,
                tiles_per_seq, final_norm):
    o_ref, os_ref, st_ref, sst_new_ref, xn_ref, prev_ref, carry_ref = rest[-7:]
    i = pl.program_id(0)
    f = pl.program_id(1)
    tm = x_ref.shape[0]
    ns = xs_ref.shape[0]

    @pl.when(f == 0)
    def _():
        x = x_ref[...]
        xs = xs_ref[...]
        xn_ref[0:tm, :] = _rms(x, g_ref[...]).astype(BF16)
        xn_ref[tm:tm + ns, :] = _rms(xs, g_ref[...]).astype(BF16)
        o_ref[...] = x
        os_ref[...] = xs

    first = lax.rem(i, tiles_per_seq) == 0

    @pl.when(first)
    def _():
        prev_ref[...] = jnp.zeros(prev_ref.shape, F32)

    @pl.when(jnp.logical_not(first))
    def _():
        prev_ref[...] = carry_ref[f]

    xn = xn_ref[...]
    u_all = _dot(xn, wu_ref[...].astype(BF16))
    v_all = _dot(xn, wv_ref[...].astype(BF16))
    u = u_all[0:tm, :]
    u1, u2 = _delayed_rows(u, prev_ref[...])
    cu = u2 * cw_ref[0:1, :]
    cu = cu + u1 * cw_ref[1:2, :]
    cu = cu + u * cw_ref[2:3, :]
    carry_ref[f] = u[tm - 8:tm, :]
    st_ref[...] = u[tm - (CONV_WIDTH - 1):tm, :]
    cu_s = _conv_tap_step(u_all[tm:tm + ns, :], sst_ref, cw_ref, sst_new_ref)

    pre = jnp.concatenate([cu, cu_s], axis=0) + cb_ref[...]
    h = (pre * _sigmoid(pre) * v_all).astype(BF16)
    acc = _dot(h, wd_ref[...].astype(BF16))
    o_ref[...] += acc[0:tm, :]
    os_ref[...] += acc[tm:tm + ns, :]

    if final_norm:
        @pl.when(f == pl.num_programs(1) - 1)
        def _():
            o_ref[...] = _rms(o_ref[...], fg_ref[...])
            os_ref[...] = _rms(os_ref[...], fg_ref[...])


def _ffn(x, xs, state_s, g, w_up, conv_w, conv_b, w_down, final_g, layer, seq, tm, tf, final_norm, stacked):
    m = x.shape[0]
    n_tiles = m // tm
    ns = xs.shape[0] // n_tiles
    tiles_per_seq = seq // tm
    nf = D_FF // tf
    row_spec = pl.BlockSpec((tm, D_MODEL), lambda i, f: (i, 0), pipeline_mode=pl.Buffered(1))
    srow_spec = pl.BlockSpec((ns, D_MODEL), lambda i, f: (i, 0))
    sst_spec = pl.BlockSpec((None, ns, CONV_WIDTH - 1, tf), lambda i, f: (layer, i, 0, f))
    g_spec = pl.BlockSpec((None, 1, D_MODEL), lambda i, f: (layer, 0, 0))
    wu_spec = pl.BlockSpec((None, D_MODEL, tf), lambda i, f: (layer, 0, f))
    wv_spec = pl.BlockSpec((None, D_MODEL, tf), lambda i, f: (layer, 0, nf + f))
    cw_spec = pl.BlockSpec((None, CONV_WIDTH, tf), lambda i, f: (layer, 0, f))
    cb_spec = pl.BlockSpec((None, 1, tf), lambda i, f: (layer, 0, f))
    wd_spec = pl.BlockSpec((None, tf, D_MODEL), lambda i, f: (layer, f, 0))
    fg_spec = pl.BlockSpec((1, D_MODEL), lambda i, f: (0, 0))
    st_spec = pl.BlockSpec((None, CONV_WIDTH - 1, tf), lambda i, f: (i, 0, f))
    in_specs = [row_spec, srow_spec, sst_spec, g_spec, wu_spec, wv_spec, cw_spec, cb_spec, wd_spec, fg_spec]
    args = [x, xs, state_s, g, w_up, w_up, conv_w, conv_b, w_down, final_g]
    aliases = _stacked_operand(in_specs, args, stacked, 3)
    body = functools.partial(_ffn_kernel, tiles_per_seq=tiles_per_seq, final_norm=final_norm)
    out, out_s, tails, state_new = pl.pallas_call(
        body, grid=(n_tiles, nf), in_specs=in_specs,
        out_specs=[row_spec, srow_spec, st_spec, sst_spec],
        out_shape=[jax.ShapeDtypeStruct((m, D_MODEL), F32),
                   jax.ShapeDtypeStruct(xs.shape, F32),
                   jax.ShapeDtypeStruct((n_tiles, CONV_WIDTH - 1, D_FF), F32),
                   jax.ShapeDtypeStruct(state_s.shape, F32)],
        scratch_shapes=[pltpu.VMEM((tm + ns, D_MODEL), BF16), pltpu.VMEM((8, tf), F32),
                        pltpu.VMEM((nf, 8, tf), F32)],
        input_output_aliases=aliases, compiler_params=_params(2), name="ffn")(*args)
    return out, out_s, tails[tiles_per_seq - 1::tiles_per_seq], state_new


def _ref_rows(s, chunk):
    rows = np.arange(chunk)
    return (rows // (2 * s)) * (2 * s) + s - 1


def _gla_constants(chunk):
    rows = np.arange(chunk)
    tri = (rows[:, None] >= rows[None, :]).astype(np.float32)
    sels = [(_ref_rows(s, chunk)[:, None] == rows[None, :]).astype(np.float32) for s in GLA_FINE_LEVELS]
    return jnp.asarray(tri, BF16), jnp.asarray(np.concatenate(sels, axis=0), BF16)


def _gla_log_decay(glr, w2, gb):
    gate = _dot(glr.astype(BF16), w2.astype(BF16)) + gb
    return _log_sigmoid(gate) * (1.0 / GATE_TAU)


def _gla_decay_kernel(glr_ref, w2_ref, gb_ref, tri_ref, b_ref):
    chunk = tri_ref.shape[0]
    g = _gla_log_decay(glr_ref[...], w2_ref[...], gb_ref[...])
    tri = tri_ref[...]
    for c in range(glr_ref.shape[0] // chunk):
        rows = slice(c * chunk, (c + 1) * chunk)
        b_ref[rows, :] = _exact_dot01(tri, g[rows, :])


def _gla_decay(glr, gate_w2, gate_b, tri, layer, tg):
    m = glr.shape[0]
    return pl.pallas_call(
        _gla_decay_kernel, grid=(m // tg,),
        in_specs=[pl.BlockSpec((tg, GATE_RANK), lambda i: (i, 0)),
                  pl.BlockSpec((None, GATE_RANK, GLA_QK_DIM), lambda i: (layer, 0, 0)),
                  pl.BlockSpec((None, 1, GLA_QK_DIM), lambda i: (layer, 0, 0)),
                  pl.BlockSpec(tri.shape, lambda i: (0, 0))],
        out_specs=pl.BlockSpec((tg, GLA_QK_DIM), lambda i: (i, 0)),
        out_shape=jax.ShapeDtypeStruct((m, GLA_QK_DIM), F32),
        compiler_params=_params(1), name="gla_decay")(glr, gate_w2, gate_b, tri)


def _gla_out(o, gn, og):
    var = jnp.mean(o * o, axis=-1, keepdims=True)
    return o * lax.rsqrt(var + EPS) * gn * (og * _sigmoid(og))


def _gla_prompt_kernel(q_ref, k_ref, v_ref, og_ref, b_ref, gn_ref, sel_ref, yb_ref, sfin_ref, s_ref):
    c_idx = pl.program_id(1)
    chunk = q_ref.shape[0]

    @pl.when(c_idx == 0)
    def _():
        s_ref[...] = jnp.zeros(s_ref.shape, F32)

    b_all = b_ref[...]
    beta_fine = _exact_dot01(sel_ref[...], b_all)

    row = lax.broadcasted_iota(jnp.int32, (chunk, chunk), 0)
    col = lax.broadcasted_iota(jnp.int32, (chunk, chunk), 1)
    diff_bits = row ^ col
    scale = GLA_DK ** -0.5

    for h in range(GLA_HEADS):
        ks = slice(h * GLA_DK, (h + 1) * GLA_DK)
        vs = slice(h * GLA_DV, (h + 1) * GLA_DV)
        b = b_all[:, ks]
        b_last = b_ref[chunk - 1:chunk, ks]
        q = q_ref[:, ks].astype(F32) * scale
        k = k_ref[:, ks].astype(F32)
        v = v_ref[:, vs]
        s_prev = s_ref[h]

        inter = _dot((q * jnp.exp(b)).astype(BF16), s_prev.astype(BF16))

        acc = _dot_nt(q.astype(BF16), k.astype(BF16))
        for s in GLA_LEVELS:
            if s in GLA_FINE_LEVELS:
                li = GLA_FINE_LEVELS.index(s)
                beta = beta_fine[li * chunk:(li + 1) * chunk, ks]
            else:
                beta = jnp.concatenate(
                    [jnp.broadcast_to(b_ref[int(r):int(r) + 1, ks], (2 * s, GLA_DK))
                     for r in _ref_rows(s, chunk)[::2 * s]], axis=0)
            e = jnp.exp(-jnp.abs(b - beta))
            p = _dot_nt((q * e).astype(BF16), (k * e).astype(BF16))
            acc = jnp.where(diff_bits >= s, p, acc)
        a = jnp.where(row >= col, acc, 0.0)
        o = inter + _dot(a.astype(BF16), v)

        kdec_t = (k * jnp.exp(b_last - b)).T
        decay_t = jnp.broadcast_to(jnp.exp(b_last), (GLA_DK, GLA_DK)).T
        s_new = jnp.concatenate([decay_t, decay_t], axis=1) * s_prev + _dot(kdec_t.astype(BF16), v)
        s_ref[h] = s_new

        yb_ref[:, vs] = _gla_out(o, gn_ref[:, vs], og_ref[:, vs].astype(F32)).astype(BF16)

    @pl.when(c_idx == pl.num_programs(1) - 1)
    def _():
        sfin_ref[...] = s_ref[...]


def _gla_prompt(qkvg, glr, gate_w2, gate_b, gla_norm_g, layer, batch, seq):
    chunk = GLA_CHUNK
    nc = seq // chunk
    tri, sel = _gla_constants(chunk)
    b_all = _gla_decay(glr, gate_w2, gate_b, tri, layer, 4 * chunk)

    def col_spec(width, blk):
        return pl.BlockSpec((chunk, width), lambda b, c: (b * nc + c, blk))

    gn_spec = pl.BlockSpec((None, 1, GLA_V_DIM), lambda b, c: (layer, 0, 0))
    sel_spec = pl.BlockSpec(sel.shape, lambda b, c: (0, 0))
    yb_spec = pl.BlockSpec((chunk, GLA_V_DIM), lambda b, c: (b * nc + c, 0))
    s_spec = pl.BlockSpec((None, GLA_HEADS, GLA_DK, GLA_DV), lambda b, c: (b, 0, 0, 0))
    return pl.pallas_call(
        _gla_prompt_kernel, grid=(batch, nc),
        in_specs=[col_spec(GLA_QK_DIM, 0), col_spec(GLA_QK_DIM, 1), col_spec(GLA_V_DIM, 1),
                  col_spec(GLA_V_DIM, 2), col_spec(GLA_QK_DIM, 0), gn_spec, sel_spec],
        out_specs=[yb_spec, s_spec],
        out_shape=[jax.ShapeDtypeStruct((batch * seq, GLA_V_DIM), BF16),
                   jax.ShapeDtypeStruct((batch, GLA_HEADS, GLA_DK, GLA_DV), F32)],
        scratch_shapes=[pltpu.VMEM((GLA_HEADS, GLA_DK, GLA_DV), F32)],
        compiler_params=_params(2), name="gla_prompt")(
            qkvg, qkvg, qkvg, qkvg, b_all, gla_norm_g, sel)


def _columns(slab):
    pad = jnp.zeros((GLA_DK - GLA_STEP_ROWS, GLA_DK), F32)
    return jnp.concatenate([slab, pad], axis=0).T


def _gla_step_kernel(q_ref, k_ref, v_ref, og_ref, glr_ref, w2_ref, gb_ref, gn_ref, s_ref, *rest):
    yb_ref, snew_ref = rest[-2:]
    rows = GLA_STEP_ROWS
    decay_all = jnp.exp(_gla_log_decay(glr_ref[...], w2_ref[...], gb_ref[...]))
    row_id = lax.broadcasted_iota(jnp.int32, (rows, rows * GLA_DV), 0)
    lane_blk = lax.broadcasted_iota(jnp.int32, (rows, rows * GLA_DV), 1) // GLA_DV

    for h in range(GLA_HEADS):
        ks = slice(h * GLA_DK, (h + 1) * GLA_DK)
        vs = slice(h * GLA_DV, (h + 1) * GLA_DV)
        decay = decay_all[:, ks]
        q = q_ref[:, ks] * (GLA_DK ** -0.5)
        k = k_ref[:, ks]
        v = v_ref[:, vs]
        qk = jnp.sum(q * k, axis=-1, keepdims=True)

        v_diag = jnp.where(row_id == lane_blk, jnp.concatenate([v] * rows, axis=1), 0.0)
        v_diag = jnp.concatenate([v_diag, jnp.zeros_like(v_diag)], axis=0)
        k_t = _columns(k)[:, 0:2 * rows]
        kv = _dot(k_t.astype(BF16), v_diag.astype(BF16))

        decay_c = _columns(decay)
        qd = (q * decay).astype(BF16)
        inter_rows = []
        for i in range(rows):
            s_prev = s_ref[i, h]
            inter_rows.append(_dot(qd, s_prev.astype(BF16))[i:i + 1, :])
            snew_ref[i, h] = decay_c[:, i:i + 1] * s_prev + kv[:, i * GLA_DV:(i + 1) * GLA_DV]
        o = jnp.concatenate(inter_rows, axis=0) + qk * v
        yb_ref[:, vs] = _gla_out(o, gn_ref[:, vs], og_ref[:, vs]).astype(BF16)


def _gla_step(qkvg, glr, state, gate_w2, gate_b, gla_norm_g, layer, stacked):
    nb = qkvg.shape[0]
    rows = GLA_STEP_ROWS
    s_spec = pl.BlockSpec((None, rows, GLA_HEADS, GLA_DK, GLA_DV), lambda t: (layer, t, 0, 0, 0))
    in_specs = [pl.BlockSpec((rows, GLA_QK_DIM), lambda t: (t, 0)),
                pl.BlockSpec((rows, GLA_QK_DIM), lambda t: (t, 1)),
                pl.BlockSpec((rows, GLA_V_DIM), lambda t: (t, 1)),
                pl.BlockSpec((rows, GLA_V_DIM), lambda t: (t, 2)),
                pl.BlockSpec((rows, GATE_RANK), lambda t: (t, 0)),
                pl.BlockSpec((None, GATE_RANK, GLA_QK_DIM), lambda t: (layer, 0, 0)),
                pl.BlockSpec((None, 1, GLA_QK_DIM), lambda t: (layer, 0, 0)),
                pl.BlockSpec((None, 1, GLA_V_DIM), lambda t: (layer, 0, 0)),
                s_spec]
    args = [qkvg, qkvg, qkvg, qkvg, glr, gate_w2, gate_b, gla_norm_g, state]
    aliases = _stacked_operand(in_specs, args, stacked, 1)
    return pl.pallas_call(
        _gla_step_kernel, grid=(nb // rows,), in_specs=in_specs,
        out_specs=[pl.BlockSpec((rows, GLA_V_DIM), lambda t: (t, 0)), s_spec],
        out_shape=[jax.ShapeDtypeStruct((nb, GLA_V_DIM), BF16),
                   jax.ShapeDtypeStruct(state.shape, F32)],
        input_output_aliases=aliases,
        compiler_params=_params(1), name="gla_step")(*args)


def kernel(x_prompt, x_sample, state_conv, state_gla, state_ffn_conv, norm_mix_g, w_in, conv_w, gate_w2, gate_b, gla_norm_g, w_out, norm_ffn_g, w_up, ffn_conv_w, ffn_conv_b, w_down, final_norm_g):
    batch, seq, _ = x_prompt.shape
    nb = x_sample.shape[0]
    norm_mix_g, norm_ffn_g = norm_mix_g[:, None, :], norm_ffn_g[:, None, :]
    gate_b, gla_norm_g, ffn_conv_b = gate_b[:, None, :], gla_norm_g[:, None, :], ffn_conv_b[:, None, :]
    final_norm_g = final_norm_g[None, :]
    w_in_t = jnp.swapaxes(w_in, 1, 2)

    xp = x_prompt.reshape(batch * seq, D_MODEL)
    xs = x_sample.reshape(nb, D_MODEL)
    conv_p, gla_p, ffn_p = [], [], []
    conv_s, gla_s, ffn_s = None, None, None
    for l in range(DEPTH):
        ya, ya_s, qkvg, qkvg_s, glr, glr_s, c_new, conv_s = _mixer_in(
            xp, xs, state_conv, norm_mix_g, w_in_t, conv_w, l, seq, 256, conv_s)

        yb, s_new = _gla_prompt(qkvg, glr, gate_w2, gate_b, gla_norm_g, l, batch, seq)
        xp = _out_proj(ya, yb, w_out, xp, l, 512)

        yb_s, gla_s = _gla_step(qkvg_s, glr_s, state_gla, gate_w2, gate_b, gla_norm_g, l, gla_s)
        xs = _out_proj(ya_s, yb_s, w_out, xs, l, nb)

        xp, xs, f_new, ffn_s = _ffn(xp, xs, state_ffn_conv, norm_ffn_g, w_up, ffn_conv_w, ffn_conv_b, w_down,
                                    final_norm_g, l, seq, 1024, 512, l == DEPTH - 1, ffn_s)
        conv_p.append(c_new)
        gla_p.append(s_new)
        ffn_p.append(f_new)

    return (xp.reshape(batch, seq, D_MODEL), xs.reshape(nb, 1, D_MODEL),
            jnp.stack(conv_p), jnp.stack(gla_p), jnp.stack(ffn_p), conv_s, gla_s, ffn_s)
```

```python
import functools

import numpy as np
import jax
import jax.numpy as jnp
from jax import lax
from jax.experimental import pallas as pl
from jax.experimental.pallas import tpu as pltpu

F32 = jnp.float32
BF16 = jnp.bfloat16

D_MODEL = 2048
DEPTH = 2
CONV_DIM = 1024
CONV_WIDTH = 3
GLA_HEADS = 4
GLA_QK_DIM = 512
GLA_V_DIM = 1024
GLA_DK = 128
GLA_DV = 256
GATE_RANK = 16
GATE_TAU = 16.0
D_FF = 5632
EPS = 1e-6
MAIN_COLS = 3 * CONV_DIM + 2 * GLA_QK_DIM + 2 * GLA_V_DIM

GLA_CHUNK = 128
GLA_LEVELS = tuple(1 << i for i in range(GLA_CHUNK.bit_length() - 1))
GLA_FINE_LEVELS = tuple(s for s in GLA_LEVELS if s < 4)
GLA_STEP_ROWS = 8

VMEM_BYTES_V7X = 64 * 1024 * 1024
VMEM_LIMIT = VMEM_BYTES_V7X - 2 * 1024 * 1024


def _params(n_axes):
    return pltpu.CompilerParams(dimension_semantics=("arbitrary",) * n_axes,
                                vmem_limit_bytes=VMEM_LIMIT)


def _dot(a, b):
    return jnp.dot(a, b, preferred_element_type=F32)


def _dot_nt(a, b):
    return lax.dot_general(a, b, (((1,), (1,)), ((), ())), preferred_element_type=F32)


def _split3(x):
    hi = x.astype(BF16)
    r1 = x - hi.astype(F32)
    mid = r1.astype(BF16)
    lo = (r1 - mid.astype(F32)).astype(BF16)
    return hi, mid, lo


def _exact_dot01(m01, x):
    x1, x2, x3 = _split3(x)
    return _dot(m01, x1) + _dot(m01, x2) + _dot(m01, x3)


def _sigmoid(x):
    return 1.0 / (1.0 + jnp.exp(-x))


def _log_sigmoid(x):
    return jnp.minimum(x, 0.0) - jnp.log1p(jnp.exp(-jnp.abs(x)))


def _rms(x, g):
    var = jnp.mean(x * x, axis=-1, keepdims=True)
    return x * lax.rsqrt(var + EPS) * g


def _delayed_rows(u, prev):
    sub = lax.broadcasted_iota(jnp.int32, prev.shape, 0)
    r1 = pltpu.roll(u, 1, axis=0)
    r2 = pltpu.roll(u, 2, axis=0)
    head1 = jnp.where(sub < 1, pltpu.roll(prev, 1, axis=0), r1[0:8, :])
    head2 = jnp.where(sub < 2, pltpu.roll(prev, 2, axis=0), r2[0:8, :])
    return (jnp.concatenate([head1, r1[8:, :]], axis=0), jnp.concatenate([head2, r2[8:, :]], axis=0))


def _stacked_operand(in_specs, args, stacked, out_index):
    if stacked is None:
        return {}
    in_specs.append(pl.BlockSpec(memory_space=pl.ANY))
    args.append(stacked)
    return {len(args) - 1: out_index}


def _conv_tap_step(new_row, st_ref, w_ref, new_st_ref):
    p1 = st_ref[:, 1, :]
    cu = st_ref[:, 0, :] * w_ref[0:1, :]
    cu = cu + p1 * w_ref[1:2, :]
    cu = cu + new_row * w_ref[2:3, :]
    new_st_ref[:, 0, :] = p1
    new_st_ref[:, 1, :] = new_row
    return cu


def _mixer_in_kernel(x_ref, xs_ref, sconv_ref, g_ref, wa_ref, wb_ref, wc_ref, wgt_ref, cw_ref, *rest, n_conv):
    ya_ref, yas_ref, o_ref, os_ref, og_ref, ogs_ref, st_ref, sconv_new_ref, xn_ref = rest[-9:]
    j = pl.program_id(1)
    tm = x_ref.shape[0]
    ns = xs_ref.shape[0]
    tn = wa_ref.shape[0]

    @pl.when(j == 0)
    def _():
        xn_ref[0:tm, :] = _rms(x_ref[...], g_ref[...]).astype(BF16)
        xn_ref[tm:tm + ns, :] = _rms(xs_ref[...], g_ref[...]).astype(BF16)
        gate = _dot_nt(xn_ref[...], wgt_ref[...].astype(BF16))
        og_ref[...] = gate[0:tm, :]
        ogs_ref[...] = gate[tm:tm + ns, :]

    def projected():
        xn = xn_ref[...]
        return [_dot_nt(xn, w_ref[...].astype(BF16)) for w_ref in (wa_ref, wb_ref, wc_ref)]

    @pl.when(j < n_conv)
    def _():
        bg, cg, hin = projected()
        z = cg[0:tm, :] * hin[0:tm, :]
        z1, z2 = _delayed_rows(z, jnp.zeros((8, tn), F32))
        cu = z2 * cw_ref[0:1, :]
        cu = cu + z1 * cw_ref[1:2, :]
        cu = cu + z * cw_ref[2:3, :]
        ya_ref[...] = (bg[0:tm, :] * cu).astype(BF16)
        st_ref[...] = z[tm - (CONV_WIDTH - 1):tm, :]
        cu_s = _conv_tap_step(cg[tm:tm + ns, :] * hin[tm:tm + ns, :], sconv_ref, cw_ref, sconv_new_ref)
        yas_ref[...] = (bg[tm:tm + ns, :] * cu_s).astype(BF16)

    @pl.when(j >= n_conv)
    def _():
        for idx, tile in enumerate(projected()):
            cols = slice(idx * tn, (idx + 1) * tn)
            o_ref[:, cols] = tile[0:tm, :].astype(BF16)
            os_ref[:, cols] = tile[tm:tm + ns, :]


def _mixer_in(x, xs, state_conv, g, w_in_t, conv_w, layer, seq, tn, stacked):
    m, nb = x.shape[0], xs.shape[0]
    batch = m // seq
    ns = nb // batch
    n_conv = CONV_DIM // tn
    qkvg = MAIN_COLS - 3 * CONV_DIM
    n_rest = qkvg // (3 * tn)
    last_conv = n_conv - 1

    def w_spec(group):
        def index(i, j):
            conv_tile = group * n_conv + j
            rest_tile = 3 * n_conv + 3 * (j - n_conv) + group
            return (layer, jnp.where(j < n_conv, conv_tile, rest_tile), 0)
        return pl.BlockSpec((None, tn, D_MODEL), index)

    conv_col = lambda i, j: jnp.minimum(j, last_conv)
    rest_col = lambda i, j: jnp.maximum(j - n_conv, 0)
    x_spec = pl.BlockSpec((seq, D_MODEL), lambda i, j: (i, 0), pipeline_mode=pl.Buffered(1))
    xs_spec = pl.BlockSpec((ns, D_MODEL), lambda i, j: (i, 0))
    sconv_spec = pl.BlockSpec((None, ns, CONV_WIDTH - 1, tn), lambda i, j: (layer, i, 0, conv_col(i, j)))
    g_spec = pl.BlockSpec((None, 1, D_MODEL), lambda i, j: (layer, 0, 0))
    wgt_spec = pl.BlockSpec((None, GATE_RANK, D_MODEL), lambda i, j: (layer, MAIN_COLS // GATE_RANK, 0))
    cw_spec = pl.BlockSpec((None, CONV_WIDTH, tn), lambda i, j: (layer, 0, conv_col(i, j)))
    in_specs = [x_spec, xs_spec, sconv_spec, g_spec, w_spec(0), w_spec(1), w_spec(2), wgt_spec, cw_spec]
    args = [x, xs, state_conv, g, w_in_t, w_in_t, w_in_t, w_in_t, conv_w]
    aliases = _stacked_operand(in_specs, args, stacked, 7)
    out_specs = [pl.BlockSpec((seq, tn), lambda i, j: (i, conv_col(i, j))),
                 pl.BlockSpec((ns, tn), lambda i, j: (i, conv_col(i, j))),
                 pl.BlockSpec((seq, 3 * tn), lambda i, j: (i, rest_col(i, j))),
                 pl.BlockSpec((ns, 3 * tn), lambda i, j: (i, rest_col(i, j))),
                 pl.BlockSpec((seq, GATE_RANK), lambda i, j: (i, 0)),
                 pl.BlockSpec((ns, GATE_RANK), lambda i, j: (i, 0)),
                 pl.BlockSpec((None, CONV_WIDTH - 1, tn), lambda i, j: (i, 0, conv_col(i, j))),
                 sconv_spec]
    out_shape = [jax.ShapeDtypeStruct((m, CONV_DIM), BF16), jax.ShapeDtypeStruct((nb, CONV_DIM), BF16),
                 jax.ShapeDtypeStruct((m, qkvg), BF16), jax.ShapeDtypeStruct((nb, qkvg), F32),
                 jax.ShapeDtypeStruct((m, GATE_RANK), F32), jax.ShapeDtypeStruct((nb, GATE_RANK), F32),
                 jax.ShapeDtypeStruct((batch, CONV_WIDTH - 1, CONV_DIM), F32),
                 jax.ShapeDtypeStruct(state_conv.shape, F32)]
    return pl.pallas_call(
        functools.partial(_mixer_in_kernel, n_conv=n_conv), grid=(batch, n_conv + n_rest),
        in_specs=in_specs, out_specs=out_specs, out_shape=out_shape,
        scratch_shapes=[pltpu.VMEM((seq + ns, D_MODEL), BF16)],
        input_output_aliases=aliases, compiler_params=_params(2), name="mixer_in")(*args)


def _out_proj_kernel(ya_ref, yb_ref, wa_ref, wb_ref, x_ref, o_ref):
    acc = _dot(ya_ref[...], wa_ref[...].astype(BF16))
    acc = acc + _dot(yb_ref[...], wb_ref[...].astype(BF16))
    o_ref[...] = x_ref[...] + acc


def _out_proj(ya, yb, w_out, x, layer, tm):
    m = x.shape[0]
    a_spec = pl.BlockSpec((tm, CONV_DIM), lambda i: (i, 0))
    wa_spec = pl.BlockSpec((None, CONV_DIM, D_MODEL), lambda i: (layer, 0, 0), pipeline_mode=pl.Buffered(1))
    wb_spec = pl.BlockSpec((None, GLA_V_DIM, D_MODEL), lambda i: (layer, 1, 0), pipeline_mode=pl.Buffered(1))
    x_spec = pl.BlockSpec((tm, D_MODEL), lambda i: (i, 0))
    return pl.pallas_call(
        _out_proj_kernel, grid=(m // tm,), in_specs=[a_spec, a_spec, wa_spec, wb_spec, x_spec],
        out_specs=x_spec, out_shape=jax.ShapeDtypeStruct((m, D_MODEL), F32),
        compiler_params=_params(1), name="out_proj")(ya, yb, w_out, w_out, x)


def _ffn_kernel(x_ref, xs_ref, sst_ref, g_ref, wu_ref, wv_ref, cw_ref, cb_ref, wd_ref, fg_ref, *rest,
                tiles_per_seq, final_norm):
    o_ref, os_ref, st_ref, sst_new_ref, xn_ref, prev_ref, carry_ref = rest[-7:]
    i = pl.program_id(0)
    f = pl.program_id(1)
    tm = x_ref.shape[0]
    ns = xs_ref.shape[0]

    @pl.when(f == 0)
    def _():
        x = x_ref[...]
        xs = xs_ref[...]
        xn_ref[0:tm, :] = _rms(x, g_ref[...]).astype(BF16)
        xn_ref[tm:tm + ns, :] = _rms(xs, g_ref[...]).astype(BF16)
        o_ref[...] = x
        os_ref[...] = xs

    first = lax.rem(i, tiles_per_seq) == 0

    @pl.when(first)
    def _():
        prev_ref[...] = jnp.zeros(prev_ref.shape, F32)

    @pl.when(jnp.logical_not(first))
    def _():
        prev_ref[...] = carry_ref[f]

    xn = xn_ref[...]
    u_all = _dot(xn, wu_ref[...].astype(BF16))
    v_all = _dot(xn, wv_ref[...].astype(BF16))
    u = u_all[0:tm, :]
    u1, u2 = _delayed_rows(u, prev_ref[...])
    cu = u2 * cw_ref[0:1, :]
    cu = cu + u1 * cw_ref[1:2, :]
    cu = cu + u * cw_ref[2:3, :]
    carry_ref[f] = u[tm - 8:tm, :]
    st_ref[...] = u[tm - (CONV_WIDTH - 1):tm, :]
    cu_s = _conv_tap_step(u_all[tm:tm + ns, :], sst_ref, cw_ref, sst_new_ref)

    pre = jnp.concatenate([cu, cu_s], axis=0) + cb_ref[...]
    h = (pre * _sigmoid(pre) * v_all).astype(BF16)
    acc = _dot(h, wd_ref[...].astype(BF16))
    o_ref[...] += acc[0:tm, :]
    os_ref[...] += acc[tm:tm + ns, :]

    if final_norm:
        @pl.when(f == pl.num_programs(1) - 1)
        def _():
            o_ref[...] = _rms(o_ref[...], fg_ref[...])
            os_ref[...] = _rms(os_ref[...], fg_ref[...])


def _ffn(x, xs, state_s, g, w_up, conv_w, conv_b, w_down, final_g, layer, seq, tm, tf, final_norm, stacked):
    m = x.shape[0]
    n_tiles = m // tm
    ns = xs.shape[0] // n_tiles
    tiles_per_seq = seq // tm
    nf = D_FF // tf
    row_spec = pl.BlockSpec((tm, D_MODEL), lambda i, f: (i, 0), pipeline_mode=pl.Buffered(1))
    srow_spec = pl.BlockSpec((ns, D_MODEL), lambda i, f: (i, 0))
    sst_spec = pl.BlockSpec((None, ns, CONV_WIDTH - 1, tf), lambda i, f: (layer, i, 0, f))
    g_spec = pl.BlockSpec((None, 1, D_MODEL), lambda i, f: (layer, 0, 0))
    wu_spec = pl.BlockSpec((None, D_MODEL, tf), lambda i, f: (layer, 0, f))
    wv_spec = pl.BlockSpec((None, D_MODEL, tf), lambda i, f: (layer, 0, nf + f))
    cw_spec = pl.BlockSpec((None, CONV_WIDTH, tf), lambda i, f: (layer, 0, f))
    cb_spec = pl.BlockSpec((None, 1, tf), lambda i, f: (layer, 0, f))
    wd_spec = pl.BlockSpec((None, tf, D_MODEL), lambda i, f: (layer, f, 0))
    fg_spec = pl.BlockSpec((1, D_MODEL), lambda i, f: (0, 0))
    st_spec = pl.BlockSpec((None, CONV_WIDTH - 1, tf), lambda i, f: (i, 0, f))
    x_spec = pl.BlockSpec((tm, D_MODEL), lambda i, f: (i, 0))
    in_specs = [x_spec, srow_spec, sst_spec, g_spec, wu_spec, wv_spec, cw_spec, cb_spec, wd_spec, fg_spec]
    args = [x, xs, state_s, g, w_up, w_up, conv_w, conv_b, w_down, final_g]
    aliases = _stacked_operand(in_specs, args, stacked, 3)
    body = functools.partial(_ffn_kernel, tiles_per_seq=tiles_per_seq, final_norm=final_norm)
    out, out_s, tails, state_new = pl.pallas_call(
        body, grid=(n_tiles, nf), in_specs=in_specs,
        out_specs=[row_spec, srow_spec, st_spec, sst_spec],
        out_shape=[jax.ShapeDtypeStruct((m, D_MODEL), F32),
                   jax.ShapeDtypeStruct(xs.shape, F32),
                   jax.ShapeDtypeStruct((n_tiles, CONV_WIDTH - 1, D_FF), F32),
                   jax.ShapeDtypeStruct(state_s.shape, F32)],
        scratch_shapes=[pltpu.VMEM((tm + ns, D_MODEL), BF16), pltpu.VMEM((8, tf), F32),
                        pltpu.VMEM((nf, 8, tf), F32)],
        input_output_aliases=aliases, compiler_params=_params(2), name="ffn")(*args)
    return out, out_s, tails[tiles_per_seq - 1::tiles_per_seq], state_new


def _ref_rows(s, chunk):
    rows = np.arange(chunk)
    return (rows // (2 * s)) * (2 * s) + s - 1


def _gla_constants(chunk):
    rows = np.arange(chunk)
    tri = (rows[:, None] >= rows[None, :]).astype(np.float32)
    sels = [(_ref_rows(s, chunk)[:, None] == rows[None, :]).astype(np.float32) for s in GLA_FINE_LEVELS]
    sgns = [np.where(rows > _ref_rows(s, chunk), 1.0, -1.0) * np.log2(np.e) for s in GLA_LEVELS]
    sgn = np.broadcast_to(np.concatenate(sgns)[:, None], (len(GLA_LEVELS) * chunk, GLA_DK))
    return (jnp.asarray(tri, BF16), jnp.asarray(np.concatenate(sels, axis=0), BF16),
            jnp.asarray(sgn, F32))


def _gla_log_decay(glr, w2, gb):
    gate = _dot(glr.astype(BF16), w2.astype(BF16)) + gb
    return _log_sigmoid(gate) * (1.0 / GATE_TAU)


def _gla_decay_kernel(glr_ref, w2_ref, gb_ref, tri_ref, b_ref):
    chunk = tri_ref.shape[0]
    g = _gla_log_decay(glr_ref[...], w2_ref[...], gb_ref[...])
    tri = tri_ref[...]
    for c in range(glr_ref.shape[0] // chunk):
        rows = slice(c * chunk, (c + 1) * chunk)
        b_ref[rows, :] = _exact_dot01(tri, g[rows, :])


def _gla_decay(glr, gate_w2, gate_b, tri, layer, tg):
    m = glr.shape[0]
    return pl.pallas_call(
        _gla_decay_kernel, grid=(m // tg,),
        in_specs=[pl.BlockSpec((tg, GATE_RANK), lambda i: (i, 0)),
                  pl.BlockSpec((None, GATE_RANK, GLA_QK_DIM), lambda i: (layer, 0, 0)),
                  pl.BlockSpec((None, 1, GLA_QK_DIM), lambda i: (layer, 0, 0)),
                  pl.BlockSpec(tri.shape, lambda i: (0, 0))],
        out_specs=pl.BlockSpec((tg, GLA_QK_DIM), lambda i: (i, 0)),
        out_shape=jax.ShapeDtypeStruct((m, GLA_QK_DIM), F32),
        compiler_params=_params(1), name="gla_decay")(glr, gate_w2, gate_b, tri)


def _gla_out(o, gn, og):
    var = jnp.mean(o * o, axis=-1, keepdims=True)
    return o * lax.rsqrt(var + EPS) * gn * (og * _sigmoid(og))


def _gla_prompt_kernel(q_ref, k_ref, v_ref, og_ref, b_ref, bh0_ref, bh1_ref, bh2_ref, bh3_ref,
                       gn_ref, sel_ref, sgn_ref, yb_ref, sfin_ref, s_ref):
    bh_refs = (bh0_ref, bh1_ref, bh2_ref, bh3_ref)
    c_idx = pl.program_id(1)
    chunk = q_ref.shape[0]

    @pl.when(c_idx == 0)
    def _():
        s_ref[...] = jnp.zeros(s_ref.shape, F32)

    b_all = b_ref[...]
    beta_fine = _exact_dot01(sel_ref[...], b_all)

    row = lax.broadcasted_iota(jnp.int32, (chunk, chunk), 0)
    col = lax.broadcasted_iota(jnp.int32, (chunk, chunk), 1)
    diff_bits = row ^ col

    for h in range(GLA_HEADS):
        ks = slice(h * GLA_DK, (h + 1) * GLA_DK)
        vs = slice(h * GLA_DV, (h + 1) * GLA_DV)
        b = b_all[:, ks]
        b_last = b_ref[chunk - 1:chunk, ks]
        q = q_ref[:, ks]
        k = k_ref[:, ks]
        v = v_ref[:, vs]
        s_prev = s_ref[h]

        inter = _dot(q * jnp.exp(b).astype(BF16), s_prev.astype(BF16))

        acc = _dot_nt(q, k)
        for li, s in enumerate(GLA_LEVELS):
            if s in GLA_FINE_LEVELS:
                fi = GLA_FINE_LEVELS.index(s)
                beta = beta_fine[fi * chunk:(fi + 1) * chunk, ks]
            else:
                beta = jnp.concatenate(
                    [bh_refs[h][pl.ds(int(r), 8, stride=0), :]
                     for r in _ref_rows(s, chunk)[::2 * s] for _ in range(2 * s // 8)], axis=0)
            e = jnp.exp2((b - beta) * sgn_ref[li * chunk:(li + 1) * chunk, :]).astype(BF16)
            p = _dot_nt(q * e, k * e)
            acc = jnp.where(diff_bits >= s, p, acc)
        a = jnp.where(row >= col, acc, 0.0)
        o = (inter + _dot(a.astype(BF16), v)) * (GLA_DK ** -0.5)

        kdec_t = (k.astype(F32) * jnp.exp(b_last - b)).T
        decay_t = jnp.broadcast_to(jnp.exp(b_last), (GLA_DK, GLA_DK)).T
        s_new = jnp.concatenate([decay_t, decay_t], axis=1) * s_prev + _dot(kdec_t.astype(BF16), v)
        s_ref[h] = s_new

        yb_ref[:, vs] = _gla_out(o, gn_ref[:, vs], og_ref[:, vs].astype(F32)).astype(BF16)

    @pl.when(c_idx == pl.num_programs(1) - 1)
    def _():
        sfin_ref[...] = s_ref[...]


def _gla_prompt(qkvg, glr, gate_w2, gate_b, gla_norm_g, layer, batch, seq):
    chunk = GLA_CHUNK
    nc = seq // chunk
    tri, sel, sgn = _gla_constants(chunk)
    b_all = _gla_decay(glr, gate_w2, gate_b, tri, layer, 4 * chunk)

    def col_spec(width, blk):
        return pl.BlockSpec((chunk, width), lambda b, c: (b * nc + c, blk))

    gn_spec = pl.BlockSpec((None, 1, GLA_V_DIM), lambda b, c: (layer, 0, 0))
    sel_spec = pl.BlockSpec(sel.shape, lambda b, c: (0, 0))
    sgn_spec = pl.BlockSpec(sgn.shape, lambda b, c: (0, 0))
    yb_spec = pl.BlockSpec((chunk, GLA_V_DIM), lambda b, c: (b * nc + c, 0))
    s_spec = pl.BlockSpec((None, GLA_HEADS, GLA_DK, GLA_DV), lambda b, c: (b, 0, 0, 0))
    return pl.pallas_call(
        _gla_prompt_kernel, grid=(batch, nc),
        in_specs=[col_spec(GLA_QK_DIM, 0), col_spec(GLA_QK_DIM, 1), col_spec(GLA_V_DIM, 1),
                  col_spec(GLA_V_DIM, 2), col_spec(GLA_QK_DIM, 0)]
                 + [col_spec(GLA_DK, h) for h in range(GLA_HEADS)] + [gn_spec, sel_spec, sgn_spec],
        out_specs=[yb_spec, s_spec],
        out_shape=[jax.ShapeDtypeStruct((batch * seq, GLA_V_DIM), BF16),
                   jax.ShapeDtypeStruct((batch, GLA_HEADS, GLA_DK, GLA_DV), F32)],
        scratch_shapes=[pltpu.VMEM((GLA_HEADS, GLA_DK, GLA_DV), F32)],
        compiler_params=_params(2), name="gla_prompt")(
            qkvg, qkvg, qkvg, qkvg, b_all, *([b_all] * GLA_HEADS), gla_norm_g, sel, sgn)


def _columns(slab):
    pad = jnp.zeros((GLA_DK - GLA_STEP_ROWS, GLA_DK), F32)
    return jnp.concatenate([slab, pad], axis=0).T


def _gla_step_kernel(q_ref, k_ref, v_ref, og_ref, glr_ref, w2_ref, gb_ref, gn_ref, s_ref, *rest):
    yb_ref, snew_ref = rest[-2:]
    rows = GLA_STEP_ROWS
    decay_all = jnp.exp(_gla_log_decay(glr_ref[...], w2_ref[...], gb_ref[...]))
    row_id = lax.broadcasted_iota(jnp.int32, (rows, rows * GLA_DV), 0)
    lane_blk = lax.broadcasted_iota(jnp.int32, (rows, rows * GLA_DV), 1) // GLA_DV

    for h in range(GLA_HEADS):
        ks = slice(h * GLA_DK, (h + 1) * GLA_DK)
        vs = slice(h * GLA_DV, (h + 1) * GLA_DV)
        decay = decay_all[:, ks]
        q = q_ref[:, ks] * (GLA_DK ** -0.5)
        k = k_ref[:, ks]
        v = v_ref[:, vs]
        qk = jnp.sum(q * k, axis=-1, keepdims=True)

        v_diag = jnp.where(row_id == lane_blk, jnp.concatenate([v] * rows, axis=1), 0.0)
        v_diag = jnp.concatenate([v_diag, jnp.zeros_like(v_diag)], axis=0)
        k_t = _columns(k)[:, 0:2 * rows]
        kv = _dot(k_t.astype(BF16), v_diag.astype(BF16))

        decay_c = _columns(decay)
        qd = (q * decay).astype(BF16)
        inter_rows = []
        for i in range(rows):
            s_prev = s_ref[i, h]
            inter_rows.append(_dot(qd, s_prev.astype(BF16))[i:i + 1, :])
            snew_ref[i, h] = decay_c[:, i:i + 1] * s_prev + kv[:, i * GLA_DV:(i + 1) * GLA_DV]
        o = jnp.concatenate(inter_rows, axis=0) + qk * v
        yb_ref[:, vs] = _gla_out(o, gn_ref[:, vs], og_ref[:, vs]).astype(BF16)


def _gla_step(qkvg, glr, state, gate_w2, gate_b, gla_norm_g, layer, stacked):
    nb = qkvg.shape[0]
    rows = GLA_STEP_ROWS
    s_spec = pl.BlockSpec((None, rows, GLA_HEADS, GLA_DK, GLA_DV), lambda t: (layer, t, 0, 0, 0))
    in_specs = [pl.BlockSpec((rows, GLA_QK_DIM), lambda t: (t, 0)),
                pl.BlockSpec((rows, GLA_QK_DIM), lambda t: (t, 1)),
                pl.BlockSpec((rows, GLA_V_DIM), lambda t: (t, 1)),
                pl.BlockSpec((rows, GLA_V_DIM), lambda t: (t, 2)),
                pl.BlockSpec((rows, GATE_RANK), lambda t: (t, 0)),
                pl.BlockSpec((None, GATE_RANK, GLA_QK_DIM), lambda t: (layer, 0, 0)),
                pl.BlockSpec((None, 1, GLA_QK_DIM), lambda t: (layer, 0, 0)),
                pl.BlockSpec((None, 1, GLA_V_DIM), lambda t: (layer, 0, 0)),
                s_spec]
    args = [qkvg, qkvg, qkvg, qkvg, glr, gate_w2, gate_b, gla_norm_g, state]
    aliases = _stacked_operand(in_specs, args, stacked, 1)
    return pl.pallas_call(
        _gla_step_kernel, grid=(nb // rows,), in_specs=in_specs,
        out_specs=[pl.BlockSpec((rows, GLA_V_DIM), lambda t: (t, 0)), s_spec],
        out_shape=[jax.ShapeDtypeStruct((nb, GLA_V_DIM), BF16),
                   jax.ShapeDtypeStruct(state.shape, F32)],
        input_output_aliases=aliases,
        compiler_params=_params(1), name="gla_step")(*args)


def kernel(x_prompt, x_sample, state_conv, state_gla, state_ffn_conv, norm_mix_g, w_in, conv_w, gate_w2, gate_b, gla_norm_g, w_out, norm_ffn_g, w_up, ffn_conv_w, ffn_conv_b, w_down, final_norm_g):
    batch, seq, _ = x_prompt.shape
    nb = x_sample.shape[0]
    norm_mix_g, norm_ffn_g = norm_mix_g[:, None, :], norm_ffn_g[:, None, :]
    gate_b, gla_norm_g, ffn_conv_b = gate_b[:, None, :], gla_norm_g[:, None, :], ffn_conv_b[:, None, :]
    final_norm_g = final_norm_g[None, :]
    w_in_t = jnp.swapaxes(w_in, 1, 2)

    xp = x_prompt.reshape(batch * seq, D_MODEL)
    xs = x_sample.reshape(nb, D_MODEL)
    conv_p, gla_p, ffn_p = [], [], []
    conv_s, gla_s, ffn_s = None, None, None
    for l in range(DEPTH):
        ya, ya_s, qkvg, qkvg_s, glr, glr_s, c_new, conv_s = _mixer_in(
            xp, xs, state_conv, norm_mix_g, w_in_t, conv_w, l, seq, 256, conv_s)

        yb, s_new = _gla_prompt(qkvg, glr, gate_w2, gate_b, gla_norm_g, l, batch, seq)
        xp = _out_proj(ya, yb, w_out, xp, l, 512)

        yb_s, gla_s = _gla_step(qkvg_s, glr_s, state_gla, gate_w2, gate_b, gla_norm_g, l, gla_s)
        xs = _out_proj(ya_s, yb_s, w_out, xs, l, nb)

        xp, xs, f_new, ffn_s = _ffn(xp, xs, state_ffn_conv, norm_ffn_g, w_up, ffn_conv_w, ffn_conv_b, w_down,
                                    final_norm_g, l, seq, 1024, 512, l == DEPTH - 1, ffn_s)
        conv_p.append(c_new)
        gla_p.append(s_new)
        ffn_p.append(f_new)

    return (xp.reshape(batch, seq, D_MODEL), xs.reshape(nb, 1, D_MODEL),
            jnp.stack(conv_p), jnp.stack(gla_p), jnp.stack(ffn_p), conv_s, gla_s, ffn_s)
```

```python
import functools

import numpy as np
import jax
import jax.numpy as jnp
from jax import lax
from jax.experimental import pallas as pl
from jax.experimental.pallas import tpu as pltpu

F32 = jnp.float32
BF16 = jnp.bfloat16

D_MODEL = 2048
DEPTH = 2
CONV_DIM = 1024
CONV_WIDTH = 3
GLA_HEADS = 4
GLA_QK_DIM = 512
GLA_V_DIM = 1024
GLA_DK = 128
GLA_DV = 256
GATE_RANK = 16
GATE_TAU = 16.0
D_FF = 5632
EPS = 1e-6
MAIN_COLS = 3 * CONV_DIM + 2 * GLA_QK_DIM + 2 * GLA_V_DIM

GLA_CHUNK = 128
GLA_LEVELS = tuple(1 << i for i in range(GLA_CHUNK.bit_length() - 1))
GLA_FINE_LEVELS = tuple(s for s in GLA_LEVELS if s < 4)
GLA_STEP_ROWS = 8

VMEM_BYTES_V7X = 64 * 1024 * 1024
VMEM_LIMIT = VMEM_BYTES_V7X - 2 * 1024 * 1024


def _params(n_axes):
    return pltpu.CompilerParams(dimension_semantics=("arbitrary",) * n_axes,
                                vmem_limit_bytes=VMEM_LIMIT)


def _dot(a, b):
    return jnp.dot(a, b, preferred_element_type=F32)


def _dot_nt(a, b):
    return lax.dot_general(a, b, (((1,), (1,)), ((), ())), preferred_element_type=F32)


def _split3(x):
    hi = x.astype(BF16)
    r1 = x - hi.astype(F32)
    mid = r1.astype(BF16)
    lo = (r1 - mid.astype(F32)).astype(BF16)
    return hi, mid, lo


def _exact_dot01(m01, x):
    x1, x2, x3 = _split3(x)
    return _dot(m01, x1) + _dot(m01, x2) + _dot(m01, x3)


def _sigmoid(x):
    return 1.0 / (1.0 + jnp.exp(-x))


def _log_sigmoid(x):
    return jnp.minimum(x, 0.0) - jnp.log1p(jnp.exp(-jnp.abs(x)))


def _rms(x, g):
    var = jnp.mean(x * x, axis=-1, keepdims=True)
    return x * lax.rsqrt(var + EPS) * g


def _delayed_rows(u, prev):
    sub = lax.broadcasted_iota(jnp.int32, prev.shape, 0)
    r1 = pltpu.roll(u, 1, axis=0)
    r2 = pltpu.roll(u, 2, axis=0)
    head1 = jnp.where(sub < 1, pltpu.roll(prev, 1, axis=0), r1[0:8, :])
    head2 = jnp.where(sub < 2, pltpu.roll(prev, 2, axis=0), r2[0:8, :])
    return (jnp.concatenate([head1, r1[8:, :]], axis=0), jnp.concatenate([head2, r2[8:, :]], axis=0))


def _prefetched_rows(x_hbm, xbuf_ref, sem_ref, tile, step, n_tiles):
    rows = xbuf_ref.shape[0]

    def copy(t):
        return pltpu.make_async_copy(x_hbm.at[pl.ds(t * rows, rows), :], xbuf_ref, sem_ref.at[0])

    @pl.when(jnp.logical_and(tile == 0, step == 0))
    def _():
        copy(0).start()

    @pl.when(step == 0)
    def _():
        copy(tile).wait()

    @pl.when(jnp.logical_and(step == 1, tile + 1 < n_tiles))
    def _():
        copy(tile + 1).start()


def _stacked_operand(in_specs, args, stacked, out_index):
    if stacked is None:
        return {}
    in_specs.append(pl.BlockSpec(memory_space=pl.ANY))
    args.append(stacked)
    return {len(args) - 1: out_index}


def _conv_tap_step(new_row, st_ref, w_ref, new_st_ref):
    p1 = st_ref[:, 1, :]
    cu = st_ref[:, 0, :] * w_ref[0:1, :]
    cu = cu + p1 * w_ref[1:2, :]
    cu = cu + new_row * w_ref[2:3, :]
    new_st_ref[:, 0, :] = p1
    new_st_ref[:, 1, :] = new_row
    return cu


def _mixer_in_kernel(x_hbm, xs_ref, sconv_ref, g_ref, wa_ref, wb_ref, wc_ref, wgt_ref, cw_ref, *rest, n_conv):
    (ya_ref, yas_ref, o_ref, os_ref, og_ref, ogs_ref, st_ref, sconv_new_ref,
     xn_ref, xbuf_ref, sem_ref) = rest[-11:]
    j = pl.program_id(1)
    tm = xbuf_ref.shape[0]
    ns = xs_ref.shape[0]
    tn = wa_ref.shape[0]
    _prefetched_rows(x_hbm, xbuf_ref, sem_ref, pl.program_id(0), j, pl.num_programs(0))

    @pl.when(j == 0)
    def _():
        xn_ref[0:tm, :] = _rms(xbuf_ref[...], g_ref[...]).astype(BF16)
        xn_ref[tm:tm + ns, :] = _rms(xs_ref[...], g_ref[...]).astype(BF16)
        gate = _dot_nt(xn_ref[...], wgt_ref[...].astype(BF16))
        og_ref[...] = gate[0:tm, :]
        ogs_ref[...] = gate[tm:tm + ns, :]

    def projected():
        xn = xn_ref[...]
        return [_dot_nt(xn, w_ref[...].astype(BF16)) for w_ref in (wa_ref, wb_ref, wc_ref)]

    @pl.when(j < n_conv)
    def _():
        bg, cg, hin = projected()
        z = cg[0:tm, :] * hin[0:tm, :]
        z1, z2 = _delayed_rows(z, jnp.zeros((8, tn), F32))
        cu = z2 * cw_ref[0:1, :]
        cu = cu + z1 * cw_ref[1:2, :]
        cu = cu + z * cw_ref[2:3, :]
        ya_ref[...] = (bg[0:tm, :] * cu).astype(BF16)
        st_ref[...] = z[tm - (CONV_WIDTH - 1):tm, :]
        cu_s = _conv_tap_step(cg[tm:tm + ns, :] * hin[tm:tm + ns, :], sconv_ref, cw_ref, sconv_new_ref)
        yas_ref[...] = (bg[tm:tm + ns, :] * cu_s).astype(BF16)

    @pl.when(j >= n_conv)
    def _():
        for idx, tile in enumerate(projected()):
            cols = slice(idx * tn, (idx + 1) * tn)
            o_ref[:, cols] = tile[0:tm, :].astype(BF16)
            os_ref[:, cols] = tile[tm:tm + ns, :]


def _mixer_in(x, xs, state_conv, g, w_in_t, conv_w, layer, seq, tn, stacked):
    m, nb = x.shape[0], xs.shape[0]
    batch = m // seq
    ns = nb // batch
    n_conv = CONV_DIM // tn
    qkvg = MAIN_COLS - 3 * CONV_DIM
    n_rest = qkvg // (3 * tn)
    last_conv = n_conv - 1

    def w_spec(group):
        def index(i, j):
            conv_tile = group * n_conv + j
            rest_tile = 3 * n_conv + 3 * (j - n_conv) + group
            return (layer, jnp.where(j < n_conv, conv_tile, rest_tile), 0)
        return pl.BlockSpec((None, tn, D_MODEL), index)

    conv_col = lambda i, j: jnp.minimum(j, last_conv)
    rest_col = lambda i, j: jnp.maximum(j - n_conv, 0)
    x_spec = pl.BlockSpec(memory_space=pl.ANY)
    xs_spec = pl.BlockSpec((ns, D_MODEL), lambda i, j: (i, 0))
    sconv_spec = pl.BlockSpec((None, ns, CONV_WIDTH - 1, tn), lambda i, j: (layer, i, 0, conv_col(i, j)))
    g_spec = pl.BlockSpec((None, 1, D_MODEL), lambda i, j: (layer, 0, 0))
    wgt_spec = pl.BlockSpec((None, GATE_RANK, D_MODEL), lambda i, j: (layer, MAIN_COLS // GATE_RANK, 0))
    cw_spec = pl.BlockSpec((None, CONV_WIDTH, tn), lambda i, j: (layer, 0, conv_col(i, j)))
    in_specs = [x_spec, xs_spec, sconv_spec, g_spec, w_spec(0), w_spec(1), w_spec(2), wgt_spec, cw_spec]
    args = [x, xs, state_conv, g, w_in_t, w_in_t, w_in_t, w_in_t, conv_w]
    aliases = _stacked_operand(in_specs, args, stacked, 7)
    out_specs = [pl.BlockSpec((seq, tn), lambda i, j: (i, conv_col(i, j))),
                 pl.BlockSpec((ns, tn), lambda i, j: (i, conv_col(i, j))),
                 pl.BlockSpec((seq, 3 * tn), lambda i, j: (i, rest_col(i, j))),
                 pl.BlockSpec((ns, 3 * tn), lambda i, j: (i, rest_col(i, j))),
                 pl.BlockSpec((seq, GATE_RANK), lambda i, j: (i, 0)),
                 pl.BlockSpec((ns, GATE_RANK), lambda i, j: (i, 0)),
                 pl.BlockSpec((None, CONV_WIDTH - 1, tn), lambda i, j: (i, 0, conv_col(i, j))),
                 sconv_spec]
    out_shape = [jax.ShapeDtypeStruct((m, CONV_DIM), BF16), jax.ShapeDtypeStruct((nb, CONV_DIM), BF16),
                 jax.ShapeDtypeStruct((m, qkvg), BF16), jax.ShapeDtypeStruct((nb, qkvg), F32),
                 jax.ShapeDtypeStruct((m, GATE_RANK), F32), jax.ShapeDtypeStruct((nb, GATE_RANK), F32),
                 jax.ShapeDtypeStruct((batch, CONV_WIDTH - 1, CONV_DIM), F32),
                 jax.ShapeDtypeStruct(state_conv.shape, F32)]
    return pl.pallas_call(
        functools.partial(_mixer_in_kernel, n_conv=n_conv), grid=(batch, n_conv + n_rest),
        in_specs=in_specs, out_specs=out_specs, out_shape=out_shape,
        scratch_shapes=[pltpu.VMEM((seq + ns, D_MODEL), BF16), pltpu.VMEM((seq, D_MODEL), F32),
                        pltpu.SemaphoreType.DMA((1,))],
        input_output_aliases=aliases, compiler_params=_params(2), name="mixer_in")(*args)


def _out_proj_kernel(ya_ref, yb_ref, wa_ref, wb_ref, x_ref, o_ref):
    acc = _dot(ya_ref[...], wa_ref[...].astype(BF16))
    acc = acc + _dot(yb_ref[...], wb_ref[...].astype(BF16))
    o_ref[...] = x_ref[...] + acc


def _out_proj(ya, yb, w_out, x, layer, tm):
    m = x.shape[0]
    a_spec = pl.BlockSpec((tm, CONV_DIM), lambda i: (i, 0))
    wa_spec = pl.BlockSpec((None, CONV_DIM, D_MODEL), lambda i: (layer, 0, 0), pipeline_mode=pl.Buffered(1))
    wb_spec = pl.BlockSpec((None, GLA_V_DIM, D_MODEL), lambda i: (layer, 1, 0), pipeline_mode=pl.Buffered(1))
    x_spec = pl.BlockSpec((tm, D_MODEL), lambda i: (i, 0))
    return pl.pallas_call(
        _out_proj_kernel, grid=(m // tm,), in_specs=[a_spec, a_spec, wa_spec, wb_spec, x_spec],
        out_specs=x_spec, out_shape=jax.ShapeDtypeStruct((m, D_MODEL), F32),
        compiler_params=_params(1), name="out_proj")(ya, yb, w_out, w_out, x)


def _ffn_kernel(x_hbm, xs_ref, sst_ref, g_ref, wu_ref, wv_ref, cw_ref, cb_ref, wd_ref, fg_ref, *rest,
                tiles_per_seq, final_norm):
    o_ref, os_ref, st_ref, sst_new_ref, xn_ref, prev_ref, carry_ref, xbuf_ref, sem_ref = rest[-9:]
    i = pl.program_id(0)
    f = pl.program_id(1)
    tm = xbuf_ref.shape[0]
    ns = xs_ref.shape[0]
    _prefetched_rows(x_hbm, xbuf_ref, sem_ref, i, f, pl.num_programs(0))

    @pl.when(f == 0)
    def _():
        x = xbuf_ref[...]
        xs = xs_ref[...]
        xn_ref[0:tm, :] = _rms(x, g_ref[...]).astype(BF16)
        xn_ref[tm:tm + ns, :] = _rms(xs, g_ref[...]).astype(BF16)
        o_ref[...] = x
        os_ref[...] = xs

    first = lax.rem(i, tiles_per_seq) == 0

    @pl.when(first)
    def _():
        prev_ref[...] = jnp.zeros(prev_ref.shape, F32)

    @pl.when(jnp.logical_not(first))
    def _():
        prev_ref[...] = carry_ref[f]

    xn = xn_ref[...]
    u_all = _dot(xn, wu_ref[...].astype(BF16))
    v_all = _dot(xn, wv_ref[...].astype(BF16))
    u = u_all[0:tm, :]
    u1, u2 = _delayed_rows(u, prev_ref[...])
    cu = u2 * cw_ref[0:1, :]
    cu = cu + u1 * cw_ref[1:2, :]
    cu = cu + u * cw_ref[2:3, :]
    carry_ref[f] = u[tm - 8:tm, :]
    st_ref[...] = u[tm - (CONV_WIDTH - 1):tm, :]
    cu_s = _conv_tap_step(u_all[tm:tm + ns, :], sst_ref, cw_ref, sst_new_ref)

    pre = jnp.concatenate([cu, cu_s], axis=0) + cb_ref[...]
    h = (pre * _sigmoid(pre) * v_all).astype(BF16)
    acc = _dot(h, wd_ref[...].astype(BF16))
    o_ref[...] += acc[0:tm, :]
    os_ref[...] += acc[tm:tm + ns, :]

    if final_norm:
        @pl.when(f == pl.num_programs(1) - 1)
        def _():
            o_ref[...] = _rms(o_ref[...], fg_ref[...])
            os_ref[...] = _rms(os_ref[...], fg_ref[...])


def _ffn(x, xs, state_s, g, w_up, conv_w, conv_b, w_down, final_g, layer, seq, tm, tf, final_norm, stacked):
    m = x.shape[0]
    n_tiles = m // tm
    ns = xs.shape[0] // n_tiles
    tiles_per_seq = seq // tm
    nf = D_FF // tf
    row_spec = pl.BlockSpec((tm, D_MODEL), lambda i, f: (i, 0))
    srow_spec = pl.BlockSpec((ns, D_MODEL), lambda i, f: (i, 0))
    sst_spec = pl.BlockSpec((None, ns, CONV_WIDTH - 1, tf), lambda i, f: (layer, i, 0, f))
    g_spec = pl.BlockSpec((None, 1, D_MODEL), lambda i, f: (layer, 0, 0))
    wu_spec = pl.BlockSpec((None, D_MODEL, tf), lambda i, f: (layer, 0, f))
    wv_spec = pl.BlockSpec((None, D_MODEL, tf), lambda i, f: (layer, 0, nf + f))
    cw_spec = pl.BlockSpec((None, CONV_WIDTH, tf), lambda i, f: (layer, 0, f))
    cb_spec = pl.BlockSpec((None, 1, tf), lambda i, f: (layer, 0, f))
    wd_spec = pl.BlockSpec((None, tf, D_MODEL), lambda i, f: (layer, f, 0))
    fg_spec = pl.BlockSpec((1, D_MODEL), lambda i, f: (0, 0))
    st_spec = pl.BlockSpec((None, CONV_WIDTH - 1, tf), lambda i, f: (i, 0, f))
    x_spec = pl.BlockSpec(memory_space=pl.ANY)
    in_specs = [x_spec, srow_spec, sst_spec, g_spec, wu_spec, wv_spec, cw_spec, cb_spec, wd_spec, fg_spec]
    args = [x, xs, state_s, g, w_up, w_up, conv_w, conv_b, w_down, final_g]
    aliases = _stacked_operand(in_specs, args, stacked, 3)
    body = functools.partial(_ffn_kernel, tiles_per_seq=tiles_per_seq, final_norm=final_norm)
    out, out_s, tails, state_new = pl.pallas_call(
        body, grid=(n_tiles, nf), in_specs=in_specs,
        out_specs=[row_spec, srow_spec, st_spec, sst_spec],
        out_shape=[jax.ShapeDtypeStruct((m, D_MODEL), F32),
                   jax.ShapeDtypeStruct(xs.shape, F32),
                   jax.ShapeDtypeStruct((n_tiles, CONV_WIDTH - 1, D_FF), F32),
                   jax.ShapeDtypeStruct(state_s.shape, F32)],
        scratch_shapes=[pltpu.VMEM((tm + ns, D_MODEL), BF16), pltpu.VMEM((8, tf), F32),
                        pltpu.VMEM((nf, 8, tf), F32), pltpu.VMEM((tm, D_MODEL), F32),
                        pltpu.SemaphoreType.DMA((1,))],
        input_output_aliases=aliases, compiler_params=_params(2), name="ffn")(*args)
    return out, out_s, tails[tiles_per_seq - 1::tiles_per_seq], state_new


def _ref_rows(s, chunk):
    rows = np.arange(chunk)
    return (rows // (2 * s)) * (2 * s) + s - 1


def _gla_constants(chunk):
    rows = np.arange(chunk)
    tri = (rows[:, None] >= rows[None, :]).astype(np.float32)
    sels = [(_ref_rows(s, chunk)[:, None] == rows[None, :]).astype(np.float32) for s in GLA_FINE_LEVELS]
    sgns = [np.where(rows > _ref_rows(s, chunk), 1.0, -1.0) * np.log2(np.e) for s in GLA_LEVELS]
    sgn = np.broadcast_to(np.concatenate(sgns)[:, None], (len(GLA_LEVELS) * chunk, GLA_DK))
    return (jnp.asarray(tri, BF16), jnp.asarray(np.concatenate(sels, axis=0), BF16),
            jnp.asarray(sgn, F32))


def _gla_log_decay(glr, w2, gb):
    gate = _dot(glr.astype(BF16), w2.astype(BF16)) + gb
    return _log_sigmoid(gate) * (1.0 / GATE_TAU)


def _gla_decay_kernel(glr_ref, w2_ref, gb_ref, tri_ref, b_ref):
    chunk = tri_ref.shape[0]
    g = _gla_log_decay(glr_ref[...], w2_ref[...], gb_ref[...])
    tri = tri_ref[...]
    for c in range(glr_ref.shape[0] // chunk):
        rows = slice(c * chunk, (c + 1) * chunk)
        b_ref[rows, :] = _exact_dot01(tri, g[rows, :])


def _gla_decay(glr, gate_w2, gate_b, tri, layer, tg):
    m = glr.shape[0]
    return pl.pallas_call(
        _gla_decay_kernel, grid=(m // tg,),
        in_specs=[pl.BlockSpec((tg, GATE_RANK), lambda i: (i, 0)),
                  pl.BlockSpec((None, GATE_RANK, GLA_QK_DIM), lambda i: (layer, 0, 0)),
                  pl.BlockSpec((None, 1, GLA_QK_DIM), lambda i: (layer, 0, 0)),
                  pl.BlockSpec(tri.shape, lambda i: (0, 0))],
        out_specs=pl.BlockSpec((tg, GLA_QK_DIM), lambda i: (i, 0)),
        out_shape=jax.ShapeDtypeStruct((m, GLA_QK_DIM), F32),
        compiler_params=_params(1), name="gla_decay")(glr, gate_w2, gate_b, tri)


def _gla_out(o, gn, og):
    var = jnp.mean(o * o, axis=-1, keepdims=True)
    return o * lax.rsqrt(var + EPS) * gn * (og * _sigmoid(og))


def _gla_prompt_kernel(q_ref, k_ref, v_ref, og_ref, b_ref, bh0_ref, bh1_ref, bh2_ref, bh3_ref,
                       gn_ref, sel_ref, sgn_ref, yb_ref, sfin_ref, s_ref):
    bh_refs = (bh0_ref, bh1_ref, bh2_ref, bh3_ref)
    c_idx = pl.program_id(1)
    chunk = q_ref.shape[0]

    @pl.when(c_idx == 0)
    def _():
        s_ref[...] = jnp.zeros(s_ref.shape, F32)

    b_all = b_ref[...]
    beta_fine = _exact_dot01(sel_ref[...], b_all)

    row = lax.broadcasted_iota(jnp.int32, (chunk, chunk), 0)
    col = lax.broadcasted_iota(jnp.int32, (chunk, chunk), 1)
    diff_bits = row ^ col

    for h in range(GLA_HEADS):
        ks = slice(h * GLA_DK, (h + 1) * GLA_DK)
        vs = slice(h * GLA_DV, (h + 1) * GLA_DV)
        b = b_all[:, ks]
        b_last = b_ref[chunk - 1:chunk, ks]
        q = q_ref[:, ks]
        k = k_ref[:, ks]
        v = v_ref[:, vs]
        s_prev = s_ref[h]

        inter = _dot(q * jnp.exp(b).astype(BF16), s_prev.astype(BF16))

        acc = _dot_nt(q, k)
        for li, s in enumerate(GLA_LEVELS):
            if s in GLA_FINE_LEVELS:
                fi = GLA_FINE_LEVELS.index(s)
                beta = beta_fine[fi * chunk:(fi + 1) * chunk, ks]
            else:
                beta = jnp.concatenate(
                    [bh_refs[h][pl.ds(int(r), 8, stride=0), :]
                     for r in _ref_rows(s, chunk)[::2 * s] for _ in range(2 * s // 8)], axis=0)
            e = jnp.exp2((b - beta) * sgn_ref[li * chunk:(li + 1) * chunk, :]).astype(BF16)
            p = _dot_nt(q * e, k * e)
            acc = jnp.where(diff_bits >= s, p, acc)
        a = jnp.where(row >= col, acc, 0.0)
        o = (inter + _dot(a.astype(BF16), v)) * (GLA_DK ** -0.5)

        kdec_t = (k.astype(F32) * jnp.exp(b_last - b)).T
        decay_t = jnp.broadcast_to(jnp.exp(b_last), (GLA_DK, GLA_DK)).T
        s_new = jnp.concatenate([decay_t, decay_t], axis=1) * s_prev + _dot(kdec_t.astype(BF16), v)
        s_ref[h] = s_new

        yb_ref[:, vs] = _gla_out(o, gn_ref[:, vs], og_ref[:, vs].astype(F32)).astype(BF16)

    @pl.when(c_idx == pl.num_programs(1) - 1)
    def _():
        sfin_ref[...] = s_ref[...]


def _gla_prompt(qkvg, glr, gate_w2, gate_b, gla_norm_g, layer, batch, seq):
    chunk = GLA_CHUNK
    nc = seq // chunk
    tri, sel, sgn = _gla_constants(chunk)
    b_all = _gla_decay(glr, gate_w2, gate_b, tri, layer, 4 * chunk)

    def col_spec(width, blk):
        return pl.BlockSpec((chunk, width), lambda b, c: (b * nc + c, blk))

    gn_spec = pl.BlockSpec((None, 1, GLA_V_DIM), lambda b, c: (layer, 0, 0))
    sel_spec = pl.BlockSpec(sel.shape, lambda b, c: (0, 0))
    sgn_spec = pl.BlockSpec(sgn.shape, lambda b, c: (0, 0))
    yb_spec = pl.BlockSpec((chunk, GLA_V_DIM), lambda b, c: (b * nc + c, 0))
    s_spec = pl.BlockSpec((None, GLA_HEADS, GLA_DK, GLA_DV), lambda b, c: (b, 0, 0, 0))
    return pl.pallas_call(
        _gla_prompt_kernel, grid=(batch, nc),
        in_specs=[col_spec(GLA_QK_DIM, 0), col_spec(GLA_QK_DIM, 1), col_spec(GLA_V_DIM, 1),
                  col_spec(GLA_V_DIM, 2), col_spec(GLA_QK_DIM, 0)]
                 + [col_spec(GLA_DK, h) for h in range(GLA_HEADS)] + [gn_spec, sel_spec, sgn_spec],
        out_specs=[yb_spec, s_spec],
        out_shape=[jax.ShapeDtypeStruct((batch * seq, GLA_V_DIM), BF16),
                   jax.ShapeDtypeStruct((batch, GLA_HEADS, GLA_DK, GLA_DV), F32)],
        scratch_shapes=[pltpu.VMEM((GLA_HEADS, GLA_DK, GLA_DV), F32)],
        compiler_params=_params(2), name="gla_prompt")(
            qkvg, qkvg, qkvg, qkvg, b_all, *([b_all] * GLA_HEADS), gla_norm_g, sel, sgn)


def _columns(slab):
    pad = jnp.zeros((GLA_DK - GLA_STEP_ROWS, GLA_DK), F32)
    return jnp.concatenate([slab, pad], axis=0).T


def _gla_step_kernel(q_ref, k_ref, v_ref, og_ref, glr_ref, w2_ref, gb_ref, gn_ref, s_ref, *rest):
    yb_ref, snew_ref = rest[-2:]
    rows = GLA_STEP_ROWS
    decay_all = jnp.exp(_gla_log_decay(glr_ref[...], w2_ref[...], gb_ref[...]))
    row_id = lax.broadcasted_iota(jnp.int32, (rows, rows * GLA_DV), 0)
    lane_blk = lax.broadcasted_iota(jnp.int32, (rows, rows * GLA_DV), 1) // GLA_DV

    for h in range(GLA_HEADS):
        ks = slice(h * GLA_DK, (h + 1) * GLA_DK)
        vs = slice(h * GLA_DV, (h + 1) * GLA_DV)
        decay = decay_all[:, ks]
        q = q_ref[:, ks] * (GLA_DK ** -0.5)
        k = k_ref[:, ks]
        v = v_ref[:, vs]
        qk = jnp.sum(q * k, axis=-1, keepdims=True)

        v_diag = jnp.where(row_id == lane_blk, jnp.concatenate([v] * rows, axis=1), 0.0)
        v_diag = jnp.concatenate([v_diag, jnp.zeros_like(v_diag)], axis=0)
        k_t = _columns(k)[:, 0:2 * rows]
        kv = _dot(k_t.astype(BF16), v_diag.astype(BF16))

        decay_c = _columns(decay)
        qd = (q * decay).astype(BF16)
        inter_rows = []
        for i in range(rows):
            s_prev = s_ref[i, h]
            inter_rows.append(_dot(qd, s_prev.astype(BF16))[i:i + 1, :])
            snew_ref[i, h] = decay_c[:, i:i + 1] * s_prev + kv[:, i * GLA_DV:(i + 1) * GLA_DV]
        o = jnp.concatenate(inter_rows, axis=0) + qk * v
        yb_ref[:, vs] = _gla_out(o, gn_ref[:, vs], og_ref[:, vs]).astype(BF16)


def _gla_step(qkvg, glr, state, gate_w2, gate_b, gla_norm_g, layer, stacked):
    nb = qkvg.shape[0]
    rows = GLA_STEP_ROWS
    s_spec = pl.BlockSpec((None, rows, GLA_HEADS, GLA_DK, GLA_DV), lambda t: (layer, t, 0, 0, 0))
    in_specs = [pl.BlockSpec((rows, GLA_QK_DIM), lambda t: (t, 0)),
                pl.BlockSpec((rows, GLA_QK_DIM), lambda t: (t, 1)),
                pl.BlockSpec((rows, GLA_V_DIM), lambda t: (t, 1)),
                pl.BlockSpec((rows, GLA_V_DIM), lambda t: (t, 2)),
                pl.BlockSpec((rows, GATE_RANK), lambda t: (t, 0)),
                pl.BlockSpec((None, GATE_RANK, GLA_QK_DIM), lambda t: (layer, 0, 0)),
                pl.BlockSpec((None, 1, GLA_QK_DIM), lambda t: (layer, 0, 0)),
                pl.BlockSpec((None, 1, GLA_V_DIM), lambda t: (layer, 0, 0)),
                s_spec]
    args = [qkvg, qkvg, qkvg, qkvg, glr, gate_w2, gate_b, gla_norm_g, state]
    aliases = _stacked_operand(in_specs, args, stacked, 1)
    return pl.pallas_call(
        _gla_step_kernel, grid=(nb // rows,), in_specs=in_specs,
        out_specs=[pl.BlockSpec((rows, GLA_V_DIM), lambda t: (t, 0)), s_spec],
        out_shape=[jax.ShapeDtypeStruct((nb, GLA_V_DIM), BF16),
                   jax.ShapeDtypeStruct(state.shape, F32)],
        input_output_aliases=aliases,
        compiler_params=_params(1), name="gla_step")(*args)


def kernel(x_prompt, x_sample, state_conv, state_gla, state_ffn_conv, norm_mix_g, w_in, conv_w, gate_w2, gate_b, gla_norm_g, w_out, norm_ffn_g, w_up, ffn_conv_w, ffn_conv_b, w_down, final_norm_g):
    batch, seq, _ = x_prompt.shape
    nb = x_sample.shape[0]
    norm_mix_g, norm_ffn_g = norm_mix_g[:, None, :], norm_ffn_g[:, None, :]
    gate_b, gla_norm_g, ffn_conv_b = gate_b[:, None, :], gla_norm_g[:, None, :], ffn_conv_b[:, None, :]
    final_norm_g = final_norm_g[None, :]
    w_in_t = jnp.swapaxes(w_in, 1, 2)

    xp = x_prompt.reshape(batch * seq, D_MODEL)
    xs = x_sample.reshape(nb, D_MODEL)
    conv_p, gla_p, ffn_p = [], [], []
    conv_s, gla_s, ffn_s = None, None, None
    for l in range(DEPTH):
        ya, ya_s, qkvg, qkvg_s, glr, glr_s, c_new, conv_s = _mixer_in(
            xp, xs, state_conv, norm_mix_g, w_in_t, conv_w, l, seq, 256, conv_s)

        yb, s_new = _gla_prompt(qkvg, glr, gate_w2, gate_b, gla_norm_g, l, batch, seq)
        xp = _out_proj(ya, yb, w_out, xp, l, 512)

        yb_s, gla_s = _gla_step(qkvg_s, glr_s, state_gla, gate_w2, gate_b, gla_norm_g, l, gla_s)
        xs = _out_proj(ya_s, yb_s, w_out, xs, l, nb)

        xp, xs, f_new, ffn_s = _ffn(xp, xs, state_ffn_conv, norm_ffn_g, w_up, ffn_conv_w, ffn_conv_b, w_down,
                                    final_norm_g, l, seq, 1024, 512, l == DEPTH - 1, ffn_s)
        conv_p.append(c_new)
        gla_p.append(s_new)
        ffn_p.append(f_new)

    return (xp.reshape(batch, seq, D_MODEL), xs.reshape(nb, 1, D_MODEL),
            jnp.stack(conv_p), jnp.stack(gla_p), jnp.stack(ffn_p), conv_s, gla_s, ffn_s)
```

```python
import functools

import numpy as np
import jax
import jax.numpy as jnp
from jax import lax
from jax.experimental import pallas as pl
from jax.experimental.pallas import tpu as pltpu

F32 = jnp.float32
BF16 = jnp.bfloat16

D_MODEL = 2048
DEPTH = 2
CONV_DIM = 1024
CONV_WIDTH = 3
GLA_HEADS = 4
GLA_QK_DIM = 512
GLA_V_DIM = 1024
GLA_DK = 128
GLA_DV = 256
GATE_RANK = 16
GATE_TAU = 16.0
D_FF = 5632
EPS = 1e-6
MAIN_COLS = 3 * CONV_DIM + 2 * GLA_QK_DIM + 2 * GLA_V_DIM

GLA_CHUNK = 128
GLA_LEVELS = tuple(1 << i for i in range(GLA_CHUNK.bit_length() - 1))
GLA_FINE_LEVELS = tuple(s for s in GLA_LEVELS if s < 4)
GLA_STEP_ROWS = 8

VMEM_BYTES_V7X = 64 * 1024 * 1024
VMEM_LIMIT = VMEM_BYTES_V7X - 2 * 1024 * 1024


def _params(n_axes):
    return pltpu.CompilerParams(dimension_semantics=("arbitrary",) * n_axes,
                                vmem_limit_bytes=VMEM_LIMIT)


def _dot(a, b):
    return jnp.dot(a, b, preferred_element_type=F32)


def _dot_nt(a, b):
    return lax.dot_general(a, b, (((1,), (1,)), ((), ())), preferred_element_type=F32)


def _split3(x):
    hi = x.astype(BF16)
    r1 = x - hi.astype(F32)
    mid = r1.astype(BF16)
    lo = (r1 - mid.astype(F32)).astype(BF16)
    return hi, mid, lo


def _exact_dot01(m01, x):
    x1, x2, x3 = _split3(x)
    return _dot(m01, x1) + _dot(m01, x2) + _dot(m01, x3)


def _sigmoid(x):
    return 1.0 / (1.0 + jnp.exp(-x))


def _log_sigmoid(x):
    return jnp.minimum(x, 0.0) - jnp.log1p(jnp.exp(-jnp.abs(x)))


def _rms(x, g):
    var = jnp.mean(x * x, axis=-1, keepdims=True)
    return x * lax.rsqrt(var + EPS) * g


def _delayed_rows(u, prev):
    sub = lax.broadcasted_iota(jnp.int32, prev.shape, 0)
    r1 = pltpu.roll(u, 1, axis=0)
    r2 = pltpu.roll(u, 2, axis=0)
    head1 = jnp.where(sub < 1, pltpu.roll(prev, 1, axis=0), r1[0:8, :])
    head2 = jnp.where(sub < 2, pltpu.roll(prev, 2, axis=0), r2[0:8, :])
    return (jnp.concatenate([head1, r1[8:, :]], axis=0), jnp.concatenate([head2, r2[8:, :]], axis=0))


def _prefetched_rows(x_hbm, xbuf_ref, sem_ref, tile, step, n_tiles):
    rows = xbuf_ref.shape[0]

    def copy(t):
        return pltpu.make_async_copy(x_hbm.at[pl.ds(t * rows, rows), :], xbuf_ref, sem_ref.at[0])

    @pl.when(jnp.logical_and(tile == 0, step == 0))
    def _():
        copy(0).start()

    @pl.when(step == 0)
    def _():
        copy(tile).wait()

    @pl.when(jnp.logical_and(step == 1, tile + 1 < n_tiles))
    def _():
        copy(tile + 1).start()


def _stacked_operand(in_specs, args, stacked, out_index):
    if stacked is None:
        return {}
    in_specs.append(pl.BlockSpec(memory_space=pl.ANY))
    args.append(stacked)
    return {len(args) - 1: out_index}


def _conv_tap_step(new_row, st_ref, w_ref, new_st_ref):
    p1 = st_ref[:, 1, :]
    cu = st_ref[:, 0, :] * w_ref[0:1, :]
    cu = cu + p1 * w_ref[1:2, :]
    cu = cu + new_row * w_ref[2:3, :]
    new_st_ref[:, 0, :] = p1
    new_st_ref[:, 1, :] = new_row
    return cu


def _mixer_in_kernel(x_hbm, xs_ref, sconv_ref, g_ref, wa_ref, wb_ref, wc_ref, wgt_ref, cw_ref, *rest, n_conv):
    (ya_ref, yas_ref, o_ref, os_ref, og_ref, ogs_ref, st_ref, sconv_new_ref,
     xn_ref, xbuf_ref, sem_ref) = rest[-11:]
    j = pl.program_id(1)
    tm = xbuf_ref.shape[0]
    ns = xs_ref.shape[0]
    tn = wa_ref.shape[0]
    _prefetched_rows(x_hbm, xbuf_ref, sem_ref, pl.program_id(0), j, pl.num_programs(0))

    @pl.when(j == 0)
    def _():
        xn_ref[0:tm, :] = _rms(xbuf_ref[...], g_ref[...]).astype(BF16)
        xn_ref[tm:tm + ns, :] = _rms(xs_ref[...], g_ref[...]).astype(BF16)
        gate = _dot_nt(xn_ref[...], wgt_ref[...].astype(BF16))
        og_ref[...] = gate[0:tm, :]
        ogs_ref[...] = gate[tm:tm + ns, :]

    def projected():
        xn = xn_ref[...]
        return [_dot_nt(xn, w_ref[...].astype(BF16)) for w_ref in (wa_ref, wb_ref, wc_ref)]

    @pl.when(j < n_conv)
    def _():
        bg, cg, hin = projected()
        z = cg[0:tm, :] * hin[0:tm, :]
        z1, z2 = _delayed_rows(z, jnp.zeros((8, tn), F32))
        cu = z2 * cw_ref[0:1, :]
        cu = cu + z1 * cw_ref[1:2, :]
        cu = cu + z * cw_ref[2:3, :]
        ya_ref[...] = (bg[0:tm, :] * cu).astype(BF16)
        st_ref[...] = z[tm - (CONV_WIDTH - 1):tm, :]
        cu_s = _conv_tap_step(cg[tm:tm + ns, :] * hin[tm:tm + ns, :], sconv_ref, cw_ref, sconv_new_ref)
        yas_ref[...] = (bg[tm:tm + ns, :] * cu_s).astype(BF16)

    @pl.when(j >= n_conv)
    def _():
        for idx, tile in enumerate(projected()):
            cols = slice(idx * tn, (idx + 1) * tn)
            o_ref[:, cols] = tile[0:tm, :].astype(BF16)
            os_ref[:, cols] = tile[tm:tm + ns, :]


def _mixer_in(x, xs, state_conv, g, w_in_t, conv_w, layer, seq, tn, stacked):
    m, nb = x.shape[0], xs.shape[0]
    batch = m // seq
    ns = nb // batch
    n_conv = CONV_DIM // tn
    qkvg = MAIN_COLS - 3 * CONV_DIM
    n_rest = qkvg // (3 * tn)
    last_conv = n_conv - 1

    def w_spec(group):
        def index(i, j):
            conv_tile = group * n_conv + j
            rest_tile = 3 * n_conv + 3 * (j - n_conv) + group
            return (layer, jnp.where(j < n_conv, conv_tile, rest_tile), 0)
        return pl.BlockSpec((None, tn, D_MODEL), index)

    conv_col = lambda i, j: jnp.minimum(j, last_conv)
    rest_col = lambda i, j: jnp.maximum(j - n_conv, 0)
    x_spec = pl.BlockSpec(memory_space=pl.ANY)
    xs_spec = pl.BlockSpec((ns, D_MODEL), lambda i, j: (i, 0))
    sconv_spec = pl.BlockSpec((None, ns, CONV_WIDTH - 1, tn), lambda i, j: (layer, i, 0, conv_col(i, j)))
    g_spec = pl.BlockSpec((None, 1, D_MODEL), lambda i, j: (layer, 0, 0))
    wgt_spec = pl.BlockSpec((None, GATE_RANK, D_MODEL), lambda i, j: (layer, MAIN_COLS // GATE_RANK, 0))
    cw_spec = pl.BlockSpec((None, CONV_WIDTH, tn), lambda i, j: (layer, 0, conv_col(i, j)))
    in_specs = [x_spec, xs_spec, sconv_spec, g_spec, w_spec(0), w_spec(1), w_spec(2), wgt_spec, cw_spec]
    args = [x, xs, state_conv, g, w_in_t, w_in_t, w_in_t, w_in_t, conv_w]
    aliases = _stacked_operand(in_specs, args, stacked, 7)
    out_specs = [pl.BlockSpec((seq, tn), lambda i, j: (i, conv_col(i, j))),
                 pl.BlockSpec((ns, tn), lambda i, j: (i, conv_col(i, j))),
                 pl.BlockSpec((seq, 3 * tn), lambda i, j: (i, rest_col(i, j))),
                 pl.BlockSpec((ns, 3 * tn), lambda i, j: (i, rest_col(i, j))),
                 pl.BlockSpec((seq, GATE_RANK), lambda i, j: (i, 0)),
                 pl.BlockSpec((ns, GATE_RANK), lambda i, j: (i, 0)),
                 pl.BlockSpec((None, CONV_WIDTH - 1, tn), lambda i, j: (i, 0, conv_col(i, j))),
                 sconv_spec]
    out_shape = [jax.ShapeDtypeStruct((m, CONV_DIM), BF16), jax.ShapeDtypeStruct((nb, CONV_DIM), BF16),
                 jax.ShapeDtypeStruct((m, qkvg), BF16), jax.ShapeDtypeStruct((nb, qkvg), F32),
                 jax.ShapeDtypeStruct((m, GATE_RANK), F32), jax.ShapeDtypeStruct((nb, GATE_RANK), F32),
                 jax.ShapeDtypeStruct((batch, CONV_WIDTH - 1, CONV_DIM), F32),
                 jax.ShapeDtypeStruct(state_conv.shape, F32)]
    return pl.pallas_call(
        functools.partial(_mixer_in_kernel, n_conv=n_conv), grid=(batch, n_conv + n_rest),
        in_specs=in_specs, out_specs=out_specs, out_shape=out_shape,
        scratch_shapes=[pltpu.VMEM((seq + ns, D_MODEL), BF16), pltpu.VMEM((seq, D_MODEL), F32),
                        pltpu.SemaphoreType.DMA((1,))],
        input_output_aliases=aliases, compiler_params=_params(2), name="mixer_in")(*args)


def _out_proj_kernel(ya_ref, yb_ref, wa_ref, wb_ref, x_ref, o_ref):
    acc = _dot(ya_ref[...], wa_ref[...].astype(BF16))
    acc = acc + _dot(yb_ref[...], wb_ref[...].astype(BF16))
    o_ref[...] = x_ref[...] + acc


def _out_proj(ya, yb, w_out, x, layer, tm):
    m = x.shape[0]
    a_spec = pl.BlockSpec((tm, CONV_DIM), lambda i: (i, 0))
    wa_spec = pl.BlockSpec((None, CONV_DIM, D_MODEL), lambda i: (layer, 0, 0), pipeline_mode=pl.Buffered(1))
    wb_spec = pl.BlockSpec((None, GLA_V_DIM, D_MODEL), lambda i: (layer, 1, 0), pipeline_mode=pl.Buffered(1))
    x_spec = pl.BlockSpec((tm, D_MODEL), lambda i: (i, 0))
    return pl.pallas_call(
        _out_proj_kernel, grid=(m // tm,), in_specs=[a_spec, a_spec, wa_spec, wb_spec, x_spec],
        out_specs=x_spec, out_shape=jax.ShapeDtypeStruct((m, D_MODEL), F32),
        compiler_params=_params(1), name="out_proj")(ya, yb, w_out, w_out, x)


def _ffn_kernel(x_hbm, xs_ref, sst_ref, g_ref, wu_ref, wv_ref, cw_ref, cb_ref, wd_ref, fg_ref, *rest,
                tiles_per_seq, final_norm):
    o_ref, os_ref, st_ref, sst_new_ref, xn_ref, prev_ref, carry_ref, xbuf_ref, sem_ref = rest[-9:]
    i = pl.program_id(0)
    f = pl.program_id(1)
    tm = xbuf_ref.shape[0]
    ns = xs_ref.shape[0]
    _prefetched_rows(x_hbm, xbuf_ref, sem_ref, i, f, pl.num_programs(0))

    @pl.when(f == 0)
    def _():
        x = xbuf_ref[...]
        xs = xs_ref[...]
        xn_ref[0:tm, :] = _rms(x, g_ref[...]).astype(BF16)
        xn_ref[tm:tm + ns, :] = _rms(xs, g_ref[...]).astype(BF16)
        o_ref[...] = x
        os_ref[...] = xs

    first = lax.rem(i, tiles_per_seq) == 0

    @pl.when(first)
    def _():
        prev_ref[...] = jnp.zeros(prev_ref.shape, F32)

    @pl.when(jnp.logical_not(first))
    def _():
        prev_ref[...] = carry_ref[f]

    xn = xn_ref[...]
    u_all = _dot(xn, wu_ref[...].astype(BF16))
    v_all = _dot(xn, wv_ref[...].astype(BF16))
    u = u_all[0:tm, :]
    u1, u2 = _delayed_rows(u, prev_ref[...])
    cu = u2 * cw_ref[0:1, :]
    cu = cu + u1 * cw_ref[1:2, :]
    cu = cu + u * cw_ref[2:3, :]
    carry_ref[f] = u[tm - 8:tm, :]
    st_ref[...] = u[tm - (CONV_WIDTH - 1):tm, :]
    cu_s = _conv_tap_step(u_all[tm:tm + ns, :], sst_ref, cw_ref, sst_new_ref)

    pre = jnp.concatenate([cu, cu_s], axis=0) + cb_ref[...]
    h = (pre * _sigmoid(pre) * v_all).astype(BF16)
    acc = _dot(h, wd_ref[...].astype(BF16))
    o_ref[...] += acc[0:tm, :]
    os_ref[...] += acc[tm:tm + ns, :]

    if final_norm:
        @pl.when(f == pl.num_programs(1) - 1)
        def _():
            o_ref[...] = _rms(o_ref[...], fg_ref[...])
            os_ref[...] = _rms(os_ref[...], fg_ref[...])


def _ffn(x, xs, state_s, g, w_up, conv_w, conv_b, w_down, final_g, layer, seq, tm, tf, final_norm, stacked):
    m = x.shape[0]
    n_tiles = m // tm
    ns = xs.shape[0] // n_tiles
    tiles_per_seq = seq // tm
    nf = D_FF // tf
    row_spec = pl.BlockSpec((tm, D_MODEL), lambda i, f: (i, 0))
    srow_spec = pl.BlockSpec((ns, D_MODEL), lambda i, f: (i, 0))
    sst_spec = pl.BlockSpec((None, ns, CONV_WIDTH - 1, tf), lambda i, f: (layer, i, 0, f))
    g_spec = pl.BlockSpec((None, 1, D_MODEL), lambda i, f: (layer, 0, 0))
    wu_spec = pl.BlockSpec((None, D_MODEL, tf), lambda i, f: (layer, 0, f))
    wv_spec = pl.BlockSpec((None, D_MODEL, tf), lambda i, f: (layer, 0, nf + f))
    cw_spec = pl.BlockSpec((None, CONV_WIDTH, tf), lambda i, f: (layer, 0, f))
    cb_spec = pl.BlockSpec((None, 1, tf), lambda i, f: (layer, 0, f))
    wd_spec = pl.BlockSpec((None, tf, D_MODEL), lambda i, f: (layer, f, 0))
    fg_spec = pl.BlockSpec((1, D_MODEL), lambda i, f: (0, 0))
    st_spec = pl.BlockSpec((None, CONV_WIDTH - 1, tf), lambda i, f: (i, 0, f))
    x_spec = pl.BlockSpec(memory_space=pl.ANY)
    in_specs = [x_spec, srow_spec, sst_spec, g_spec, wu_spec, wv_spec, cw_spec, cb_spec, wd_spec, fg_spec]
    args = [x, xs, state_s, g, w_up, w_up, conv_w, conv_b, w_down, final_g]
    aliases = _stacked_operand(in_specs, args, stacked, 3)
    body = functools.partial(_ffn_kernel, tiles_per_seq=tiles_per_seq, final_norm=final_norm)
    out, out_s, tails, state_new = pl.pallas_call(
        body, grid=(n_tiles, nf), in_specs=in_specs,
        out_specs=[row_spec, srow_spec, st_spec, sst_spec],
        out_shape=[jax.ShapeDtypeStruct((m, D_MODEL), F32),
                   jax.ShapeDtypeStruct(xs.shape, F32),
                   jax.ShapeDtypeStruct((n_tiles, CONV_WIDTH - 1, D_FF), F32),
                   jax.ShapeDtypeStruct(state_s.shape, F32)],
        scratch_shapes=[pltpu.VMEM((tm + ns, D_MODEL), BF16), pltpu.VMEM((8, tf), F32),
                        pltpu.VMEM((nf, 8, tf), F32), pltpu.VMEM((tm, D_MODEL), F32),
                        pltpu.SemaphoreType.DMA((1,))],
        input_output_aliases=aliases, compiler_params=_params(2), name="ffn")(*args)
    return out, out_s, tails[tiles_per_seq - 1::tiles_per_seq], state_new


def _ref_rows(s, chunk):
    rows = np.arange(chunk)
    return (rows // (2 * s)) * (2 * s) + s - 1


def _gla_constants(chunk):
    rows = np.arange(chunk)
    tri = (rows[:, None] >= rows[None, :]).astype(np.float32)
    cums = [tri] + [tri[_ref_rows(s, chunk)] for s in GLA_FINE_LEVELS]
    sgns = [np.where(rows > _ref_rows(s, chunk), 1.0, -1.0) * np.log2(np.e) for s in GLA_LEVELS]
    sgn = np.broadcast_to(np.concatenate(sgns)[:, None], (len(GLA_LEVELS) * chunk, GLA_DK))
    return jnp.asarray(np.concatenate(cums, axis=0), BF16), jnp.asarray(sgn, F32)


def _gla_log_decay(glr, w2, gb):
    gate = _dot(glr.astype(BF16), w2.astype(BF16)) + gb
    return _log_sigmoid(gate) * (1.0 / GATE_TAU)


def _gla_decay_kernel(glr_ref, w2_ref, gb_ref, cum_ref, b_ref):
    chunk = GLA_CHUNK
    g = _gla_log_decay(glr_ref[...], w2_ref[...], gb_ref[...])
    cum = cum_ref[...]
    for c in range(glr_ref.shape[0] // chunk):
        rows = slice(c * chunk, (c + 1) * chunk)
        cums = _exact_dot01(cum, g[rows, :])
        for part in range(1 + len(GLA_FINE_LEVELS)):
            b_ref[rows, part * GLA_QK_DIM:(part + 1) * GLA_QK_DIM] = cums[part * chunk:(part + 1) * chunk, :]


def _gla_decay(glr, gate_w2, gate_b, cum, layer, tg):
    m = glr.shape[0]
    width = (1 + len(GLA_FINE_LEVELS)) * GLA_QK_DIM
    return pl.pallas_call(
        _gla_decay_kernel, grid=(m // tg,),
        in_specs=[pl.BlockSpec((tg, GATE_RANK), lambda i: (i, 0)),
                  pl.BlockSpec((None, GATE_RANK, GLA_QK_DIM), lambda i: (layer, 0, 0)),
                  pl.BlockSpec((None, 1, GLA_QK_DIM), lambda i: (layer, 0, 0)),
                  pl.BlockSpec(cum.shape, lambda i: (0, 0))],
        out_specs=pl.BlockSpec((tg, width), lambda i: (i, 0)),
        out_shape=jax.ShapeDtypeStruct((m, width), F32),
        compiler_params=_params(1), name="gla_decay")(glr, gate_w2, gate_b, cum)


def _gla_out(o, gn, og):
    var = jnp.mean(o * o, axis=-1, keepdims=True)
    return o * lax.rsqrt(var + EPS) * gn * (og * _sigmoid(og))


def _gla_prompt_kernel(q_ref, k_ref, v_ref, og_ref, b_ref, bf0_ref, bf1_ref, bh0_ref, bh1_ref, bh2_ref, bh3_ref,
                       gn_ref, sgn_ref, yb_ref, sfin_ref, s_ref):
    bf_refs = (bf0_ref, bf1_ref)
    bh_refs = (bh0_ref, bh1_ref, bh2_ref, bh3_ref)
    c_idx = pl.program_id(1)
    chunk = q_ref.shape[0]

    @pl.when(c_idx == 0)
    def _():
        s_ref[...] = jnp.zeros(s_ref.shape, F32)

    b_all = b_ref[...]

    row = lax.broadcasted_iota(jnp.int32, (chunk, chunk), 0)
    col = lax.broadcasted_iota(jnp.int32, (chunk, chunk), 1)
    diff_bits = row ^ col

    for h in range(GLA_HEADS):
        ks = slice(h * GLA_DK, (h + 1) * GLA_DK)
        vs = slice(h * GLA_DV, (h + 1) * GLA_DV)
        b = b_all[:, ks]
        b_last = b_ref[chunk - 1:chunk, ks]
        q = q_ref[:, ks]
        k = k_ref[:, ks]
        v = v_ref[:, vs]
        s_prev = s_ref[h]

        inter = _dot(q * jnp.exp(b).astype(BF16), s_prev.astype(BF16))

        acc = _dot_nt(q, k)
        for li, s in enumerate(GLA_LEVELS):
            if s in GLA_FINE_LEVELS:
                beta = bf_refs[GLA_FINE_LEVELS.index(s)][:, ks]
            else:
                beta = jnp.concatenate(
                    [bh_refs[h][pl.ds(int(r), 8, stride=0), :]
                     for r in _ref_rows(s, chunk)[::2 * s] for _ in range(2 * s // 8)], axis=0)
            e = jnp.exp2((b - beta) * sgn_ref[li * chunk:(li + 1) * chunk, :]).astype(BF16)
            p = _dot_nt(q * e, k * e)
            acc = jnp.where(diff_bits >= s, p, acc)
        a = jnp.where(row >= col, acc, 0.0)
        o = (inter + _dot(a.astype(BF16), v)) * (GLA_DK ** -0.5)

        kdec_t = (k.astype(F32) * jnp.exp(b_last - b)).T
        decay_t = jnp.broadcast_to(jnp.exp(b_last), (GLA_DK, GLA_DK)).T
        s_new = jnp.concatenate([decay_t, decay_t], axis=1) * s_prev + _dot(kdec_t.astype(BF16), v)
        s_ref[h] = s_new

        yb_ref[:, vs] = _gla_out(o, gn_ref[:, vs], og_ref[:, vs].astype(F32)).astype(BF16)

    @pl.when(c_idx == pl.num_programs(1) - 1)
    def _():
        sfin_ref[...] = s_ref[...]


def _gla_prompt(qkvg, glr, gate_w2, gate_b, gla_norm_g, layer, batch, seq):
    chunk = GLA_CHUNK
    nc = seq // chunk
    cum, sgn = _gla_constants(chunk)
    b_all = _gla_decay(glr, gate_w2, gate_b, cum, layer, 8 * chunk)

    def col_spec(width, blk):
        return pl.BlockSpec((chunk, width), lambda b, c: (b * nc + c, blk))

    gn_spec = pl.BlockSpec((None, 1, GLA_V_DIM), lambda b, c: (layer, 0, 0))
    sgn_spec = pl.BlockSpec(sgn.shape, lambda b, c: (0, 0))
    yb_spec = pl.BlockSpec((chunk, GLA_V_DIM), lambda b, c: (b * nc + c, 0))
    s_spec = pl.BlockSpec((None, GLA_HEADS, GLA_DK, GLA_DV), lambda b, c: (b, 0, 0, 0))
    return pl.pallas_call(
        _gla_prompt_kernel, grid=(batch, nc),
        in_specs=[col_spec(GLA_QK_DIM, 0), col_spec(GLA_QK_DIM, 1), col_spec(GLA_V_DIM, 1),
                  col_spec(GLA_V_DIM, 2)]
                 + [col_spec(GLA_QK_DIM, part) for part in range(1 + len(GLA_FINE_LEVELS))]
                 + [col_spec(GLA_DK, h) for h in range(GLA_HEADS)] + [gn_spec, sgn_spec],
        out_specs=[yb_spec, s_spec],
        out_shape=[jax.ShapeDtypeStruct((batch * seq, GLA_V_DIM), BF16),
                   jax.ShapeDtypeStruct((batch, GLA_HEADS, GLA_DK, GLA_DV), F32)],
        scratch_shapes=[pltpu.VMEM((GLA_HEADS, GLA_DK, GLA_DV), F32)],
        compiler_params=_params(2), name="gla_prompt")(
            qkvg, qkvg, qkvg, qkvg, *([b_all] * (1 + len(GLA_FINE_LEVELS) + GLA_HEADS)), gla_norm_g, sgn)


def _columns(slab):
    pad = jnp.zeros((GLA_DK - GLA_STEP_ROWS, GLA_DK), F32)
    return jnp.concatenate([slab, pad], axis=0).T


def _gla_step_kernel(q_ref, k_ref, v_ref, og_ref, glr_ref, w2_ref, gb_ref, gn_ref, s_ref, *rest):
    yb_ref, snew_ref = rest[-2:]
    rows = GLA_STEP_ROWS
    decay_all = jnp.exp(_gla_log_decay(glr_ref[...], w2_ref[...], gb_ref[...]))
    row_id = lax.broadcasted_iota(jnp.int32, (rows, rows * GLA_DV), 0)
    lane_blk = lax.broadcasted_iota(jnp.int32, (rows, rows * GLA_DV), 1) // GLA_DV

    for h in range(GLA_HEADS):
        ks = slice(h * GLA_DK, (h + 1) * GLA_DK)
        vs = slice(h * GLA_DV, (h + 1) * GLA_DV)
        decay = decay_all[:, ks]
        q = q_ref[:, ks] * (GLA_DK ** -0.5)
        k = k_ref[:, ks]
        v = v_ref[:, vs]
        qk = jnp.sum(q * k, axis=-1, keepdims=True)

        v_diag = jnp.where(row_id == lane_blk, jnp.concatenate([v] * rows, axis=1), 0.0)
        v_diag = jnp.concatenate([v_diag, jnp.zeros_like(v_diag)], axis=0)
        k_t = _columns(k)[:, 0:2 * rows]
        kv = _dot(k_t.astype(BF16), v_diag.astype(BF16))

        decay_c = _columns(decay)
        qd = (q * decay).astype(BF16)
        inter_rows = []
        for i in range(rows):
            s_prev = s_ref[i, h]
            inter_rows.append(_dot(qd, s_prev.astype(BF16))[i:i + 1, :])
            snew_ref[i, h] = decay_c[:, i:i + 1] * s_prev + kv[:, i * GLA_DV:(i + 1) * GLA_DV]
        o = jnp.concatenate(inter_rows, axis=0) + qk * v
        yb_ref[:, vs] = _gla_out(o, gn_ref[:, vs], og_ref[:, vs]).astype(BF16)


def _gla_step(qkvg, glr, state, gate_w2, gate_b, gla_norm_g, layer, stacked):
    nb = qkvg.shape[0]
    rows = GLA_STEP_ROWS
    s_spec = pl.BlockSpec((None, rows, GLA_HEADS, GLA_DK, GLA_DV), lambda t: (layer, t, 0, 0, 0))
    in_specs = [pl.BlockSpec((rows, GLA_QK_DIM), lambda t: (t, 0)),
                pl.BlockSpec((rows, GLA_QK_DIM), lambda t: (t, 1)),
                pl.BlockSpec((rows, GLA_V_DIM), lambda t: (t, 1)),
                pl.BlockSpec((rows, GLA_V_DIM), lambda t: (t, 2)),
                pl.BlockSpec((rows, GATE_RANK), lambda t: (t, 0)),
                pl.BlockSpec((None, GATE_RANK, GLA_QK_DIM), lambda t: (layer, 0, 0)),
                pl.BlockSpec((None, 1, GLA_QK_DIM), lambda t: (layer, 0, 0)),
                pl.BlockSpec((None, 1, GLA_V_DIM), lambda t: (layer, 0, 0)),
                s_spec]
    args = [qkvg, qkvg, qkvg, qkvg, glr, gate_w2, gate_b, gla_norm_g, state]
    aliases = _stacked_operand(in_specs, args, stacked, 1)
    return pl.pallas_call(
        _gla_step_kernel, grid=(nb // rows,), in_specs=in_specs,
        out_specs=[pl.BlockSpec((rows, GLA_V_DIM), lambda t: (t, 0)), s_spec],
        out_shape=[jax.ShapeDtypeStruct((nb, GLA_V_DIM), BF16),
                   jax.ShapeDtypeStruct(state.shape, F32)],
        input_output_aliases=aliases,
        compiler_params=_params(1), name="gla_step")(*args)


def kernel(x_prompt, x_sample, state_conv, state_gla, state_ffn_conv, norm_mix_g, w_in, conv_w, gate_w2, gate_b, gla_norm_g, w_out, norm_ffn_g, w_up, ffn_conv_w, ffn_conv_b, w_down, final_norm_g):
    batch, seq, _ = x_prompt.shape
    nb = x_sample.shape[0]
    norm_mix_g, norm_ffn_g = norm_mix_g[:, None, :], norm_ffn_g[:, None, :]
    gate_b, gla_norm_g, ffn_conv_b = gate_b[:, None, :], gla_norm_g[:, None, :], ffn_conv_b[:, None, :]
    final_norm_g = final_norm_g[None, :]
    w_in_t = jnp.swapaxes(w_in, 1, 2)

    xp = x_prompt.reshape(batch * seq, D_MODEL)
    xs = x_sample.reshape(nb, D_MODEL)
    conv_p, gla_p, ffn_p = [], [], []
    conv_s, gla_s, ffn_s = None, None, None
    for l in range(DEPTH):
        ya, ya_s, qkvg, qkvg_s, glr, glr_s, c_new, conv_s = _mixer_in(
            xp, xs, state_conv, norm_mix_g, w_in_t, conv_w, l, seq, 256, conv_s)

        yb, s_new = _gla_prompt(qkvg, glr, gate_w2, gate_b, gla_norm_g, l, batch, seq)
        xp = _out_proj(ya, yb, w_out, xp, l, 512)

        yb_s, gla_s = _gla_step(qkvg_s, glr_s, state_gla, gate_w2, gate_b, gla_norm_g, l, gla_s)
        xs = _out_proj(ya_s, yb_s, w_out, xs, l, nb)

        xp, xs, f_new, ffn_s = _ffn(xp, xs, state_ffn_conv, norm_ffn_g, w_up, ffn_conv_w, ffn_conv_b, w_down,
                                    final_norm_g, l, seq, 1024, 512, l == DEPTH - 1, ffn_s)
        conv_p.append(c_new)
        gla_p.append(s_new)
        ffn_p.append(f_new)

    return (xp.reshape(batch, seq, D_MODEL), xs.reshape(nb, 1, D_MODEL),
            jnp.stack(conv_p), jnp.stack(gla_p), jnp.stack(ffn_p), conv_s, gla_s, ffn_s)
```

```python
import functools

import numpy as np
import jax
import jax.numpy as jnp
from jax import lax
from jax.experimental import pallas as pl
from jax.experimental.pallas import tpu as pltpu

F32 = jnp.float32
BF16 = jnp.bfloat16

D_MODEL = 2048
DEPTH = 2
CONV_DIM = 1024
CONV_WIDTH = 3
GLA_HEADS = 4
GLA_QK_DIM = 512
GLA_V_DIM = 1024
GLA_DK = 128
GLA_DV = 256
GATE_RANK = 16
GATE_TAU = 16.0
D_FF = 5632
EPS = 1e-6
MAIN_COLS = 3 * CONV_DIM + 2 * GLA_QK_DIM + 2 * GLA_V_DIM

GLA_CHUNK = 128
GLA_LEVELS = tuple(1 << i for i in range(GLA_CHUNK.bit_length() - 1))
GLA_FINE_LEVELS = tuple(s for s in GLA_LEVELS if s < 4)
GLA_STEP_ROWS = 8

VMEM_BYTES_V7X = 64 * 1024 * 1024
VMEM_LIMIT = VMEM_BYTES_V7X - 2 * 1024 * 1024


def _params(n_axes):
    return pltpu.CompilerParams(dimension_semantics=("arbitrary",) * n_axes,
                                vmem_limit_bytes=VMEM_LIMIT)


def _dot(a, b):
    return jnp.dot(a, b, preferred_element_type=F32)


def _dot_nt(a, b):
    return lax.dot_general(a, b, (((1,), (1,)), ((), ())), preferred_element_type=F32)


def _split3(x):
    hi = x.astype(BF16)
    r1 = x - hi.astype(F32)
    mid = r1.astype(BF16)
    lo = (r1 - mid.astype(F32)).astype(BF16)
    return hi, mid, lo


def _exact_dot01(m01, x):
    x1, x2, x3 = _split3(x)
    return _dot(m01, x1) + _dot(m01, x2) + _dot(m01, x3)


def _sigmoid(x):
    return 1.0 / (1.0 + jnp.exp(-x))


def _log_sigmoid(x):
    return jnp.minimum(x, 0.0) - jnp.log1p(jnp.exp(-jnp.abs(x)))


def _rms(x, g):
    var = jnp.mean(x * x, axis=-1, keepdims=True)
    return x * lax.rsqrt(var + EPS) * g


def _delayed_rows(u, prev):
    sub = lax.broadcasted_iota(jnp.int32, prev.shape, 0)
    r1 = pltpu.roll(u, 1, axis=0)
    r2 = pltpu.roll(u, 2, axis=0)
    head1 = jnp.where(sub < 1, pltpu.roll(prev, 1, axis=0), r1[0:8, :])
    head2 = jnp.where(sub < 2, pltpu.roll(prev, 2, axis=0), r2[0:8, :])
    return (jnp.concatenate([head1, r1[8:, :]], axis=0), jnp.concatenate([head2, r2[8:, :]], axis=0))


def _prefetched_rows(x_hbm, xbuf_ref, sem_ref, tile, step, n_tiles):
    rows = xbuf_ref.shape[0]

    def copy(t):
        return pltpu.make_async_copy(x_hbm.at[pl.ds(t * rows, rows), :], xbuf_ref, sem_ref.at[0])

    @pl.when(jnp.logical_and(tile == 0, step == 0))
    def _():
        copy(0).start()

    @pl.when(step == 0)
    def _():
        copy(tile).wait()

    @pl.when(jnp.logical_and(step == 1, tile + 1 < n_tiles))
    def _():
        copy(tile + 1).start()


def _stacked_operand(in_specs, args, stacked, out_index):
    if stacked is None:
        return {}
    in_specs.append(pl.BlockSpec(memory_space=pl.ANY))
    args.append(stacked)
    return {len(args) - 1: out_index}


def _conv_tap_step(new_row, st_ref, w_ref, new_st_ref):
    p1 = st_ref[:, 1, :]
    cu = st_ref[:, 0, :] * w_ref[0:1, :]
    cu = cu + p1 * w_ref[1:2, :]
    cu = cu + new_row * w_ref[2:3, :]
    new_st_ref[:, 0, :] = p1
    new_st_ref[:, 1, :] = new_row
    return cu


def _mixer_in_kernel(x_hbm, xs_ref, sconv_ref, g_ref, wa_ref, wb_ref, wc_ref, wgt_ref, cw_ref, *rest, n_conv):
    (ya_ref, yas_ref, o_ref, os_ref, og_ref, ogs_ref, st_ref, sconv_new_ref,
     xn_ref, xbuf_ref, sem_ref) = rest[-11:]
    j = pl.program_id(1)
    tm = xbuf_ref.shape[0]
    ns = xs_ref.shape[0]
    tn = wa_ref.shape[0]
    _prefetched_rows(x_hbm, xbuf_ref, sem_ref, pl.program_id(0), j, pl.num_programs(0))

    @pl.when(j == 0)
    def _():
        xn_ref[0:tm, :] = _rms(xbuf_ref[...], g_ref[...]).astype(BF16)
        xn_ref[tm:tm + ns, :] = _rms(xs_ref[...], g_ref[...]).astype(BF16)
        gate = _dot_nt(xn_ref[...], wgt_ref[...].astype(BF16))
        og_ref[...] = gate[0:tm, :]
        ogs_ref[...] = gate[tm:tm + ns, :]

    def projected():
        xn = xn_ref[...]
        return [_dot_nt(xn, w_ref[...].astype(BF16)) for w_ref in (wa_ref, wb_ref, wc_ref)]

    @pl.when(j < n_conv)
    def _():
        bg, cg, hin = projected()
        z = cg[0:tm, :] * hin[0:tm, :]
        z1, z2 = _delayed_rows(z, jnp.zeros((8, tn), F32))
        cu = z2 * cw_ref[0:1, :]
        cu = cu + z1 * cw_ref[1:2, :]
        cu = cu + z * cw_ref[2:3, :]
        ya_ref[...] = (bg[0:tm, :] * cu).astype(BF16)
        st_ref[...] = z[tm - (CONV_WIDTH - 1):tm, :]
        cu_s = _conv_tap_step(cg[tm:tm + ns, :] * hin[tm:tm + ns, :], sconv_ref, cw_ref, sconv_new_ref)
        yas_ref[...] = (bg[tm:tm + ns, :] * cu_s).astype(BF16)

    @pl.when(j >= n_conv)
    def _():
        for idx, tile in enumerate(projected()):
            cols = slice(idx * tn, (idx + 1) * tn)
            o_ref[:, cols] = tile[0:tm, :].astype(BF16)
            os_ref[:, cols] = tile[tm:tm + ns, :]


def _mixer_in(x, xs, state_conv, g, w_in_t, conv_w, layer, seq, tn, stacked):
    m, nb = x.shape[0], xs.shape[0]
    batch = m // seq
    ns = nb // batch
    n_conv = CONV_DIM // tn
    qkvg = MAIN_COLS - 3 * CONV_DIM
    n_rest = qkvg // (3 * tn)
    last_conv = n_conv - 1

    def w_spec(group):
        def index(i, j):
            conv_tile = group * n_conv + j
            rest_tile = 3 * n_conv + 3 * (j - n_conv) + group
            return (layer, jnp.where(j < n_conv, conv_tile, rest_tile), 0)
        return pl.BlockSpec((None, tn, D_MODEL), index)

    conv_col = lambda i, j: jnp.minimum(j, last_conv)
    rest_col = lambda i, j: jnp.maximum(j - n_conv, 0)
    x_spec = pl.BlockSpec(memory_space=pl.ANY)
    xs_spec = pl.BlockSpec((ns, D_MODEL), lambda i, j: (i, 0))
    sconv_spec = pl.BlockSpec((None, ns, CONV_WIDTH - 1, tn), lambda i, j: (layer, i, 0, conv_col(i, j)))
    g_spec = pl.BlockSpec((None, 1, D_MODEL), lambda i, j: (layer, 0, 0))
    wgt_spec = pl.BlockSpec((None, GATE_RANK, D_MODEL), lambda i, j: (layer, MAIN_COLS // GATE_RANK, 0))
    cw_spec = pl.BlockSpec((None, CONV_WIDTH, tn), lambda i, j: (layer, 0, conv_col(i, j)))
    in_specs = [x_spec, xs_spec, sconv_spec, g_spec, w_spec(0), w_spec(1), w_spec(2), wgt_spec, cw_spec]
    args = [x, xs, state_conv, g, w_in_t, w_in_t, w_in_t, w_in_t, conv_w]
    aliases = _stacked_operand(in_specs, args, stacked, 7)
    out_specs = [pl.BlockSpec((seq, tn), lambda i, j: (i, conv_col(i, j))),
                 pl.BlockSpec((ns, tn), lambda i, j: (i, conv_col(i, j))),
                 pl.BlockSpec((seq, 3 * tn), lambda i, j: (i, rest_col(i, j))),
                 pl.BlockSpec((ns, 3 * tn), lambda i, j: (i, rest_col(i, j))),
                 pl.BlockSpec((seq, GATE_RANK), lambda i, j: (i, 0)),
                 pl.BlockSpec((ns, GATE_RANK), lambda i, j: (i, 0)),
                 pl.BlockSpec((None, CONV_WIDTH - 1, tn), lambda i, j: (i, 0, conv_col(i, j))),
                 sconv_spec]
    out_shape = [jax.ShapeDtypeStruct((m, CONV_DIM), BF16), jax.ShapeDtypeStruct((nb, CONV_DIM), BF16),
                 jax.ShapeDtypeStruct((m, qkvg), BF16), jax.ShapeDtypeStruct((nb, qkvg), F32),
                 jax.ShapeDtypeStruct((m, GATE_RANK), F32), jax.ShapeDtypeStruct((nb, GATE_RANK), F32),
                 jax.ShapeDtypeStruct((batch, CONV_WIDTH - 1, CONV_DIM), F32),
                 jax.ShapeDtypeStruct(state_conv.shape, F32)]
    return pl.pallas_call(
        functools.partial(_mixer_in_kernel, n_conv=n_conv), grid=(batch, n_conv + n_rest),
        in_specs=in_specs, out_specs=out_specs, out_shape=out_shape,
        scratch_shapes=[pltpu.VMEM((seq + ns, D_MODEL), BF16), pltpu.VMEM((seq, D_MODEL), F32),
                        pltpu.SemaphoreType.DMA((1,))],
        input_output_aliases=aliases, compiler_params=_params(2), name="mixer_in")(*args)


def _out_proj_kernel(ya_ref, yb_ref, wa_ref, wb_ref, x_ref, o_ref):
    acc = _dot(ya_ref[...], wa_ref[...].astype(BF16))
    acc = acc + _dot(yb_ref[...], wb_ref[...].astype(BF16))
    o_ref[...] = x_ref[...] + acc


def _out_proj(ya, yb, w_out, x, layer, tm):
    m = x.shape[0]
    a_spec = pl.BlockSpec((tm, CONV_DIM), lambda i: (i, 0))
    wa_spec = pl.BlockSpec((None, CONV_DIM, D_MODEL), lambda i: (layer, 0, 0), pipeline_mode=pl.Buffered(1))
    wb_spec = pl.BlockSpec((None, GLA_V_DIM, D_MODEL), lambda i: (layer, 1, 0), pipeline_mode=pl.Buffered(1))
    x_spec = pl.BlockSpec((tm, D_MODEL), lambda i: (i, 0))
    return pl.pallas_call(
        _out_proj_kernel, grid=(m // tm,), in_specs=[a_spec, a_spec, wa_spec, wb_spec, x_spec],
        out_specs=x_spec, out_shape=jax.ShapeDtypeStruct((m, D_MODEL), F32),
        compiler_params=_params(1), name="out_proj")(ya, yb, w_out, w_out, x)


def _ffn_kernel(x_hbm, xs_ref, sst_ref, g_ref, wu_ref, wv_ref, cw_ref, cb_ref, wd_ref, fg_ref, *rest,
                tiles_per_seq, final_norm):
    o_ref, os_ref, st_ref, sst_new_ref, xn_ref, prev_ref, carry_ref, xbuf_ref, sem_ref = rest[-9:]
    i = pl.program_id(0)
    f = pl.program_id(1)
    tm = xbuf_ref.shape[0]
    ns = xs_ref.shape[0]
    _prefetched_rows(x_hbm, xbuf_ref, sem_ref, i, f, pl.num_programs(0))

    @pl.when(f == 0)
    def _():
        x = xbuf_ref[...]
        xs = xs_ref[...]
        xn_ref[0:tm, :] = _rms(x, g_ref[...]).astype(BF16)
        xn_ref[tm:tm + ns, :] = _rms(xs, g_ref[...]).astype(BF16)
        o_ref[...] = x
        os_ref[...] = xs

    first = lax.rem(i, tiles_per_seq) == 0

    @pl.when(first)
    def _():
        prev_ref[...] = jnp.zeros(prev_ref.shape, F32)

    @pl.when(jnp.logical_not(first))
    def _():
        prev_ref[...] = carry_ref[f]

    xn = xn_ref[...]
    u_all = _dot(xn, wu_ref[...].astype(BF16))
    v_all = _dot(xn, wv_ref[...].astype(BF16))
    u = u_all[0:tm, :]
    u1, u2 = _delayed_rows(u, prev_ref[...])
    cu = u2 * cw_ref[0:1, :]
    cu = cu + u1 * cw_ref[1:2, :]
    cu = cu + u * cw_ref[2:3, :]
    carry_ref[f] = u[tm - 8:tm, :]
    st_ref[...] = u[tm - (CONV_WIDTH - 1):tm, :]
    cu_s = _conv_tap_step(u_all[tm:tm + ns, :], sst_ref, cw_ref, sst_new_ref)

    pre = jnp.concatenate([cu, cu_s], axis=0) + cb_ref[...]
    h = (pre * _sigmoid(pre) * v_all).astype(BF16)
    acc = _dot(h, wd_ref[...].astype(BF16))
    o_ref[...] += acc[0:tm, :]
    os_ref[...] += acc[tm:tm + ns, :]

    if final_norm:
        @pl.when(f == pl.num_programs(1) - 1)
        def _():
            o_ref[...] = _rms(o_ref[...], fg_ref[...])
            os_ref[...] = _rms(os_ref[...], fg_ref[...])


def _ffn(x, xs, state_s, g, w_up, conv_w, conv_b, w_down, final_g, layer, seq, tm, tf, final_norm, stacked):
    m = x.shape[0]
    n_tiles = m // tm
    ns = xs.shape[0] // n_tiles
    tiles_per_seq = seq // tm
    nf = D_FF // tf
    row_spec = pl.BlockSpec((tm, D_MODEL), lambda i, f: (i, 0))
    srow_spec = pl.BlockSpec((ns, D_MODEL), lambda i, f: (i, 0))
    sst_spec = pl.BlockSpec((None, ns, CONV_WIDTH - 1, tf), lambda i, f: (layer, i, 0, f))
    g_spec = pl.BlockSpec((None, 1, D_MODEL), lambda i, f: (layer, 0, 0))
    wu_spec = pl.BlockSpec((None, D_MODEL, tf), lambda i, f: (layer, 0, f))
    wv_spec = pl.BlockSpec((None, D_MODEL, tf), lambda i, f: (layer, 0, nf + f))
    cw_spec = pl.BlockSpec((None, CONV_WIDTH, tf), lambda i, f: (layer, 0, f))
    cb_spec = pl.BlockSpec((None, 1, tf), lambda i, f: (layer, 0, f))
    wd_spec = pl.BlockSpec((None, tf, D_MODEL), lambda i, f: (layer, f, 0))
    fg_spec = pl.BlockSpec((1, D_MODEL), lambda i, f: (0, 0))
    st_spec = pl.BlockSpec((None, CONV_WIDTH - 1, tf), lambda i, f: (i, 0, f))
    x_spec = pl.BlockSpec(memory_space=pl.ANY)
    in_specs = [x_spec, srow_spec, sst_spec, g_spec, wu_spec, wv_spec, cw_spec, cb_spec, wd_spec, fg_spec]
    args = [x, xs, state_s, g, w_up, w_up, conv_w, conv_b, w_down, final_g]
    aliases = _stacked_operand(in_specs, args, stacked, 3)
    body = functools.partial(_ffn_kernel, tiles_per_seq=tiles_per_seq, final_norm=final_norm)
    out, out_s, tails, state_new = pl.pallas_call(
        body, grid=(n_tiles, nf), in_specs=in_specs,
        out_specs=[row_spec, srow_spec, st_spec, sst_spec],
        out_shape=[jax.ShapeDtypeStruct((m, D_MODEL), F32),
                   jax.ShapeDtypeStruct(xs.shape, F32),
                   jax.ShapeDtypeStruct((n_tiles, CONV_WIDTH - 1, D_FF), F32),
                   jax.ShapeDtypeStruct(state_s.shape, F32)],
        scratch_shapes=[pltpu.VMEM((tm + ns, D_MODEL), BF16), pltpu.VMEM((8, tf), F32),
                        pltpu.VMEM((nf, 8, tf), F32), pltpu.VMEM((tm, D_MODEL), F32),
                        pltpu.SemaphoreType.DMA((1,))],
        input_output_aliases=aliases, compiler_params=_params(2), name="ffn")(*args)
    return out, out_s, tails[tiles_per_seq - 1::tiles_per_seq], state_new


def _ref_rows(s, chunk):
    rows = np.arange(chunk)
    return (rows // (2 * s)) * (2 * s) + s - 1


def _gla_constants(chunk):
    rows = np.arange(chunk)
    tri = (rows[:, None] >= rows[None, :]).astype(np.float32)
    cums = [tri] + [tri[_ref_rows(s, chunk)] for s in GLA_FINE_LEVELS]
    sgns = [np.where(rows > _ref_rows(s, chunk), 1.0, -1.0) * np.log2(np.e) for s in GLA_LEVELS]
    sgn = np.broadcast_to(np.concatenate(sgns)[:, None], (len(GLA_LEVELS) * chunk, GLA_DK))
    return jnp.asarray(np.concatenate(cums, axis=0), BF16), jnp.asarray(sgn, F32)


def _gla_log_decay(glr, w2, gb):
    gate = _dot(glr.astype(BF16), w2.astype(BF16)) + gb
    return _log_sigmoid(gate) * (1.0 / GATE_TAU)


def _gla_decay_kernel(glr_ref, w2_ref, gb_ref, cum_ref, b_ref):
    chunk = GLA_CHUNK
    g = _gla_log_decay(glr_ref[...], w2_ref[...], gb_ref[...])
    cum = cum_ref[...]
    for c in range(glr_ref.shape[0] // chunk):
        rows = slice(c * chunk, (c + 1) * chunk)
        gc = g[rows, :]
        b_ref[rows, 0:GLA_QK_DIM] = _exact_dot01(cum[0:chunk, :], gc)
        refs = _dot(cum[chunk:, :], gc.astype(BF16))
        for part in range(len(GLA_FINE_LEVELS)):
            cols = slice((1 + part) * GLA_QK_DIM, (2 + part) * GLA_QK_DIM)
            b_ref[rows, cols] = refs[part * chunk:(part + 1) * chunk, :]


def _gla_decay(glr, gate_w2, gate_b, cum, layer, tg):
    m = glr.shape[0]
    width = (1 + len(GLA_FINE_LEVELS)) * GLA_QK_DIM
    return pl.pallas_call(
        _gla_decay_kernel, grid=(m // tg,),
        in_specs=[pl.BlockSpec((tg, GATE_RANK), lambda i: (i, 0)),
                  pl.BlockSpec((None, GATE_RANK, GLA_QK_DIM), lambda i: (layer, 0, 0)),
                  pl.BlockSpec((None, 1, GLA_QK_DIM), lambda i: (layer, 0, 0)),
                  pl.BlockSpec(cum.shape, lambda i: (0, 0))],
        out_specs=pl.BlockSpec((tg, width), lambda i: (i, 0)),
        out_shape=jax.ShapeDtypeStruct((m, width), F32),
        compiler_params=_params(1), name="gla_decay")(glr, gate_w2, gate_b, cum)


def _gla_out(o, gn, og):
    var = jnp.mean(o * o, axis=-1, keepdims=True)
    return o * lax.rsqrt(var + EPS) * gn * (og * _sigmoid(og))


def _gla_prompt_kernel(q_ref, k_ref, v_ref, og_ref, b_ref, bf0_ref, bf1_ref, bh0_ref, bh1_ref, bh2_ref, bh3_ref,
                       gn_ref, sgn_ref, yb_ref, sfin_ref, s_ref):
    bf_refs = (bf0_ref, bf1_ref)
    bh_refs = (bh0_ref, bh1_ref, bh2_ref, bh3_ref)
    c_idx = pl.program_id(1)
    n_seq, chunk = q_ref.shape[0], q_ref.shape[1]

    @pl.when(c_idx == 0)
    def _():
        s_ref[...] = jnp.zeros(s_ref.shape, F32)

    row = lax.broadcasted_iota(jnp.int32, (chunk, chunk), 0)
    col = lax.broadcasted_iota(jnp.int32, (chunk, chunk), 1)
    diff_bits = row ^ col

    for n, h in [(n, h) for n in range(n_seq) for h in range(GLA_HEADS)]:
        ks = slice(h * GLA_DK, (h + 1) * GLA_DK)
        vs = slice(h * GLA_DV, (h + 1) * GLA_DV)
        b = b_ref[n, :, ks]
        b_last = b_ref[n, chunk - 1:chunk, ks]
        q = q_ref[n, :, ks]
        k = k_ref[n, :, ks]
        v = v_ref[n, :, vs]
        s_prev = s_ref[n, h]

        inter = _dot(q * jnp.exp(b).astype(BF16), s_prev.astype(BF16))

        acc = _dot_nt(q, k)
        for li, s in enumerate(GLA_LEVELS):
            if s in GLA_FINE_LEVELS:
                beta = bf_refs[GLA_FINE_LEVELS.index(s)][n, :, ks]
            else:
                beta = jnp.concatenate(
                    [bh_refs[h][n, pl.ds(int(r), 8, stride=0), :]
                     for r in _ref_rows(s, chunk)[::2 * s] for _ in range(2 * s // 8)], axis=0)
            e = jnp.exp2((b - beta) * sgn_ref[li * chunk:(li + 1) * chunk, :]).astype(BF16)
            p = _dot_nt(q * e, k * e)
            acc = jnp.where(diff_bits >= s, p, acc)
        a = jnp.where(row >= col, acc, 0.0)
        o = (inter + _dot(a.astype(BF16), v)) * (GLA_DK ** -0.5)

        kdec_t = (k.astype(F32) * jnp.exp(b_last - b)).T
        decay_t = jnp.broadcast_to(jnp.exp(b_last), (GLA_DK, GLA_DK)).T
        s_new = jnp.concatenate([decay_t, decay_t], axis=1) * s_prev + _dot(kdec_t.astype(BF16), v)
        s_ref[n, h] = s_new

        yb_ref[n, :, vs] = _gla_out(o, gn_ref[:, vs], og_ref[n, :, vs].astype(F32)).astype(BF16)

    @pl.when(c_idx == pl.num_programs(1) - 1)
    def _():
        sfin_ref[...] = s_ref[...]


def _gla_prompt(qkvg, glr, gate_w2, gate_b, gla_norm_g, layer, batch, seq, n_seq):
    chunk = GLA_CHUNK
    nc = seq // chunk
    cum, sgn = _gla_constants(chunk)
    b_all = _gla_decay(glr, gate_w2, gate_b, cum, layer, 8 * chunk).reshape(batch, seq, -1)
    qkvg = qkvg.reshape(batch, seq, -1)

    def col_spec(width, blk):
        return pl.BlockSpec((n_seq, chunk, width), lambda b, c: (b, c, blk))

    gn_spec = pl.BlockSpec((None, 1, GLA_V_DIM), lambda b, c: (layer, 0, 0))
    sgn_spec = pl.BlockSpec(sgn.shape, lambda b, c: (0, 0))
    yb_spec = pl.BlockSpec((n_seq, chunk, GLA_V_DIM), lambda b, c: (b, c, 0))
    s_spec = pl.BlockSpec((n_seq, GLA_HEADS, GLA_DK, GLA_DV), lambda b, c: (b, 0, 0, 0))
    yb, s_fin = pl.pallas_call(
        _gla_prompt_kernel, grid=(batch // n_seq, nc),
        in_specs=[col_spec(GLA_QK_DIM, 0), col_spec(GLA_QK_DIM, 1), col_spec(GLA_V_DIM, 1),
                  col_spec(GLA_V_DIM, 2)]
                 + [col_spec(GLA_QK_DIM, part) for part in range(1 + len(GLA_FINE_LEVELS))]
                 + [col_spec(GLA_DK, h) for h in range(GLA_HEADS)] + [gn_spec, sgn_spec],
        out_specs=[yb_spec, s_spec],
        out_shape=[jax.ShapeDtypeStruct((batch, seq, GLA_V_DIM), BF16),
                   jax.ShapeDtypeStruct((batch, GLA_HEADS, GLA_DK, GLA_DV), F32)],
        scratch_shapes=[pltpu.VMEM((n_seq, GLA_HEADS, GLA_DK, GLA_DV), F32)],
        compiler_params=_params(2), name="gla_prompt")(
            qkvg, qkvg, qkvg, qkvg, *([b_all] * (1 + len(GLA_FINE_LEVELS) + GLA_HEADS)), gla_norm_g, sgn)
    return yb.reshape(batch * seq, GLA_V_DIM), s_fin


def _columns(slab):
    pad = jnp.zeros((GLA_DK - GLA_STEP_ROWS, GLA_DK), F32)
    return jnp.concatenate([slab, pad], axis=0).T


def _gla_step_kernel(q_ref, k_ref, v_ref, og_ref, glr_ref, w2_ref, gb_ref, gn_ref, s_ref, *rest):
    yb_ref, snew_ref = rest[-2:]
    rows = GLA_STEP_ROWS
    decay_all = jnp.exp(_gla_log_decay(glr_ref[...], w2_ref[...], gb_ref[...]))
    row_id = lax.broadcasted_iota(jnp.int32, (rows, rows * GLA_DV), 0)
    lane_blk = lax.broadcasted_iota(jnp.int32, (rows, rows * GLA_DV), 1) // GLA_DV

    for h in range(GLA_HEADS):
        ks = slice(h * GLA_DK, (h + 1) * GLA_DK)
        vs = slice(h * GLA_DV, (h + 1) * GLA_DV)
        decay = decay_all[:, ks]
        q = q_ref[:, ks] * (GLA_DK ** -0.5)
        k = k_ref[:, ks]
        v = v_ref[:, vs]
        qk = jnp.sum(q * k, axis=-1, keepdims=True)

        v_diag = jnp.where(row_id == lane_blk, jnp.concatenate([v] * rows, axis=1), 0.0)
        v_diag = jnp.concatenate([v_diag, jnp.zeros_like(v_diag)], axis=0)
        k_t = _columns(k)[:, 0:2 * rows]
        kv = _dot(k_t.astype(BF16), v_diag.astype(BF16))

        decay_c = _columns(decay)
        qd = (q * decay).astype(BF16)
        inter_rows = []
        for i in range(rows):
            s_prev = s_ref[i, h]
            inter_rows.append(_dot(qd, s_prev.astype(BF16))[i:i + 1, :])
            snew_ref[i, h] = decay_c[:, i:i + 1] * s_prev + kv[:, i * GLA_DV:(i + 1) * GLA_DV]
        o = jnp.concatenate(inter_rows, axis=0) + qk * v
        yb_ref[:, vs] = _gla_out(o, gn_ref[:, vs], og_ref[:, vs]).astype(BF16)


def _gla_step(qkvg, glr, state, gate_w2, gate_b, gla_norm_g, layer, stacked):
    nb = qkvg.shape[0]
    rows = GLA_STEP_ROWS
    s_spec = pl.BlockSpec((None, rows, GLA_HEADS, GLA_DK, GLA_DV), lambda t: (layer, t, 0, 0, 0))
    in_specs = [pl.BlockSpec((rows, GLA_QK_DIM), lambda t: (t, 0)),
                pl.BlockSpec((rows, GLA_QK_DIM), lambda t: (t, 1)),
                pl.BlockSpec((rows, GLA_V_DIM), lambda t: (t, 1)),
                pl.BlockSpec((rows, GLA_V_DIM), lambda t: (t, 2)),
                pl.BlockSpec((rows, GATE_RANK), lambda t: (t, 0)),
                pl.BlockSpec((None, GATE_RANK, GLA_QK_DIM), lambda t: (layer, 0, 0)),
                pl.BlockSpec((None, 1, GLA_QK_DIM), lambda t: (layer, 0, 0)),
                pl.BlockSpec((None, 1, GLA_V_DIM), lambda t: (layer, 0, 0)),
                s_spec]
    args = [qkvg, qkvg, qkvg, qkvg, glr, gate_w2, gate_b, gla_norm_g, state]
    aliases = _stacked_operand(in_specs, args, stacked, 1)
    return pl.pallas_call(
        _gla_step_kernel, grid=(nb // rows,), in_specs=in_specs,
        out_specs=[pl.BlockSpec((rows, GLA_V_DIM), lambda t: (t, 0)), s_spec],
        out_shape=[jax.ShapeDtypeStruct((nb, GLA_V_DIM), BF16),
                   jax.ShapeDtypeStruct(state.shape, F32)],
        input_output_aliases=aliases,
        compiler_params=_params(1), name="gla_step")(*args)


def kernel(x_prompt, x_sample, state_conv, state_gla, state_ffn_conv, norm_mix_g, w_in, conv_w, gate_w2, gate_b, gla_norm_g, w_out, norm_ffn_g, w_up, ffn_conv_w, ffn_conv_b, w_down, final_norm_g):
    batch, seq, _ = x_prompt.shape
    nb = x_sample.shape[0]
    norm_mix_g, norm_ffn_g = norm_mix_g[:, None, :], norm_ffn_g[:, None, :]
    gate_b, gla_norm_g, ffn_conv_b = gate_b[:, None, :], gla_norm_g[:, None, :], ffn_conv_b[:, None, :]
    final_norm_g = final_norm_g[None, :]
    w_in_t = jnp.swapaxes(w_in, 1, 2)

    xp = x_prompt.reshape(batch * seq, D_MODEL)
    xs = x_sample.reshape(nb, D_MODEL)
    conv_p, gla_p, ffn_p = [], [], []
    conv_s, gla_s, ffn_s = None, None, None
    for l in range(DEPTH):
        ya, ya_s, qkvg, qkvg_s, glr, glr_s, c_new, conv_s = _mixer_in(
            xp, xs, state_conv, norm_mix_g, w_in_t, conv_w, l, seq, 256, conv_s)

        yb, s_new = _gla_prompt(qkvg, glr, gate_w2, gate_b, gla_norm_g, l, batch, seq, 2)
        xp = _out_proj(ya, yb, w_out, xp, l, 512)

        yb_s, gla_s = _gla_step(qkvg_s, glr_s, state_gla, gate_w2, gate_b, gla_norm_g, l, gla_s)
        xs = _out_proj(ya_s, yb_s, w_out, xs, l, nb)

        xp, xs, f_new, ffn_s = _ffn(xp, xs, state_ffn_conv, norm_ffn_g, w_up, ffn_conv_w, ffn_conv_b, w_down,
                                    final_norm_g, l, seq, 1024, 512, l == DEPTH - 1, ffn_s)
        conv_p.append(c_new)
        gla_p.append(s_new)
        ffn_p.append(f_new)

    return (xp.reshape(batch, seq, D_MODEL), xs.reshape(nb, 1, D_MODEL),
            jnp.stack(conv_p), jnp.stack(gla_p), jnp.stack(ffn_p), conv_s, gla_s, ffn_s)
```

```python
import functools

import numpy as np
import jax
import jax.numpy as jnp
from jax import lax
from jax.experimental import pallas as pl
from jax.experimental.pallas import tpu as pltpu

F32 = jnp.float32
BF16 = jnp.bfloat16

D_MODEL = 2048
DEPTH = 2
CONV_DIM = 1024
CONV_WIDTH = 3
GLA_HEADS = 4
GLA_QK_DIM = 512
GLA_V_DIM = 1024
GLA_DK = 128
GLA_DV = 256
GATE_RANK = 16
GATE_TAU = 16.0
D_FF = 5632
EPS = 1e-6
MAIN_COLS = 3 * CONV_DIM + 2 * GLA_QK_DIM + 2 * GLA_V_DIM

GLA_CHUNK = 128
GLA_LEVELS = tuple(1 << i for i in range(GLA_CHUNK.bit_length() - 1))
GLA_FINE_LEVELS = tuple(s for s in GLA_LEVELS if s < 4)
GLA_STEP_ROWS = 8
GLA_DECAY_ROWS = 8 * GLA_CHUNK

MIXER_IN_TN = 256
OUT_PROJ_TM = 512
FFN_TM, FFN_TF = 1024, 512

VMEM_BYTES_V7X = 64 * 1024 * 1024
VMEM_LIMIT = VMEM_BYTES_V7X - 2 * 1024 * 1024


def _params(n_axes):
    return pltpu.CompilerParams(dimension_semantics=("arbitrary",) * n_axes,
                                vmem_limit_bytes=VMEM_LIMIT)


def _dot(a, b):
    return jnp.dot(a, b, preferred_element_type=F32)


def _dot_nt(a, b):
    return lax.dot_general(a, b, (((1,), (1,)), ((), ())), preferred_element_type=F32)


def _split3(x):
    hi = x.astype(BF16)
    r1 = x - hi.astype(F32)
    mid = r1.astype(BF16)
    lo = (r1 - mid.astype(F32)).astype(BF16)
    return hi, mid, lo


def _exact_dot01(m01, x):
    x1, x2, x3 = _split3(x)
    return _dot(m01, x1) + _dot(m01, x2) + _dot(m01, x3)


def _sigmoid(x):
    return 1.0 / (1.0 + jnp.exp(-x))


def _log_sigmoid(x):
    return jnp.minimum(x, 0.0) - jnp.log1p(jnp.exp(-jnp.abs(x)))


def _rms(x, g):
    var = jnp.mean(x * x, axis=-1, keepdims=True)
    return x * lax.rsqrt(var + EPS) * g


def _delayed_rows(u, prev):
    sub = lax.broadcasted_iota(jnp.int32, prev.shape, 0)
    r1 = pltpu.roll(u, 1, axis=0)
    r2 = pltpu.roll(u, 2, axis=0)
    head1 = jnp.where(sub < 1, pltpu.roll(prev, 1, axis=0), r1[0:8, :])
    head2 = jnp.where(sub < 2, pltpu.roll(prev, 2, axis=0), r2[0:8, :])
    return (jnp.concatenate([head1, r1[8:, :]], axis=0), jnp.concatenate([head2, r2[8:, :]], axis=0))


def _prefetched_rows(x_hbm, xbuf_ref, sem_ref, tile, step, n_tiles):
    rows = xbuf_ref.shape[0]

    def copy(t):
        return pltpu.make_async_copy(x_hbm.at[pl.ds(t * rows, rows), :], xbuf_ref, sem_ref.at[0])

    @pl.when(jnp.logical_and(tile == 0, step == 0))
    def _():
        copy(0).start()

    @pl.when(step == 0)
    def _():
        copy(tile).wait()

    @pl.when(jnp.logical_and(step == 1, tile + 1 < n_tiles))
    def _():
        copy(tile + 1).start()


def _stacked_operand(in_specs, args, stacked, out_index):
    if stacked is None:
        return {}
    in_specs.append(pl.BlockSpec(memory_space=pl.ANY))
    args.append(stacked)
    return {len(args) - 1: out_index}


def _conv_tap_step(new_row, st_ref, w_ref, new_st_ref):
    p1 = st_ref[:, 1, :]
    cu = st_ref[:, 0, :] * w_ref[0:1, :]
    cu = cu + p1 * w_ref[1:2, :]
    cu = cu + new_row * w_ref[2:3, :]
    new_st_ref[:, 0, :] = p1
    new_st_ref[:, 1, :] = new_row
    return cu


def _mixer_in_kernel(x_hbm, xs_ref, sconv_ref, g_ref, wa_ref, wb_ref, wc_ref, wgt_ref, cw_ref, *rest, n_conv):
    (ya_ref, yas_ref, o_ref, os_ref, og_ref, ogs_ref, st_ref, sconv_new_ref,
     xn_ref, xbuf_ref, sem_ref) = rest[-11:]
    j = pl.program_id(1)
    tm = xbuf_ref.shape[0]
    ns = xs_ref.shape[0]
    tn = wa_ref.shape[0]
    _prefetched_rows(x_hbm, xbuf_ref, sem_ref, pl.program_id(0), j, pl.num_programs(0))

    @pl.when(j == 0)
    def _():
        xn_ref[0:tm, :] = _rms(xbuf_ref[...], g_ref[...]).astype(BF16)
        xn_ref[tm:tm + ns, :] = _rms(xs_ref[...], g_ref[...]).astype(BF16)
        gate = _dot_nt(xn_ref[...], wgt_ref[...].astype(BF16))
        og_ref[...] = gate[0:tm, :]
        ogs_ref[...] = gate[tm:tm + ns, :]

    def projected():
        xn = xn_ref[...]
        return [_dot_nt(xn, w_ref[...].astype(BF16)) for w_ref in (wa_ref, wb_ref, wc_ref)]

    @pl.when(j < n_conv)
    def _():
        bg, cg, hin = projected()
        z = cg[0:tm, :] * hin[0:tm, :]
        z1, z2 = _delayed_rows(z, jnp.zeros((8, tn), F32))
        cu = z2 * cw_ref[0:1, :]
        cu = cu + z1 * cw_ref[1:2, :]
        cu = cu + z * cw_ref[2:3, :]
        ya_ref[...] = (bg[0:tm, :] * cu).astype(BF16)
        st_ref[...] = z[tm - (CONV_WIDTH - 1):tm, :]
        cu_s = _conv_tap_step(cg[tm:tm + ns, :] * hin[tm:tm + ns, :], sconv_ref, cw_ref, sconv_new_ref)
        yas_ref[...] = (bg[tm:tm + ns, :] * cu_s).astype(BF16)

    @pl.when(j >= n_conv)
    def _():
        for idx, tile in enumerate(projected()):
            cols = slice(idx * tn, (idx + 1) * tn)
            o_ref[:, cols] = tile[0:tm, :].astype(BF16)
            os_ref[:, cols] = tile[tm:tm + ns, :]


def _mixer_in(x, xs, state_conv, g, w_in_t, conv_w, layer, seq, tn, stacked):
    m, nb = x.shape[0], xs.shape[0]
    batch = m // seq
    ns = nb // batch
    n_conv = CONV_DIM // tn
    qkvg = MAIN_COLS - 3 * CONV_DIM
    n_rest = qkvg // (3 * tn)
    last_conv = n_conv - 1

    def w_spec(group):
        def index(i, j):
            conv_tile = group * n_conv + j
            rest_tile = 3 * n_conv + 3 * (j - n_conv) + group
            return (layer, jnp.where(j < n_conv, conv_tile, rest_tile), 0)
        return pl.BlockSpec((None, tn, D_MODEL), index)

    conv_col = lambda i, j: jnp.minimum(j, last_conv)
    rest_col = lambda i, j: jnp.maximum(j - n_conv, 0)
    x_spec = pl.BlockSpec(memory_space=pl.ANY)
    xs_spec = pl.BlockSpec((ns, D_MODEL), lambda i, j: (i, 0))
    sconv_spec = pl.BlockSpec((None, ns, CONV_WIDTH - 1, tn), lambda i, j: (layer, i, 0, conv_col(i, j)))
    g_spec = pl.BlockSpec((None, 1, D_MODEL), lambda i, j: (layer, 0, 0))
    wgt_spec = pl.BlockSpec((None, GATE_RANK, D_MODEL), lambda i, j: (layer, MAIN_COLS // GATE_RANK, 0))
    cw_spec = pl.BlockSpec((None, CONV_WIDTH, tn), lambda i, j: (layer, 0, conv_col(i, j)))
    in_specs = [x_spec, xs_spec, sconv_spec, g_spec, w_spec(0), w_spec(1), w_spec(2), wgt_spec, cw_spec]
    args = [x, xs, state_conv, g, w_in_t, w_in_t, w_in_t, w_in_t, conv_w]
    aliases = _stacked_operand(in_specs, args, stacked, 7)
    out_specs = [pl.BlockSpec((seq, tn), lambda i, j: (i, conv_col(i, j))),
                 pl.BlockSpec((ns, tn), lambda i, j: (i, conv_col(i, j))),
                 pl.BlockSpec((seq, 3 * tn), lambda i, j: (i, rest_col(i, j))),
                 pl.BlockSpec((ns, 3 * tn), lambda i, j: (i, rest_col(i, j))),
                 pl.BlockSpec((seq, GATE_RANK), lambda i, j: (i, 0)),
                 pl.BlockSpec((ns, GATE_RANK), lambda i, j: (i, 0)),
                 pl.BlockSpec((None, CONV_WIDTH - 1, tn), lambda i, j: (i, 0, conv_col(i, j))),
                 sconv_spec]
    out_shape = [jax.ShapeDtypeStruct((m, CONV_DIM), BF16), jax.ShapeDtypeStruct((nb, CONV_DIM), BF16),
                 jax.ShapeDtypeStruct((m, qkvg), BF16), jax.ShapeDtypeStruct((nb, qkvg), F32),
                 jax.ShapeDtypeStruct((m, GATE_RANK), F32), jax.ShapeDtypeStruct((nb, GATE_RANK), F32),
                 jax.ShapeDtypeStruct((batch, CONV_WIDTH - 1, CONV_DIM), F32),
                 jax.ShapeDtypeStruct(state_conv.shape, F32)]
    return pl.pallas_call(
        functools.partial(_mixer_in_kernel, n_conv=n_conv), grid=(batch, n_conv + n_rest),
        in_specs=in_specs, out_specs=out_specs, out_shape=out_shape,
        scratch_shapes=[pltpu.VMEM((seq + ns, D_MODEL), BF16), pltpu.VMEM((seq, D_MODEL), F32),
                        pltpu.SemaphoreType.DMA((1,))],
        input_output_aliases=aliases, compiler_params=_params(2), name="mixer_in")(*args)


def _out_proj_kernel(ya_ref, yb_ref, wa_ref, wb_ref, x_ref, o_ref):
    acc = _dot(ya_ref[...], wa_ref[...].astype(BF16))
    acc = acc + _dot(yb_ref[...], wb_ref[...].astype(BF16))
    o_ref[...] = x_ref[...] + acc


def _out_proj(ya, yb, w_out, x, layer, tm):
    m = x.shape[0]
    a_spec = pl.BlockSpec((tm, CONV_DIM), lambda i: (i, 0))
    wa_spec = pl.BlockSpec((None, CONV_DIM, D_MODEL), lambda i: (layer, 0, 0), pipeline_mode=pl.Buffered(1))
    wb_spec = pl.BlockSpec((None, GLA_V_DIM, D_MODEL), lambda i: (layer, 1, 0), pipeline_mode=pl.Buffered(1))
    x_spec = pl.BlockSpec((tm, D_MODEL), lambda i: (i, 0))
    return pl.pallas_call(
        _out_proj_kernel, grid=(m // tm,), in_specs=[a_spec, a_spec, wa_spec, wb_spec, x_spec],
        out_specs=x_spec, out_shape=jax.ShapeDtypeStruct((m, D_MODEL), F32),
        compiler_params=_params(1), name="out_proj")(ya, yb, w_out, w_out, x)


def _ffn_kernel(x_hbm, xs_ref, sst_ref, g_ref, wu_ref, wv_ref, cw_ref, cb_ref, wd_ref, fg_ref, *rest,
                tiles_per_seq, final_norm):
    o_ref, os_ref, st_ref, sst_new_ref, xn_ref, prev_ref, carry_ref, xbuf_ref, sem_ref = rest[-9:]
    i = pl.program_id(0)
    f = pl.program_id(1)
    tm = xbuf_ref.shape[0]
    ns = xs_ref.shape[0]
    _prefetched_rows(x_hbm, xbuf_ref, sem_ref, i, f, pl.num_programs(0))

    @pl.when(f == 0)
    def _():
        x = xbuf_ref[...]
        xs = xs_ref[...]
        xn_ref[0:tm, :] = _rms(x, g_ref[...]).astype(BF16)
        xn_ref[tm:tm + ns, :] = _rms(xs, g_ref[...]).astype(BF16)
        o_ref[...] = x
        os_ref[...] = xs

    first = lax.rem(i, tiles_per_seq) == 0

    @pl.when(first)
    def _():
        prev_ref[...] = jnp.zeros(prev_ref.shape, F32)

    @pl.when(jnp.logical_not(first))
    def _():
        prev_ref[...] = carry_ref[f]

    xn = xn_ref[...]
    u_all = _dot(xn, wu_ref[...].astype(BF16))
    v_all = _dot(xn, wv_ref[...].astype(BF16))
    u = u_all[0:tm, :]
    u1, u2 = _delayed_rows(u, prev_ref[...])
    cu = u2 * cw_ref[0:1, :]
    cu = cu + u1 * cw_ref[1:2, :]
    cu = cu + u * cw_ref[2:3, :]
    carry_ref[f] = u[tm - 8:tm, :]
    st_ref[...] = u[tm - (CONV_WIDTH - 1):tm, :]
    cu_s = _conv_tap_step(u_all[tm:tm + ns, :], sst_ref, cw_ref, sst_new_ref)

    pre = jnp.concatenate([cu, cu_s], axis=0) + cb_ref[...]
    h = (pre * _sigmoid(pre) * v_all).astype(BF16)
    acc = _dot(h, wd_ref[...].astype(BF16))
    o_ref[...] += acc[0:tm, :]
    os_ref[...] += acc[tm:tm + ns, :]

    if final_norm:
        @pl.when(f == pl.num_programs(1) - 1)
        def _():
            o_ref[...] = _rms(o_ref[...], fg_ref[...])
            os_ref[...] = _rms(os_ref[...], fg_ref[...])


def _ffn(x, xs, state_s, g, w_up, conv_w, conv_b, w_down, final_g, layer, seq, tm, tf, final_norm, stacked):
    m = x.shape[0]
    n_tiles = m // tm
    ns = xs.shape[0] // n_tiles
    tiles_per_seq = seq // tm
    nf = D_FF // tf
    row_spec = pl.BlockSpec((tm, D_MODEL), lambda i, f: (i, 0))
    srow_spec = pl.BlockSpec((ns, D_MODEL), lambda i, f: (i, 0))
    sst_spec = pl.BlockSpec((None, ns, CONV_WIDTH - 1, tf), lambda i, f: (layer, i, 0, f))
    g_spec = pl.BlockSpec((None, 1, D_MODEL), lambda i, f: (layer, 0, 0))
    wu_spec = pl.BlockSpec((None, D_MODEL, tf), lambda i, f: (layer, 0, f))
    wv_spec = pl.BlockSpec((None, D_MODEL, tf), lambda i, f: (layer, 0, nf + f))
    cw_spec = pl.BlockSpec((None, CONV_WIDTH, tf), lambda i, f: (layer, 0, f))
    cb_spec = pl.BlockSpec((None, 1, tf), lambda i, f: (layer, 0, f))
    wd_spec = pl.BlockSpec((None, tf, D_MODEL), lambda i, f: (layer, f, 0))
    fg_spec = pl.BlockSpec((1, D_MODEL), lambda i, f: (0, 0))
    st_spec = pl.BlockSpec((None, CONV_WIDTH - 1, tf), lambda i, f: (i, 0, f))
    x_spec = pl.BlockSpec(memory_space=pl.ANY)
    in_specs = [x_spec, srow_spec, sst_spec, g_spec, wu_spec, wv_spec, cw_spec, cb_spec, wd_spec, fg_spec]
    args = [x, xs, state_s, g, w_up, w_up, conv_w, conv_b, w_down, final_g]
    aliases = _stacked_operand(in_specs, args, stacked, 3)
    body = functools.partial(_ffn_kernel, tiles_per_seq=tiles_per_seq, final_norm=final_norm)
    out, out_s, tails, state_new = pl.pallas_call(
        body, grid=(n_tiles, nf), in_specs=in_specs,
        out_specs=[row_spec, srow_spec, st_spec, sst_spec],
        out_shape=[jax.ShapeDtypeStruct((m, D_MODEL), F32),
                   jax.ShapeDtypeStruct(xs.shape, F32),
                   jax.ShapeDtypeStruct((n_tiles, CONV_WIDTH - 1, D_FF), F32),
                   jax.ShapeDtypeStruct(state_s.shape, F32)],
        scratch_shapes=[pltpu.VMEM((tm + ns, D_MODEL), BF16), pltpu.VMEM((8, tf), F32),
                        pltpu.VMEM((nf, 8, tf), F32), pltpu.VMEM((tm, D_MODEL), F32),
                        pltpu.SemaphoreType.DMA((1,))],
        input_output_aliases=aliases, compiler_params=_params(2), name="ffn")(*args)
    return out, out_s, tails[tiles_per_seq - 1::tiles_per_seq], state_new


def _ref_rows(s, chunk):
    rows = np.arange(chunk)
    return (rows // (2 * s)) * (2 * s) + s - 1


def _gla_constants(chunk):
    rows = np.arange(chunk)
    tri = (rows[:, None] >= rows[None, :]).astype(np.float32)
    cums = [tri] + [tri[_ref_rows(s, chunk)] for s in GLA_FINE_LEVELS]
    sgns = [np.where(rows > _ref_rows(s, chunk), 1.0, -1.0) * np.log2(np.e) for s in GLA_LEVELS]
    sgn = np.broadcast_to(np.concatenate(sgns)[:, None], (len(GLA_LEVELS) * chunk, GLA_DK))
    return jnp.asarray(np.concatenate(cums, axis=0), BF16), jnp.asarray(sgn, F32)


def _gla_log_decay(glr, w2, gb):
    gate = _dot(glr.astype(BF16), w2.astype(BF16)) + gb
    return _log_sigmoid(gate) * (1.0 / GATE_TAU)


def _gla_decay_kernel(glr_ref, w2_ref, gb_ref, cum_ref, b_ref):
    chunk = GLA_CHUNK
    g = _gla_log_decay(glr_ref[...], w2_ref[...], gb_ref[...])
    cum = cum_ref[...]
    for c in range(glr_ref.shape[0] // chunk):
        rows = slice(c * chunk, (c + 1) * chunk)
        cums = _exact_dot01(cum, g[rows, :])
        for part in range(1 + len(GLA_FINE_LEVELS)):
            b_ref[rows, part * GLA_QK_DIM:(part + 1) * GLA_QK_DIM] = cums[part * chunk:(part + 1) * chunk, :]


def _gla_decay(glr, gate_w2, gate_b, cum, layer, tg):
    m = glr.shape[0]
    width = (1 + len(GLA_FINE_LEVELS)) * GLA_QK_DIM
    return pl.pallas_call(
        _gla_decay_kernel, grid=(m // tg,),
        in_specs=[pl.BlockSpec((tg, GATE_RANK), lambda i: (i, 0)),
                  pl.BlockSpec((None, GATE_RANK, GLA_QK_DIM), lambda i: (layer, 0, 0)),
                  pl.BlockSpec((None, 1, GLA_QK_DIM), lambda i: (layer, 0, 0)),
                  pl.BlockSpec(cum.shape, lambda i: (0, 0))],
        out_specs=pl.BlockSpec((tg, width), lambda i: (i, 0)),
        out_shape=jax.ShapeDtypeStruct((m, width), F32),
        compiler_params=_params(1), name="gla_decay")(glr, gate_w2, gate_b, cum)


def _gla_out(o, gn, og):
    var = jnp.mean(o * o, axis=-1, keepdims=True)
    return o * lax.rsqrt(var + EPS) * gn * (og * _sigmoid(og))


def _gla_prompt_kernel(q_ref, k_ref, v_ref, og_ref, b_ref, bf0_ref, bf1_ref, bh0_ref, bh1_ref, bh2_ref, bh3_ref,
                       gn_ref, sgn_ref, yb_ref, sfin_ref, s_ref):
    bf_refs = (bf0_ref, bf1_ref)
    bh_refs = (bh0_ref, bh1_ref, bh2_ref, bh3_ref)
    c_idx = pl.program_id(1)
    n_seq, chunk = q_ref.shape[0], q_ref.shape[1]

    @pl.when(c_idx == 0)
    def _():
        s_ref[...] = jnp.zeros(s_ref.shape, F32)

    row = lax.broadcasted_iota(jnp.int32, (chunk, chunk), 0)
    col = lax.broadcasted_iota(jnp.int32, (chunk, chunk), 1)
    diff_bits = row ^ col

    for n, h in [(n, h) for n in range(n_seq) for h in range(GLA_HEADS)]:
        ks = slice(h * GLA_DK, (h + 1) * GLA_DK)
        vs = slice(h * GLA_DV, (h + 1) * GLA_DV)
        b = b_ref[n, :, ks]
        b_last = b_ref[n, chunk - 1:chunk, ks]
        q = q_ref[n, :, ks]
        k = k_ref[n, :, ks]
        v = v_ref[n, :, vs]
        s_prev = s_ref[n, h]

        inter = _dot(q * jnp.exp(b).astype(BF16), s_prev.astype(BF16))

        acc = _dot_nt(q, k)
        for li, s in enumerate(GLA_LEVELS):
            if s in GLA_FINE_LEVELS:
                beta = bf_refs[GLA_FINE_LEVELS.index(s)][n, :, ks]
            else:
                beta = jnp.concatenate(
                    [bh_refs[h][n, pl.ds(int(r), 8, stride=0), :]
                     for r in _ref_rows(s, chunk)[::2 * s] for _ in range(2 * s // 8)], axis=0)
            e = jnp.exp2((b - beta) * sgn_ref[li * chunk:(li + 1) * chunk, :]).astype(BF16)
            p = _dot_nt(q * e, k * e)
            acc = jnp.where(diff_bits >= s, p, acc)
        a = jnp.where(row >= col, acc, 0.0)
        o = (inter + _dot(a.astype(BF16), v)) * (GLA_DK ** -0.5)

        kdec_t = (k.astype(F32) * jnp.exp(b_last - b)).T
        decay_t = jnp.broadcast_to(jnp.exp(b_last), (GLA_DK, GLA_DK)).T
        s_new = jnp.concatenate([decay_t, decay_t], axis=1) * s_prev + _dot(kdec_t.astype(BF16), v)
        s_ref[n, h] = s_new

        yb_ref[n, :, vs] = _gla_out(o, gn_ref[:, vs], og_ref[n, :, vs].astype(F32)).astype(BF16)

    @pl.when(c_idx == pl.num_programs(1) - 1)
    def _():
        sfin_ref[...] = s_ref[...]


def _gla_prompt(qkvg, glr, gate_w2, gate_b, gla_norm_g, layer, batch, seq, n_seq):
    chunk = GLA_CHUNK
    nc = seq // chunk
    cum, sgn = _gla_constants(chunk)
    b_all = _gla_decay(glr, gate_w2, gate_b, cum, layer, GLA_DECAY_ROWS).reshape(batch, seq, -1)
    qkvg = qkvg.reshape(batch, seq, -1)

    def col_spec(width, blk):
        return pl.BlockSpec((n_seq, chunk, width), lambda b, c: (b, c, blk))

    gn_spec = pl.BlockSpec((None, 1, GLA_V_DIM), lambda b, c: (layer, 0, 0))
    sgn_spec = pl.BlockSpec(sgn.shape, lambda b, c: (0, 0))
    yb_spec = pl.BlockSpec((n_seq, chunk, GLA_V_DIM), lambda b, c: (b, c, 0))
    s_spec = pl.BlockSpec((n_seq, GLA_HEADS, GLA_DK, GLA_DV), lambda b, c: (b, 0, 0, 0))
    yb, s_fin = pl.pallas_call(
        _gla_prompt_kernel, grid=(batch // n_seq, nc),
        in_specs=[col_spec(GLA_QK_DIM, 0), col_spec(GLA_QK_DIM, 1), col_spec(GLA_V_DIM, 1),
                  col_spec(GLA_V_DIM, 2)]
                 + [col_spec(GLA_QK_DIM, part) for part in range(1 + len(GLA_FINE_LEVELS))]
                 + [col_spec(GLA_DK, h) for h in range(GLA_HEADS)] + [gn_spec, sgn_spec],
        out_specs=[yb_spec, s_spec],
        out_shape=[jax.ShapeDtypeStruct((batch, seq, GLA_V_DIM), BF16),
                   jax.ShapeDtypeStruct((batch, GLA_HEADS, GLA_DK, GLA_DV), F32)],
        scratch_shapes=[pltpu.VMEM((n_seq, GLA_HEADS, GLA_DK, GLA_DV), F32)],
        compiler_params=_params(2), name="gla_prompt")(
            qkvg, qkvg, qkvg, qkvg, *([b_all] * (1 + len(GLA_FINE_LEVELS) + GLA_HEADS)), gla_norm_g, sgn)
    return yb.reshape(batch * seq, GLA_V_DIM), s_fin


def _columns(slab):
    pad = jnp.zeros((GLA_DK - GLA_STEP_ROWS, GLA_DK), F32)
    return jnp.concatenate([slab, pad], axis=0).T


def _gla_step_kernel(q_ref, k_ref, v_ref, og_ref, glr_ref, w2_ref, gb_ref, gn_ref, s_ref, *rest):
    yb_ref, snew_ref = rest[-2:]
    rows = GLA_STEP_ROWS
    decay_all = jnp.exp(_gla_log_decay(glr_ref[...], w2_ref[...], gb_ref[...]))
    row_id = lax.broadcasted_iota(jnp.int32, (rows, rows * GLA_DV), 0)
    lane_blk = lax.broadcasted_iota(jnp.int32, (rows, rows * GLA_DV), 1) // GLA_DV

    for h in range(GLA_HEADS):
        ks = slice(h * GLA_DK, (h + 1) * GLA_DK)
        vs = slice(h * GLA_DV, (h + 1) * GLA_DV)
        decay = decay_all[:, ks]
        q = q_ref[:, ks] * (GLA_DK ** -0.5)
        k = k_ref[:, ks]
        v = v_ref[:, vs]
        qk = jnp.sum(q * k, axis=-1, keepdims=True)

        v_diag = jnp.where(row_id == lane_blk, jnp.concatenate([v] * rows, axis=1), 0.0)
        v_diag = jnp.concatenate([v_diag, jnp.zeros_like(v_diag)], axis=0)
        k_t = _columns(k)[:, 0:2 * rows]
        kv = _dot(k_t.astype(BF16), v_diag.astype(BF16))

        decay_c = _columns(decay)
        qd = (q * decay).astype(BF16)
        inter_rows = []
        for i in range(rows):
            s_prev = s_ref[i, h]
            inter_rows.append(_dot(qd, s_prev.astype(BF16))[i:i + 1, :])
            snew_ref[i, h] = decay_c[:, i:i + 1] * s_prev + kv[:, i * GLA_DV:(i + 1) * GLA_DV]
        o = jnp.concatenate(inter_rows, axis=0) + qk * v
        yb_ref[:, vs] = _gla_out(o, gn_ref[:, vs], og_ref[:, vs]).astype(BF16)


def _gla_step(qkvg, glr, state, gate_w2, gate_b, gla_norm_g, layer, stacked):
    nb = qkvg.shape[0]
    rows = GLA_STEP_ROWS
    s_spec = pl.BlockSpec((None, rows, GLA_HEADS, GLA_DK, GLA_DV), lambda t: (layer, t, 0, 0, 0))
    in_specs = [pl.BlockSpec((rows, GLA_QK_DIM), lambda t: (t, 0)),
                pl.BlockSpec((rows, GLA_QK_DIM), lambda t: (t, 1)),
                pl.BlockSpec((rows, GLA_V_DIM), lambda t: (t, 1)),
                pl.BlockSpec((rows, GLA_V_DIM), lambda t: (t, 2)),
                pl.BlockSpec((rows, GATE_RANK), lambda t: (t, 0)),
                pl.BlockSpec((None, GATE_RANK, GLA_QK_DIM), lambda t: (layer, 0, 0)),
                pl.BlockSpec((None, 1, GLA_QK_DIM), lambda t: (layer, 0, 0)),
                pl.BlockSpec((None, 1, GLA_V_DIM), lambda t: (layer, 0, 0)),
                s_spec]
    args = [qkvg, qkvg, qkvg, qkvg, glr, gate_w2, gate_b, gla_norm_g, state]
    aliases = _stacked_operand(in_specs, args, stacked, 1)
    return pl.pallas_call(
        _gla_step_kernel, grid=(nb // rows,), in_specs=in_specs,
        out_specs=[pl.BlockSpec((rows, GLA_V_DIM), lambda t: (t, 0)), s_spec],
        out_shape=[jax.ShapeDtypeStruct((nb, GLA_V_DIM), BF16),
                   jax.ShapeDtypeStruct(state.shape, F32)],
        input_output_aliases=aliases,
        compiler_params=_params(1), name="gla_step")(*args)


def kernel(x_prompt, x_sample, state_conv, state_gla, state_ffn_conv, norm_mix_g, w_in, conv_w, gate_w2, gate_b, gla_norm_g, w_out, norm_ffn_g, w_up, ffn_conv_w, ffn_conv_b, w_down, final_norm_g):
    batch, seq, _ = x_prompt.shape
    nb = x_sample.shape[0]
    norm_mix_g, norm_ffn_g = norm_mix_g[:, None, :], norm_ffn_g[:, None, :]
    gate_b, gla_norm_g, ffn_conv_b = gate_b[:, None, :], gla_norm_g[:, None, :], ffn_conv_b[:, None, :]
    final_norm_g = final_norm_g[None, :]
    w_in_t = jnp.swapaxes(w_in, 1, 2)

    xp = x_prompt.reshape(batch * seq, D_MODEL)
    xs = x_sample.reshape(nb, D_MODEL)
    conv_p, gla_p, ffn_p = [], [], []
    conv_s, gla_s, ffn_s = None, None, None
    for l in range(DEPTH):
        ya, ya_s, qkvg, qkvg_s, glr, glr_s, c_new, conv_s = _mixer_in(
            xp, xs, state_conv, norm_mix_g, w_in_t, conv_w, l, seq, MIXER_IN_TN, conv_s)

        yb, s_new = _gla_prompt(qkvg, glr, gate_w2, gate_b, gla_norm_g, l, batch, seq, batch)
        xp = _out_proj(ya, yb, w_out, xp, l, OUT_PROJ_TM)

        yb_s, gla_s = _gla_step(qkvg_s, glr_s, state_gla, gate_w2, gate_b, gla_norm_g, l, gla_s)
        xs = _out_proj(ya_s, yb_s, w_out, xs, l, nb)

        xp, xs, f_new, ffn_s = _ffn(xp, xs, state_ffn_conv, norm_ffn_g, w_up, ffn_conv_w, ffn_conv_b, w_down,
                                    final_norm_g, l, seq, FFN_TM, FFN_TF, l == DEPTH - 1, ffn_s)
        conv_p.append(c_new)
        gla_p.append(s_new)
        ffn_p.append(f_new)

    return (xp.reshape(batch, seq, D_MODEL), xs.reshape(nb, 1, D_MODEL),
            jnp.stack(conv_p), jnp.stack(gla_p), jnp.stack(ffn_p), conv_s, gla_s, ffn_s)
```

```python
import functools

import numpy as np
import jax
import jax.numpy as jnp
from jax import lax
from jax.experimental import pallas as pl
from jax.experimental.pallas import tpu as pltpu

F32 = jnp.float32
BF16 = jnp.bfloat16

D_MODEL = 2048
DEPTH = 2
CONV_DIM = 1024
CONV_WIDTH = 3
GLA_HEADS = 4
GLA_QK_DIM = 512
GLA_V_DIM = 1024
GLA_DK = 128
GLA_DV = 256
GATE_RANK = 16
GATE_TAU = 16.0
D_FF = 5632
EPS = 1e-6
MAIN_COLS = 3 * CONV_DIM + 2 * GLA_QK_DIM + 2 * GLA_V_DIM

GLA_CHUNK = 128
GLA_LEVELS = tuple(1 << i for i in range(GLA_CHUNK.bit_length() - 1))
GLA_FINE_LEVELS = tuple(s for s in GLA_LEVELS if s < 4)
GLA_STEP_ROWS = 8
GLA_DECAY_ROWS = 8 * GLA_CHUNK

MIXER_IN_TN = 256
OUT_PROJ_TM = 512
OUT_PROJ_TN_SAMPLE = 512
FFN_TM, FFN_TF = 1024, 512

VMEM_BYTES_V7X = 64 * 1024 * 1024
VMEM_LIMIT = VMEM_BYTES_V7X - 2 * 1024 * 1024


def _params(n_axes):
    return pltpu.CompilerParams(dimension_semantics=("arbitrary",) * n_axes,
                                vmem_limit_bytes=VMEM_LIMIT)


def _dot(a, b):
    return jnp.dot(a, b, preferred_element_type=F32)


def _dot_nt(a, b):
    return lax.dot_general(a, b, (((1,), (1,)), ((), ())), preferred_element_type=F32)


def _split3(x):
    hi = x.astype(BF16)
    r1 = x - hi.astype(F32)
    mid = r1.astype(BF16)
    lo = (r1 - mid.astype(F32)).astype(BF16)
    return hi, mid, lo


def _exact_dot01(m01, x):
    x1, x2, x3 = _split3(x)
    return _dot(m01, x1) + _dot(m01, x2) + _dot(m01, x3)


def _sigmoid(x):
    return 1.0 / (1.0 + jnp.exp(-x))


def _log_sigmoid(x):
    return jnp.minimum(x, 0.0) - jnp.log1p(jnp.exp(-jnp.abs(x)))


def _rms(x, g):
    var = jnp.mean(x * x, axis=-1, keepdims=True)
    return x * lax.rsqrt(var + EPS) * g


def _delayed_rows(u, prev):
    sub = lax.broadcasted_iota(jnp.int32, prev.shape, 0)
    r1 = pltpu.roll(u, 1, axis=0)
    r2 = pltpu.roll(u, 2, axis=0)
    head1 = jnp.where(sub < 1, pltpu.roll(prev, 1, axis=0), r1[0:8, :])
    head2 = jnp.where(sub < 2, pltpu.roll(prev, 2, axis=0), r2[0:8, :])
    return (jnp.concatenate([head1, r1[8:, :]], axis=0), jnp.concatenate([head2, r2[8:, :]], axis=0))


def _prefetched_rows(x_hbm, xbuf_ref, sem_ref, tile, step, n_tiles):
    rows = xbuf_ref.shape[0]

    def copy(t):
        return pltpu.make_async_copy(x_hbm.at[pl.ds(t * rows, rows), :], xbuf_ref, sem_ref.at[0])

    @pl.when(jnp.logical_and(tile == 0, step == 0))
    def _():
        copy(0).start()

    @pl.when(step == 0)
    def _():
        copy(tile).wait()

    @pl.when(jnp.logical_and(step == 1, tile + 1 < n_tiles))
    def _():
        copy(tile + 1).start()


def _stacked_operand(in_specs, args, stacked, out_index):
    if stacked is None:
        return {}
    in_specs.append(pl.BlockSpec(memory_space=pl.ANY))
    args.append(stacked)
    return {len(args) - 1: out_index}


def _conv_tap_step(new_row, st_ref, w_ref, new_st_ref):
    p1 = st_ref[:, 1, :]
    cu = st_ref[:, 0, :] * w_ref[0:1, :]
    cu = cu + p1 * w_ref[1:2, :]
    cu = cu + new_row * w_ref[2:3, :]
    new_st_ref[:, 0, :] = p1
    new_st_ref[:, 1, :] = new_row
    return cu


def _mixer_in_kernel(x_hbm, xs_ref, sconv_ref, g_ref, wa_ref, wb_ref, wc_ref, wgt_ref, cw_ref, *rest, n_conv):
    (ya_ref, yas_ref, o_ref, os_ref, og_ref, ogs_ref, st_ref, sconv_new_ref,
     xn_ref, xbuf_ref, sem_ref) = rest[-11:]
    j = pl.program_id(1)
    tm = xbuf_ref.shape[0]
    ns = xs_ref.shape[0]
    tn = wa_ref.shape[0]
    _prefetched_rows(x_hbm, xbuf_ref, sem_ref, pl.program_id(0), j, pl.num_programs(0))

    @pl.when(j == 0)
    def _():
        xn_ref[0:tm, :] = _rms(xbuf_ref[...], g_ref[...]).astype(BF16)
        xn_ref[tm:tm + ns, :] = _rms(xs_ref[...], g_ref[...]).astype(BF16)
        gate = _dot_nt(xn_ref[...], wgt_ref[...].astype(BF16))
        og_ref[...] = gate[0:tm, :]
        ogs_ref[...] = gate[tm:tm + ns, :]

    def projected():
        xn = xn_ref[...]
        return [_dot_nt(xn, w_ref[...].astype(BF16)) for w_ref in (wa_ref, wb_ref, wc_ref)]

    @pl.when(j < n_conv)
    def _():
        bg, cg, hin = projected()
        z = cg[0:tm, :] * hin[0:tm, :]
        z1, z2 = _delayed_rows(z, jnp.zeros((8, tn), F32))
        cu = z2 * cw_ref[0:1, :]
        cu = cu + z1 * cw_ref[1:2, :]
        cu = cu + z * cw_ref[2:3, :]
        ya_ref[...] = (bg[0:tm, :] * cu).astype(BF16)
        st_ref[...] = z[tm - (CONV_WIDTH - 1):tm, :]
        cu_s = _conv_tap_step(cg[tm:tm + ns, :] * hin[tm:tm + ns, :], sconv_ref, cw_ref, sconv_new_ref)
        yas_ref[...] = (bg[tm:tm + ns, :] * cu_s).astype(BF16)

    @pl.when(j >= n_conv)
    def _():
        for idx, tile in enumerate(projected()):
            cols = slice(idx * tn, (idx + 1) * tn)
            o_ref[:, cols] = tile[0:tm, :].astype(BF16)
            os_ref[:, cols] = tile[tm:tm + ns, :]


def _mixer_in(x, xs, state_conv, g, w_in_t, conv_w, layer, seq, tn, stacked):
    m, nb = x.shape[0], xs.shape[0]
    batch = m // seq
    ns = nb // batch
    n_conv = CONV_DIM // tn
    qkvg = MAIN_COLS - 3 * CONV_DIM
    n_rest = qkvg // (3 * tn)
    last_conv = n_conv - 1

    def w_spec(group):
        def index(i, j):
            conv_tile = group * n_conv + j
            rest_tile = 3 * n_conv + 3 * (j - n_conv) + group
            return (layer, jnp.where(j < n_conv, conv_tile, rest_tile), 0)
        return pl.BlockSpec((None, tn, D_MODEL), index)

    conv_col = lambda i, j: jnp.minimum(j, last_conv)
    rest_col = lambda i, j: jnp.maximum(j - n_conv, 0)
    x_spec = pl.BlockSpec(memory_space=pl.ANY)
    xs_spec = pl.BlockSpec((ns, D_MODEL), lambda i, j: (i, 0))
    sconv_spec = pl.BlockSpec((None, ns, CONV_WIDTH - 1, tn), lambda i, j: (layer, i, 0, conv_col(i, j)))
    g_spec = pl.BlockSpec((None, 1, D_MODEL), lambda i, j: (layer, 0, 0))
    wgt_spec = pl.BlockSpec((None, GATE_RANK, D_MODEL), lambda i, j: (layer, MAIN_COLS // GATE_RANK, 0))
    cw_spec = pl.BlockSpec((None, CONV_WIDTH, tn), lambda i, j: (layer, 0, conv_col(i, j)))
    in_specs = [x_spec, xs_spec, sconv_spec, g_spec, w_spec(0), w_spec(1), w_spec(2), wgt_spec, cw_spec]
    args = [x, xs, state_conv, g, w_in_t, w_in_t, w_in_t, w_in_t, conv_w]
    aliases = _stacked_operand(in_specs, args, stacked, 7)
    out_specs = [pl.BlockSpec((seq, tn), lambda i, j: (i, conv_col(i, j))),
                 pl.BlockSpec((ns, tn), lambda i, j: (i, conv_col(i, j))),
                 pl.BlockSpec((seq, 3 * tn), lambda i, j: (i, rest_col(i, j))),
                 pl.BlockSpec((ns, 3 * tn), lambda i, j: (i, rest_col(i, j))),
                 pl.BlockSpec((seq, GATE_RANK), lambda i, j: (i, 0)),
                 pl.BlockSpec((ns, GATE_RANK), lambda i, j: (i, 0)),
                 pl.BlockSpec((None, CONV_WIDTH - 1, tn), lambda i, j: (i, 0, conv_col(i, j))),
                 sconv_spec]
    out_shape = [jax.ShapeDtypeStruct((m, CONV_DIM), BF16), jax.ShapeDtypeStruct((nb, CONV_DIM), BF16),
                 jax.ShapeDtypeStruct((m, qkvg), BF16), jax.ShapeDtypeStruct((nb, qkvg), F32),
                 jax.ShapeDtypeStruct((m, GATE_RANK), F32), jax.ShapeDtypeStruct((nb, GATE_RANK), F32),
                 jax.ShapeDtypeStruct((batch, CONV_WIDTH - 1, CONV_DIM), F32),
                 jax.ShapeDtypeStruct(state_conv.shape, F32)]
    return pl.pallas_call(
        functools.partial(_mixer_in_kernel, n_conv=n_conv), grid=(batch, n_conv + n_rest),
        in_specs=in_specs, out_specs=out_specs, out_shape=out_shape,
        scratch_shapes=[pltpu.VMEM((seq + ns, D_MODEL), BF16), pltpu.VMEM((seq, D_MODEL), F32),
                        pltpu.SemaphoreType.DMA((1,))],
        input_output_aliases=aliases, compiler_params=_params(2), name="mixer_in")(*args)


def _out_proj_kernel(ya_ref, yb_ref, wa_ref, wb_ref, x_ref, o_ref):
    acc = _dot(ya_ref[...], wa_ref[...].astype(BF16))
    acc = acc + _dot(yb_ref[...], wb_ref[...].astype(BF16))
    o_ref[...] = x_ref[...] + acc


def _out_proj(ya, yb, w_out, x, layer, tm, tn):
    m = x.shape[0]
    a_spec = pl.BlockSpec((tm, CONV_DIM), lambda j, i: (i, 0))
    w_mode = pl.Buffered(1) if tn == D_MODEL else None
    wa_spec = pl.BlockSpec((None, CONV_DIM, tn), lambda j, i: (layer, 0, j), pipeline_mode=w_mode)
    wb_spec = pl.BlockSpec((None, GLA_V_DIM, tn), lambda j, i: (layer, 1, j), pipeline_mode=w_mode)
    x_spec = pl.BlockSpec((tm, tn), lambda j, i: (i, j))
    return pl.pallas_call(
        _out_proj_kernel, grid=(D_MODEL // tn, m // tm), in_specs=[a_spec, a_spec, wa_spec, wb_spec, x_spec],
        out_specs=x_spec, out_shape=jax.ShapeDtypeStruct((m, D_MODEL), F32),
        compiler_params=_params(2), name="out_proj")(ya, yb, w_out, w_out, x)


def _ffn_kernel(x_hbm, xs_ref, sst_ref, g_ref, wu_ref, wv_ref, cw_ref, cb_ref, wd_ref, fg_ref, *rest,
                tiles_per_seq, final_norm):
    o_ref, os_ref, st_ref, sst_new_ref, xn_ref, prev_ref, carry_ref, xbuf_ref, sem_ref = rest[-9:]
    i = pl.program_id(0)
    f = pl.program_id(1)
    tm = xbuf_ref.shape[0]
    ns = xs_ref.shape[0]
    _prefetched_rows(x_hbm, xbuf_ref, sem_ref, i, f, pl.num_programs(0))

    @pl.when(f == 0)
    def _():
        x = xbuf_ref[...]
        xs = xs_ref[...]
        xn_ref[0:tm, :] = _rms(x, g_ref[...]).astype(BF16)
        xn_ref[tm:tm + ns, :] = _rms(xs, g_ref[...]).astype(BF16)
        o_ref[...] = x
        os_ref[...] = xs

    first = lax.rem(i, tiles_per_seq) == 0

    @pl.when(first)
    def _():
        prev_ref[...] = jnp.zeros(prev_ref.shape, F32)

    @pl.when(jnp.logical_not(first))
    def _():
        prev_ref[...] = carry_ref[f]

    xn = xn_ref[...]
    u_all = _dot(xn, wu_ref[...].astype(BF16))
    v_all = _dot(xn, wv_ref[...].astype(BF16))
    u = u_all[0:tm, :]
    u1, u2 = _delayed_rows(u, prev_ref[...])
    cu = u2 * cw_ref[0:1, :]
    cu = cu + u1 * cw_ref[1:2, :]
    cu = cu + u * cw_ref[2:3, :]
    carry_ref[f] = u[tm - 8:tm, :]
    st_ref[...] = u[tm - (CONV_WIDTH - 1):tm, :]
    cu_s = _conv_tap_step(u_all[tm:tm + ns, :], sst_ref, cw_ref, sst_new_ref)

    pre = jnp.concatenate([cu, cu_s], axis=0) + cb_ref[...]
    h = (pre * _sigmoid(pre) * v_all).astype(BF16)
    acc = _dot(h, wd_ref[...].astype(BF16))
    o_ref[...] += acc[0:tm, :]
    os_ref[...] += acc[tm:tm + ns, :]

    if final_norm:
        @pl.when(f == pl.num_programs(1) - 1)
        def _():
            o_ref[...] = _rms(o_ref[...], fg_ref[...])
            os_ref[...] = _rms(os_ref[...], fg_ref[...])


def _ffn(x, xs, state_s, g, w_up, conv_w, conv_b, w_down, final_g, layer, seq, tm, tf, final_norm, stacked):
    m = x.shape[0]
    n_tiles = m // tm
    ns = xs.shape[0] // n_tiles
    tiles_per_seq = seq // tm
    nf = D_FF // tf
    row_spec = pl.BlockSpec((tm, D_MODEL), lambda i, f: (i, 0))
    srow_spec = pl.BlockSpec((ns, D_MODEL), lambda i, f: (i, 0))
    sst_spec = pl.BlockSpec((None, ns, CONV_WIDTH - 1, tf), lambda i, f: (layer, i, 0, f))
    g_spec = pl.BlockSpec((None, 1, D_MODEL), lambda i, f: (layer, 0, 0))
    wu_spec = pl.BlockSpec((None, D_MODEL, tf), lambda i, f: (layer, 0, f))
    wv_spec = pl.BlockSpec((None, D_MODEL, tf), lambda i, f: (layer, 0, nf + f))
    cw_spec = pl.BlockSpec((None, CONV_WIDTH, tf), lambda i, f: (layer, 0, f))
    cb_spec = pl.BlockSpec((None, 1, tf), lambda i, f: (layer, 0, f))
    wd_spec = pl.BlockSpec((None, tf, D_MODEL), lambda i, f: (layer, f, 0))
    fg_spec = pl.BlockSpec((1, D_MODEL), lambda i, f: (0, 0))
    st_spec = pl.BlockSpec((None, CONV_WIDTH - 1, tf), lambda i, f: (i, 0, f))
    x_spec = pl.BlockSpec(memory_space=pl.ANY)
    in_specs = [x_spec, srow_spec, sst_spec, g_spec, wu_spec, wv_spec, cw_spec, cb_spec, wd_spec, fg_spec]
    args = [x, xs, state_s, g, w_up, w_up, conv_w, conv_b, w_down, final_g]
    aliases = _stacked_operand(in_specs, args, stacked, 3)
    body = functools.partial(_ffn_kernel, tiles_per_seq=tiles_per_seq, final_norm=final_norm)
    out, out_s, tails, state_new = pl.pallas_call(
        body, grid=(n_tiles, nf), in_specs=in_specs,
        out_specs=[row_spec, srow_spec, st_spec, sst_spec],
        out_shape=[jax.ShapeDtypeStruct((m, D_MODEL), F32),
                   jax.ShapeDtypeStruct(xs.shape, F32),
                   jax.ShapeDtypeStruct((n_tiles, CONV_WIDTH - 1, D_FF), F32),
                   jax.ShapeDtypeStruct(state_s.shape, F32)],
        scratch_shapes=[pltpu.VMEM((tm + ns, D_MODEL), BF16), pltpu.VMEM((8, tf), F32),
                        pltpu.VMEM((nf, 8, tf), F32), pltpu.VMEM((tm, D_MODEL), F32),
                        pltpu.SemaphoreType.DMA((1,))],
        input_output_aliases=aliases, compiler_params=_params(2), name="ffn")(*args)
    return out, out_s, tails[tiles_per_seq - 1::tiles_per_seq], state_new


def _ref_rows(s, chunk):
    rows = np.arange(chunk)
    return (rows // (2 * s)) * (2 * s) + s - 1


def _gla_constants(chunk):
    rows = np.arange(chunk)
    tri = (rows[:, None] >= rows[None, :]).astype(np.float32)
    sgns = [np.where(rows > _ref_rows(s, chunk), 1.0, -1.0) * np.log2(np.e) for s in GLA_LEVELS]
    sgn = np.broadcast_to(np.concatenate(sgns)[:, None], (len(GLA_LEVELS) * chunk, GLA_DK))
    return jnp.asarray(tri, BF16), jnp.asarray(sgn, F32)


def _gla_log_decay(glr, w2, gb):
    gate = _dot(glr.astype(BF16), w2.astype(BF16)) + gb
    return _log_sigmoid(gate) * (1.0 / GATE_TAU)


def _gla_decay_kernel(glr_ref, w2_ref, gb_ref, tri_ref, b_ref):
    chunk = GLA_CHUNK
    g_all = _gla_log_decay(glr_ref[...], w2_ref[...], gb_ref[...])
    tri = tri_ref[...]
    pos = lax.broadcasted_iota(jnp.int32, (chunk, GLA_QK_DIM), 0) & 3
    for c in range(glr_ref.shape[0] // chunk):
        rows = slice(c * chunk, (c + 1) * chunk)
        g = g_all[rows, :]
        b = _exact_dot01(tri, g)
        g_prev = pltpu.roll(g, 1, axis=0)
        g_next = pltpu.roll(g, chunk - 1, axis=0)
        ref1 = b - jnp.where((pos & 1) == 1, g, 0.0)
        ref2 = b + jnp.where(pos == 0, g_next, 0.0) - jnp.where(pos >= 2, g, 0.0) - jnp.where(pos == 3, g_prev, 0.0)
        for part, val in enumerate((b, ref1, ref2)):
            b_ref[rows, part * GLA_QK_DIM:(part + 1) * GLA_QK_DIM] = val


def _gla_decay(glr, gate_w2, gate_b, tri, layer, tg):
    m = glr.shape[0]
    width = (1 + len(GLA_FINE_LEVELS)) * GLA_QK_DIM
    return pl.pallas_call(
        _gla_decay_kernel, grid=(m // tg,),
        in_specs=[pl.BlockSpec((tg, GATE_RANK), lambda i: (i, 0)),
                  pl.BlockSpec((None, GATE_RANK, GLA_QK_DIM), lambda i: (layer, 0, 0)),
                  pl.BlockSpec((None, 1, GLA_QK_DIM), lambda i: (layer, 0, 0)),
                  pl.BlockSpec(tri.shape, lambda i: (0, 0))],
        out_specs=pl.BlockSpec((tg, width), lambda i: (i, 0)),
        out_shape=jax.ShapeDtypeStruct((m, width), F32),
        compiler_params=_params(1), name="gla_decay")(glr, gate_w2, gate_b, tri)


def _gla_out(o, gn, og, eps=EPS):
    var = jnp.mean(o * o, axis=-1, keepdims=True)
    return o * lax.rsqrt(var + eps) * gn * (og * _sigmoid(og))


def _gla_prompt_kernel(q_ref, k_ref, v_ref, og_ref, b_ref, bf0_ref, bf1_ref, bh0_ref, bh1_ref, bh2_ref, bh3_ref,
                       gn_ref, sgn_ref, yb_ref, sfin_ref, s_ref):
    bf_refs = (bf0_ref, bf1_ref)
    bh_refs = (bh0_ref, bh1_ref, bh2_ref, bh3_ref)
    c_idx = pl.program_id(1)
    n_seq, chunk = q_ref.shape[0], q_ref.shape[1]

    @pl.when(c_idx == 0)
    def _():
        s_ref[...] = jnp.zeros(s_ref.shape, F32)

    row = lax.broadcasted_iota(jnp.int32, (chunk, chunk), 0)
    col = lax.broadcasted_iota(jnp.int32, (chunk, chunk), 1)
    diff_bits = row ^ col

    for n, h in [(n, h) for n in range(n_seq) for h in range(GLA_HEADS)]:
        ks = slice(h * GLA_DK, (h + 1) * GLA_DK)
        vs = slice(h * GLA_DV, (h + 1) * GLA_DV)
        b = b_ref[n, :, ks]
        b_last = b_ref[n, chunk - 1:chunk, ks]
        q = q_ref[n, :, ks]
        k = k_ref[n, :, ks]
        v = v_ref[n, :, vs]
        s_prev = s_ref[n, h]

        inter = _dot(q * jnp.exp(b).astype(BF16), s_prev.astype(BF16))

        acc = _dot_nt(q, k)
        for li, s in enumerate(GLA_LEVELS):
            if s in GLA_FINE_LEVELS:
                beta = bf_refs[GLA_FINE_LEVELS.index(s)][n, :, ks]
            else:
                beta = jnp.concatenate(
                    [bh_refs[h][n, pl.ds(int(r), 8, stride=0), :]
                     for r in _ref_rows(s, chunk)[::2 * s] for _ in range(2 * s // 8)], axis=0)
            e = jnp.exp2((b - beta) * sgn_ref[li * chunk:(li + 1) * chunk, :]).astype(BF16)
            p = _dot_nt(q * e, k * e)
            acc = jnp.where(diff_bits >= s, p, acc)
        a = jnp.where(row >= col, acc, 0.0)
        o = inter + _dot(a.astype(BF16), v)

        kdec_t = (k.astype(F32) * jnp.exp(b_last - b)).T
        decay_t = jnp.broadcast_to(jnp.exp(b_last), (GLA_DK, GLA_DK)).T
        s_new = jnp.concatenate([decay_t, decay_t], axis=1) * s_prev + _dot(kdec_t.astype(BF16), v)
        s_ref[n, h] = s_new

        yb_ref[n, :, vs] = _gla_out(o, gn_ref[:, vs], og_ref[n, :, vs].astype(F32), EPS * GLA_DK).astype(BF16)

    @pl.when(c_idx == pl.num_programs(1) - 1)
    def _():
        sfin_ref[...] = s_ref[...]


def _gla_prompt(qkvg, glr, gate_w2, gate_b, gla_norm_g, layer, batch, seq, n_seq):
    chunk = GLA_CHUNK
    nc = seq // chunk
    tri, sgn = _gla_constants(chunk)
    b_all = _gla_decay(glr, gate_w2, gate_b, tri, layer, GLA_DECAY_ROWS).reshape(batch, seq, -1)
    qkvg = qkvg.reshape(batch, seq, -1)

    def col_spec(width, blk):
        return pl.BlockSpec((n_seq, chunk, width), lambda b, c: (b, c, blk))

    gn_spec = pl.BlockSpec((None, 1, GLA_V_DIM), lambda b, c: (layer, 0, 0))
    sgn_spec = pl.BlockSpec(sgn.shape, lambda b, c: (0, 0))
    yb_spec = pl.BlockSpec((n_seq, chunk, GLA_V_DIM), lambda b, c: (b, c, 0))
    s_spec = pl.BlockSpec((n_seq, GLA_HEADS, GLA_DK, GLA_DV), lambda b, c: (b, 0, 0, 0))
    yb, s_fin = pl.pallas_call(
        _gla_prompt_kernel, grid=(batch // n_seq, nc),
        in_specs=[col_spec(GLA_QK_DIM, 0), col_spec(GLA_QK_DIM, 1), col_spec(GLA_V_DIM, 1),
                  col_spec(GLA_V_DIM, 2)]
                 + [col_spec(GLA_QK_DIM, part) for part in range(1 + len(GLA_FINE_LEVELS))]
                 + [col_spec(GLA_DK, h) for h in range(GLA_HEADS)] + [gn_spec, sgn_spec],
        out_specs=[yb_spec, s_spec],
        out_shape=[jax.ShapeDtypeStruct((batch, seq, GLA_V_DIM), BF16),
                   jax.ShapeDtypeStruct((batch, GLA_HEADS, GLA_DK, GLA_DV), F32)],
        scratch_shapes=[pltpu.VMEM((n_seq, GLA_HEADS, GLA_DK, GLA_DV), F32)],
        compiler_params=_params(2), name="gla_prompt")(
            qkvg, qkvg, qkvg, qkvg, *([b_all] * (1 + len(GLA_FINE_LEVELS) + GLA_HEADS)), gla_norm_g, sgn)
    return yb.reshape(batch * seq, GLA_V_DIM), s_fin


def _columns(slab):
    pad = jnp.zeros((GLA_DK - GLA_STEP_ROWS, GLA_DK), F32)
    return jnp.concatenate([slab, pad], axis=0).T


def _gla_step_kernel(q_ref, k_ref, v_ref, og_ref, glr_ref, w2_ref, gb_ref, gn_ref, s_ref, *rest):
    yb_ref, snew_ref = rest[-2:]
    rows = GLA_STEP_ROWS
    decay_all = jnp.exp(_gla_log_decay(glr_ref[...], w2_ref[...], gb_ref[...]))
    row_id = lax.broadcasted_iota(jnp.int32, (rows, rows * GLA_DV), 0)
    lane_blk = lax.broadcasted_iota(jnp.int32, (rows, rows * GLA_DV), 1) // GLA_DV

    for h in range(GLA_HEADS):
        ks = slice(h * GLA_DK, (h + 1) * GLA_DK)
        vs = slice(h * GLA_DV, (h + 1) * GLA_DV)
        decay = decay_all[:, ks]
        q = q_ref[:, ks] * (GLA_DK ** -0.5)
        k = k_ref[:, ks]
        v = v_ref[:, vs]
        qk = jnp.sum(q * k, axis=-1, keepdims=True)

        v_diag = jnp.where(row_id == lane_blk, jnp.concatenate([v] * rows, axis=1), 0.0)
        v_diag = jnp.concatenate([v_diag, jnp.zeros_like(v_diag)], axis=0)
        k_t = _columns(k)[:, 0:2 * rows]
        kv = _dot(k_t.astype(BF16), v_diag.astype(BF16))

        decay_c = _columns(decay)
        qd = (q * decay).astype(BF16)
        inter_rows = []
        for i in range(rows):
            s_prev = s_ref[i, h]
            inter_rows.append(_dot(qd, s_prev.astype(BF16))[i:i + 1, :])
            snew_ref[i, h] = decay_c[:, i:i + 1] * s_prev + kv[:, i * GLA_DV:(i + 1) * GLA_DV]
        o = jnp.concatenate(inter_rows, axis=0) + qk * v
        yb_ref[:, vs] = _gla_out(o, gn_ref[:, vs], og_ref[:, vs]).astype(BF16)


def _gla_step(qkvg, glr, state, gate_w2, gate_b, gla_norm_g, layer, stacked):
    nb = qkvg.shape[0]
    rows = GLA_STEP_ROWS
    s_spec = pl.BlockSpec((None, rows, GLA_HEADS, GLA_DK, GLA_DV), lambda t: (layer, t, 0, 0, 0))
    in_specs = [pl.BlockSpec((rows, GLA_QK_DIM), lambda t: (t, 0)),
                pl.BlockSpec((rows, GLA_QK_DIM), lambda t: (t, 1)),
                pl.BlockSpec((rows, GLA_V_DIM), lambda t: (t, 1)),
                pl.BlockSpec((rows, GLA_V_DIM), lambda t: (t, 2)),
                pl.BlockSpec((rows, GATE_RANK), lambda t: (t, 0)),
                pl.BlockSpec((None, GATE_RANK, GLA_QK_DIM), lambda t: (layer, 0, 0)),
                pl.BlockSpec((None, 1, GLA_QK_DIM), lambda t: (layer, 0, 0)),
                pl.BlockSpec((None, 1, GLA_V_DIM), lambda t: (layer, 0, 0)),
                s_spec]
    args = [qkvg, qkvg, qkvg, qkvg, glr, gate_w2, gate_b, gla_norm_g, state]
    aliases = _stacked_operand(in_specs, args, stacked, 1)
    return pl.pallas_call(
        _gla_step_kernel, grid=(nb // rows,), in_specs=in_specs,
        out_specs=[pl.BlockSpec((rows, GLA_V_DIM), lambda t: (t, 0)), s_spec],
        out_shape=[jax.ShapeDtypeStruct((nb, GLA_V_DIM), BF16),
                   jax.ShapeDtypeStruct(state.shape, F32)],
        input_output_aliases=aliases,
        compiler_params=_params(1), name="gla_step")(*args)


def kernel(x_prompt, x_sample, state_conv, state_gla, state_ffn_conv, norm_mix_g, w_in, conv_w, gate_w2, gate_b, gla_norm_g, w_out, norm_ffn_g, w_up, ffn_conv_w, ffn_conv_b, w_down, final_norm_g):
    batch, seq, _ = x_prompt.shape
    nb = x_sample.shape[0]
    assert seq % FFN_TM == 0 and seq % GLA_DECAY_ROWS == 0 and nb % GLA_STEP_ROWS == 0
    assert nb % (16 * batch * (seq // FFN_TM)) == 0 and x_sample.shape[1] == 1
    norm_mix_g, norm_ffn_g = norm_mix_g[:, None, :], norm_ffn_g[:, None, :]
    gate_b, gla_norm_g, ffn_conv_b = gate_b[:, None, :], gla_norm_g[:, None, :], ffn_conv_b[:, None, :]
    final_norm_g = final_norm_g[None, :]
    w_in_t = jnp.swapaxes(w_in, 1, 2)

    xp = x_prompt.reshape(batch * seq, D_MODEL)
    xs = x_sample.reshape(nb, D_MODEL)
    conv_p, gla_p, ffn_p = [], [], []
    conv_s, gla_s, ffn_s = None, None, None
    for l in range(DEPTH):
        ya, ya_s, qkvg, qkvg_s, glr, glr_s, c_new, conv_s = _mixer_in(
            xp, xs, state_conv, norm_mix_g, w_in_t, conv_w, l, seq, MIXER_IN_TN, conv_s)

        yb, s_new = _gla_prompt(qkvg, glr, gate_w2, gate_b, gla_norm_g, l, batch, seq, batch)
        xp = _out_proj(ya, yb, w_out, xp, l, OUT_PROJ_TM, D_MODEL)

        yb_s, gla_s = _gla_step(qkvg_s, glr_s, state_gla, gate_w2, gate_b, gla_norm_g, l, gla_s)
        xs = _out_proj(ya_s, yb_s, w_out, xs, l, nb, OUT_PROJ_TN_SAMPLE)

        xp, xs, f_new, ffn_s = _ffn(xp, xs, state_ffn_conv, norm_ffn_g, w_up, ffn_conv_w, ffn_conv_b, w_down,
                                    final_norm_g, l, seq, FFN_TM, FFN_TF, l == DEPTH - 1, ffn_s)
        conv_p.append(c_new)
        gla_p.append(s_new)
        ffn_p.append(f_new)

    return (xp.reshape(batch, seq, D_MODEL), xs.reshape(nb, 1, D_MODEL),
            jnp.stack(conv_p), jnp.stack(gla_p), jnp.stack(ffn_p), conv_s, gla_s, ffn_s)
```

```python
import functools

import numpy as np
import jax
import jax.numpy as jnp
from jax import lax
from jax.experimental import pallas as pl
from jax.experimental.pallas import tpu as pltpu

F32 = jnp.float32
BF16 = jnp.bfloat16

D_MODEL = 2048
DEPTH = 2
CONV_DIM = 1024
CONV_WIDTH = 3
GLA_HEADS = 4
GLA_QK_DIM = 512
GLA_V_DIM = 1024
GLA_DK = 128
GLA_DV = 256
GATE_RANK = 16
GATE_TAU = 16.0
D_FF = 5632
EPS = 1e-6
MAIN_COLS = 3 * CONV_DIM + 2 * GLA_QK_DIM + 2 * GLA_V_DIM

GLA_CHUNK = 128
GLA_LEVELS = tuple(1 << i for i in range(GLA_CHUNK.bit_length() - 1))
GLA_FINE_LEVELS = tuple(s for s in GLA_LEVELS if s < 4)
GLA_STEP_ROWS = 16
GLA_DECAY_ROWS = 8 * GLA_CHUNK

MIXER_IN_TN = 256
OUT_PROJ_TM = 512
FFN_TM, FFN_TF = 1024, 512

VMEM_BYTES_V7X = 64 * 1024 * 1024
VMEM_LIMIT = VMEM_BYTES_V7X - 2 * 1024 * 1024


def _params(n_axes):
    return pltpu.CompilerParams(dimension_semantics=("arbitrary",) * n_axes,
                                vmem_limit_bytes=VMEM_LIMIT)


def _dot(a, b):
    return jnp.dot(a, b, preferred_element_type=F32)


def _dot_nt(a, b):
    return lax.dot_general(a, b, (((1,), (1,)), ((), ())), preferred_element_type=F32)


def _split3(x):
    hi = x.astype(BF16)
    r1 = x - hi.astype(F32)
    mid = r1.astype(BF16)
    lo = (r1 - mid.astype(F32)).astype(BF16)
    return hi, mid, lo


def _exact_dot01(m01, x):
    x1, x2, x3 = _split3(x)
    return _dot(m01, x1) + _dot(m01, x2) + _dot(m01, x3)


def _sigmoid(x):
    return 1.0 / (1.0 + jnp.exp(-x))


def _log_sigmoid(x):
    return jnp.minimum(x, 0.0) - jnp.log1p(jnp.exp(-jnp.abs(x)))


def _rms(x, g):
    var = jnp.mean(x * x, axis=-1, keepdims=True)
    return x * lax.rsqrt(var + EPS) * g


def _delayed_rows(u, prev):
    sub = lax.broadcasted_iota(jnp.int32, prev.shape, 0)
    r1 = pltpu.roll(u, 1, axis=0)
    r2 = pltpu.roll(u, 2, axis=0)
    head1 = jnp.where(sub < 1, pltpu.roll(prev, 1, axis=0), r1[0:8, :])
    head2 = jnp.where(sub < 2, pltpu.roll(prev, 2, axis=0), r2[0:8, :])
    return (jnp.concatenate([head1, r1[8:, :]], axis=0), jnp.concatenate([head2, r2[8:, :]], axis=0))


def _prefetched_rows(x_hbm, xbuf_ref, sem_ref, tile, step, n_tiles):
    rows = xbuf_ref.shape[0]

    def copy(t):
        return pltpu.make_async_copy(x_hbm.at[pl.ds(t * rows, rows), :], xbuf_ref, sem_ref.at[0])

    @pl.when(jnp.logical_and(tile == 0, step == 0))
    def _():
        copy(0).start()

    @pl.when(step == 0)
    def _():
        copy(tile).wait()

    @pl.when(jnp.logical_and(step == 1, tile + 1 < n_tiles))
    def _():
        copy(tile + 1).start()


def _stacked_operand(in_specs, args, stacked, out_index):
    if stacked is None:
        return {}
    in_specs.append(pl.BlockSpec(memory_space=pl.ANY))
    args.append(stacked)
    return {len(args) - 1: out_index}


def _conv_tap_step(new_row, st_ref, w_ref, new_st_ref):
    p1 = st_ref[:, 1, :]
    cu = st_ref[:, 0, :] * w_ref[0:1, :]
    cu = cu + p1 * w_ref[1:2, :]
    cu = cu + new_row * w_ref[2:3, :]
    new_st_ref[:, 0, :] = p1
    new_st_ref[:, 1, :] = new_row
    return cu


def _mixer_in_kernel(x_hbm, xs_ref, sconv_ref, g_ref, wa_ref, wb_ref, wc_ref, wgt_ref, cw_ref, *rest, n_conv):
    (ya_ref, yas_ref, o_ref, os_ref, og_ref, ogs_ref, st_ref, sconv_new_ref,
     xn_ref, xbuf_ref, sem_ref) = rest[-11:]
    j = pl.program_id(1)
    tm = xbuf_ref.shape[0]
    ns = xs_ref.shape[0]
    tn = wa_ref.shape[0]
    _prefetched_rows(x_hbm, xbuf_ref, sem_ref, pl.program_id(0), j, pl.num_programs(0))

    @pl.when(j == 0)
    def _():
        xn_ref[0:tm, :] = _rms(xbuf_ref[...], g_ref[...]).astype(BF16)
        xn_ref[tm:tm + ns, :] = _rms(xs_ref[...], g_ref[...]).astype(BF16)
        gate = _dot_nt(xn_ref[...], wgt_ref[...].astype(BF16))
        og_ref[...] = gate[0:tm, :]
        ogs_ref[...] = gate[tm:tm + ns, :]

    def projected():
        xn = xn_ref[...]
        return [_dot_nt(xn, w_ref[...].astype(BF16)) for w_ref in (wa_ref, wb_ref, wc_ref)]

    @pl.when(j < n_conv)
    def _():
        bg, cg, hin = projected()
        z = cg[0:tm, :] * hin[0:tm, :]
        z1, z2 = _delayed_rows(z, jnp.zeros((8, tn), F32))
        cu = z2 * cw_ref[0:1, :]
        cu = cu + z1 * cw_ref[1:2, :]
        cu = cu + z * cw_ref[2:3, :]
        ya_ref[...] = (bg[0:tm, :] * cu).astype(BF16)
        st_ref[...] = z[tm - (CONV_WIDTH - 1):tm, :]
        cu_s = _conv_tap_step(cg[tm:tm + ns, :] * hin[tm:tm + ns, :], sconv_ref, cw_ref, sconv_new_ref)
        yas_ref[...] = (bg[tm:tm + ns, :] * cu_s).astype(BF16)

    @pl.when(j >= n_conv)
    def _():
        for idx, tile in enumerate(projected()):
            cols = slice(idx * tn, (idx + 1) * tn)
            o_ref[:, cols] = tile[0:tm, :].astype(BF16)
            os_ref[:, cols] = tile[tm:tm + ns, :]


def _mixer_in(x, xs, state_conv, g, w_in_t, conv_w, layer, seq, tn, stacked):
    m, nb = x.shape[0], xs.shape[0]
    batch = m // seq
    ns = nb // batch
    n_conv = CONV_DIM // tn
    qkvg = MAIN_COLS - 3 * CONV_DIM
    n_rest = qkvg // (3 * tn)
    last_conv = n_conv - 1

    def w_spec(group):
        def index(i, j):
            conv_tile = group * n_conv + j
            rest_tile = 3 * n_conv + 3 * (j - n_conv) + group
            return (layer, jnp.where(j < n_conv, conv_tile, rest_tile), 0)
        return pl.BlockSpec((None, tn, D_MODEL), index)

    conv_col = lambda i, j: jnp.minimum(j, last_conv)
    rest_col = lambda i, j: jnp.maximum(j - n_conv, 0)
    x_spec = pl.BlockSpec(memory_space=pl.ANY)
    xs_spec = pl.BlockSpec((ns, D_MODEL), lambda i, j: (i, 0))
    sconv_spec = pl.BlockSpec((None, ns, CONV_WIDTH - 1, tn), lambda i, j: (layer, i, 0, conv_col(i, j)))
    g_spec = pl.BlockSpec((None, 1, D_MODEL), lambda i, j: (layer, 0, 0))
    wgt_spec = pl.BlockSpec((None, GATE_RANK, D_MODEL), lambda i, j: (layer, MAIN_COLS // GATE_RANK, 0))
    cw_spec = pl.BlockSpec((None, CONV_WIDTH, tn), lambda i, j: (layer, 0, conv_col(i, j)))
    in_specs = [x_spec, xs_spec, sconv_spec, g_spec, w_spec(0), w_spec(1), w_spec(2), wgt_spec, cw_spec]
    args = [x, xs, state_conv, g, w_in_t, w_in_t, w_in_t, w_in_t, conv_w]
    aliases = _stacked_operand(in_specs, args, stacked, 7)
    out_specs = [pl.BlockSpec((seq, tn), lambda i, j: (i, conv_col(i, j))),
                 pl.BlockSpec((ns, tn), lambda i, j: (i, conv_col(i, j))),
                 pl.BlockSpec((seq, 3 * tn), lambda i, j: (i, rest_col(i, j))),
                 pl.BlockSpec((ns, 3 * tn), lambda i, j: (i, rest_col(i, j))),
                 pl.BlockSpec((seq, GATE_RANK), lambda i, j: (i, 0)),
                 pl.BlockSpec((ns, GATE_RANK), lambda i, j: (i, 0)),
                 pl.BlockSpec((None, CONV_WIDTH - 1, tn), lambda i, j: (i, 0, conv_col(i, j))),
                 sconv_spec]
    out_shape = [jax.ShapeDtypeStruct((m, CONV_DIM), BF16), jax.ShapeDtypeStruct((nb, CONV_DIM), BF16),
                 jax.ShapeDtypeStruct((m, qkvg), BF16), jax.ShapeDtypeStruct((nb, qkvg), F32),
                 jax.ShapeDtypeStruct((m, GATE_RANK), F32), jax.ShapeDtypeStruct((nb, GATE_RANK), F32),
                 jax.ShapeDtypeStruct((batch, CONV_WIDTH - 1, CONV_DIM), F32),
                 jax.ShapeDtypeStruct(state_conv.shape, F32)]
    return pl.pallas_call(
        functools.partial(_mixer_in_kernel, n_conv=n_conv), grid=(batch, n_conv + n_rest),
        in_specs=in_specs, out_specs=out_specs, out_shape=out_shape,
        scratch_shapes=[pltpu.VMEM((seq + ns, D_MODEL), BF16), pltpu.VMEM((seq, D_MODEL), F32),
                        pltpu.SemaphoreType.DMA((1,))],
        input_output_aliases=aliases, compiler_params=_params(2), name="mixer_in")(*args)


def _out_proj_kernel(ya_ref, yb_ref, wa_ref, wb_ref, x_ref, o_ref):
    acc = _dot(ya_ref[...], wa_ref[...].astype(BF16))
    acc = acc + _dot(yb_ref[...], wb_ref[...].astype(BF16))
    o_ref[...] = x_ref[...] + acc


def _out_proj(ya, yb, w_out, x, layer, tm):
    m = x.shape[0]
    a_spec = pl.BlockSpec((tm, CONV_DIM), lambda i: (i, 0))
    wa_spec = pl.BlockSpec((None, CONV_DIM, D_MODEL), lambda i: (layer, 0, 0), pipeline_mode=pl.Buffered(1))
    wb_spec = pl.BlockSpec((None, GLA_V_DIM, D_MODEL), lambda i: (layer, 1, 0), pipeline_mode=pl.Buffered(1))
    x_spec = pl.BlockSpec((tm, D_MODEL), lambda i: (i, 0))
    return pl.pallas_call(
        _out_proj_kernel, grid=(m // tm,), in_specs=[a_spec, a_spec, wa_spec, wb_spec, x_spec],
        out_specs=x_spec, out_shape=jax.ShapeDtypeStruct((m, D_MODEL), F32),
        compiler_params=_params(1), name="out_proj")(ya, yb, w_out, w_out, x)


def _ffn_kernel(x_hbm, xs_ref, sst_ref, g_ref, wu_ref, wv_ref, cw_ref, cb_ref, wd_ref, fg_ref, *rest,
                tiles_per_seq, final_norm):
    o_ref, os_ref, st_ref, sst_new_ref, xn_ref, prev_ref, carry_ref, xbuf_ref, sem_ref = rest[-9:]
    i = pl.program_id(0)
    f = pl.program_id(1)
    tm = xbuf_ref.shape[0]
    ns = xs_ref.shape[0]
    _prefetched_rows(x_hbm, xbuf_ref, sem_ref, i, f, pl.num_programs(0))

    @pl.when(f == 0)
    def _():
        x = xbuf_ref[...]
        xs = xs_ref[...]
        xn_ref[0:tm, :] = _rms(x, g_ref[...]).astype(BF16)
        xn_ref[tm:tm + ns, :] = _rms(xs, g_ref[...]).astype(BF16)
        o_ref[...] = x
        os_ref[...] = xs

    first = lax.rem(i, tiles_per_seq) == 0

    @pl.when(first)
    def _():
        prev_ref[...] = jnp.zeros(prev_ref.shape, F32)

    @pl.when(jnp.logical_not(first))
    def _():
        prev_ref[...] = carry_ref[f]

    xn = xn_ref[...]
    u_all = _dot(xn, wu_ref[...].astype(BF16))
    v_all = _dot(xn, wv_ref[...].astype(BF16))
    u = u_all[0:tm, :]
    u1, u2 = _delayed_rows(u, prev_ref[...])
    cu = u2 * cw_ref[0:1, :]
    cu = cu + u1 * cw_ref[1:2, :]
    cu = cu + u * cw_ref[2:3, :]
    carry_ref[f] = u[tm - 8:tm, :]
    st_ref[...] = u[tm - (CONV_WIDTH - 1):tm, :]
    cu_s = _conv_tap_step(u_all[tm:tm + ns, :], sst_ref, cw_ref, sst_new_ref)

    pre = jnp.concatenate([cu, cu_s], axis=0) + cb_ref[...]
    h = (pre * _sigmoid(pre) * v_all).astype(BF16)
    acc = _dot(h, wd_ref[...].astype(BF16))
    o_ref[...] += acc[0:tm, :]
    os_ref[...] += acc[tm:tm + ns, :]

    if final_norm:
        @pl.when(f == pl.num_programs(1) - 1)
        def _():
            o_ref[...] = _rms(o_ref[...], fg_ref[...])
            os_ref[...] = _rms(os_ref[...], fg_ref[...])


def _ffn(x, xs, state_s, g, w_up, conv_w, conv_b, w_down, final_g, layer, seq, tm, tf, final_norm, stacked):
    m = x.shape[0]
    n_tiles = m // tm
    ns = xs.shape[0] // n_tiles
    tiles_per_seq = seq // tm
    nf = D_FF // tf
    row_spec = pl.BlockSpec((tm, D_MODEL), lambda i, f: (i, 0))
    srow_spec = pl.BlockSpec((ns, D_MODEL), lambda i, f: (i, 0))
    sst_spec = pl.BlockSpec((None, ns, CONV_WIDTH - 1, tf), lambda i, f: (layer, i, 0, f))
    g_spec = pl.BlockSpec((None, 1, D_MODEL), lambda i, f: (layer, 0, 0))
    wu_spec = pl.BlockSpec((None, D_MODEL, tf), lambda i, f: (layer, 0, f))
    wv_spec = pl.BlockSpec((None, D_MODEL, tf), lambda i, f: (layer, 0, nf + f))
    cw_spec = pl.BlockSpec((None, CONV_WIDTH, tf), lambda i, f: (layer, 0, f))
    cb_spec = pl.BlockSpec((None, 1, tf), lambda i, f: (layer, 0, f))
    wd_spec = pl.BlockSpec((None, tf, D_MODEL), lambda i, f: (layer, f, 0))
    fg_spec = pl.BlockSpec((1, D_MODEL), lambda i, f: (0, 0))
    st_spec = pl.BlockSpec((None, CONV_WIDTH - 1, tf), lambda i, f: (i, 0, f))
    x_spec = pl.BlockSpec(memory_space=pl.ANY)
    in_specs = [x_spec, srow_spec, sst_spec, g_spec, wu_spec, wv_spec, cw_spec, cb_spec, wd_spec, fg_spec]
    args = [x, xs, state_s, g, w_up, w_up, conv_w, conv_b, w_down, final_g]
    aliases = _stacked_operand(in_specs, args, stacked, 3)
    body = functools.partial(_ffn_kernel, tiles_per_seq=tiles_per_seq, final_norm=final_norm)
    out, out_s, tails, state_new = pl.pallas_call(
        body, grid=(n_tiles, nf), in_specs=in_specs,
        out_specs=[row_spec, srow_spec, st_spec, sst_spec],
        out_shape=[jax.ShapeDtypeStruct((m, D_MODEL), F32),
                   jax.ShapeDtypeStruct(xs.shape, F32),
                   jax.ShapeDtypeStruct((n_tiles, CONV_WIDTH - 1, D_FF), F32),
                   jax.ShapeDtypeStruct(state_s.shape, F32)],
        scratch_shapes=[pltpu.VMEM((tm + ns, D_MODEL), BF16), pltpu.VMEM((8, tf), F32),
                        pltpu.VMEM((nf, 8, tf), F32), pltpu.VMEM((tm, D_MODEL), F32),
                        pltpu.SemaphoreType.DMA((1,))],
        input_output_aliases=aliases, compiler_params=_params(2), name="ffn")(*args)
    return out, out_s, tails[tiles_per_seq - 1::tiles_per_seq], state_new


def _ref_rows(s, chunk):
    rows = np.arange(chunk)
    return (rows // (2 * s)) * (2 * s) + s - 1


def _gla_constants(chunk):
    rows = np.arange(chunk)
    tri = (rows[:, None] >= rows[None, :]).astype(np.float32)
    cums = [tri] + [tri[_ref_rows(s, chunk)] for s in GLA_FINE_LEVELS]
    sgns = [np.where(rows > _ref_rows(s, chunk), 1.0, -1.0) * np.log2(np.e) for s in GLA_LEVELS]
    sgn = np.broadcast_to(np.concatenate(sgns)[:, None], (len(GLA_LEVELS) * chunk, GLA_DK))
    return jnp.asarray(np.concatenate(cums, axis=0), BF16), jnp.asarray(sgn, F32)


def _gla_log_decay(glr, w2, gb):
    gate = _dot(glr.astype(BF16), w2.astype(BF16)) + gb
    return _log_sigmoid(gate) * (1.0 / GATE_TAU)


def _gla_decay_kernel(glr_ref, w2_ref, gb_ref, cum_ref, b_ref):
    chunk = GLA_CHUNK
    g = _gla_log_decay(glr_ref[...], w2_ref[...], gb_ref[...])
    cum = cum_ref[...]
    for c in range(glr_ref.shape[0] // chunk):
        rows = slice(c * chunk, (c + 1) * chunk)
        g1, g2, g3 = _split3(g[rows, :])
        cums = _dot(cum, g1) + _dot(cum, g2)
        cums = jnp.concatenate([cums[0:chunk, :] + _dot(cum[0:chunk, :], g3), cums[chunk:, :]], axis=0)
        for part in range(1 + len(GLA_FINE_LEVELS)):
            b_ref[rows, part * GLA_QK_DIM:(part + 1) * GLA_QK_DIM] = cums[part * chunk:(part + 1) * chunk, :]


def _gla_decay(glr, gate_w2, gate_b, cum, layer, tg):
    m = glr.shape[0]
    width = (1 + len(GLA_FINE_LEVELS)) * GLA_QK_DIM
    return pl.pallas_call(
        _gla_decay_kernel, grid=(m // tg,),
        in_specs=[pl.BlockSpec((tg, GATE_RANK), lambda i: (i, 0)),
                  pl.BlockSpec((None, GATE_RANK, GLA_QK_DIM), lambda i: (layer, 0, 0)),
                  pl.BlockSpec((None, 1, GLA_QK_DIM), lambda i: (layer, 0, 0)),
                  pl.BlockSpec(cum.shape, lambda i: (0, 0))],
        out_specs=pl.BlockSpec((tg, width), lambda i: (i, 0)),
        out_shape=jax.ShapeDtypeStruct((m, width), F32),
        compiler_params=_params(1), name="gla_decay")(glr, gate_w2, gate_b, cum)


def _gla_out(o, gn, og):
    var = jnp.mean(o * o, axis=-1, keepdims=True)
    return o * lax.rsqrt(var + EPS) * gn * (og * _sigmoid(og))


def _gla_prompt_kernel(q_ref, k_ref, v_ref, og_ref, b_ref, bf0_ref, bf1_ref, bh0_ref, bh1_ref, bh2_ref, bh3_ref,
                       gn_ref, sgn_ref, yb_ref, sfin_ref, s_ref):
    bf_refs = (bf0_ref, bf1_ref)
    bh_refs = (bh0_ref, bh1_ref, bh2_ref, bh3_ref)
    c_idx = pl.program_id(1)
    n_seq, chunk = q_ref.shape[0], q_ref.shape[1]

    @pl.when(c_idx == 0)
    def _():
        s_ref[...] = jnp.zeros(s_ref.shape, F32)

    row = lax.broadcasted_iota(jnp.int32, (chunk, chunk), 0)
    col = lax.broadcasted_iota(jnp.int32, (chunk, chunk), 1)
    diff_bits = row ^ col

    for n, h in [(n, h) for n in range(n_seq) for h in range(GLA_HEADS)]:
        ks = slice(h * GLA_DK, (h + 1) * GLA_DK)
        vs = slice(h * GLA_DV, (h + 1) * GLA_DV)
        b = b_ref[n, :, ks]
        b_last = b_ref[n, chunk - 1:chunk, ks]
        q = q_ref[n, :, ks]
        k = k_ref[n, :, ks]
        v = v_ref[n, :, vs]
        s_prev = s_ref[n, h]

        inter = _dot(q * jnp.exp(b).astype(BF16), s_prev.astype(BF16))

        acc = _dot_nt(q, k)
        for li, s in enumerate(GLA_LEVELS):
            if s in GLA_FINE_LEVELS:
                beta = bf_refs[GLA_FINE_LEVELS.index(s)][n, :, ks]
            else:
                beta = jnp.concatenate(
                    [bh_refs[h][n, pl.ds(int(r), 8, stride=0), :]
                     for r in _ref_rows(s, chunk)[::2 * s] for _ in range(2 * s // 8)], axis=0)
            e = jnp.exp2((b - beta) * sgn_ref[li * chunk:(li + 1) * chunk, :]).astype(BF16)
            p = _dot_nt(q * e, k * e)
            acc = jnp.where(diff_bits >= s, p, acc)
        a = jnp.where(row >= col, acc, 0.0)
        o = (inter + _dot(a.astype(BF16), v)) * (GLA_DK ** -0.5)

        kdec_t = (k.astype(F32) * jnp.exp(b_last - b)).T
        decay_t = jnp.broadcast_to(jnp.exp(b_last), (GLA_DK, GLA_DK)).T
        s_new = jnp.concatenate([decay_t, decay_t], axis=1) * s_prev + _dot(kdec_t.astype(BF16), v)
        s_ref[n, h] = s_new

        yb_ref[n, :, vs] = _gla_out(o, gn_ref[:, vs], og_ref[n, :, vs].astype(F32)).astype(BF16)

    @pl.when(c_idx == pl.num_programs(1) - 1)
    def _():
        sfin_ref[...] = s_ref[...]


def _gla_prompt(qkvg, glr, gate_w2, gate_b, gla_norm_g, layer, batch, seq, n_seq):
    chunk = GLA_CHUNK
    nc = seq // chunk
    cum, sgn = _gla_constants(chunk)
    b_all = _gla_decay(glr, gate_w2, gate_b, cum, layer, GLA_DECAY_ROWS).reshape(batch, seq, -1)
    qkvg = qkvg.reshape(batch, seq, -1)

    def col_spec(width, blk):
        return pl.BlockSpec((n_seq, chunk, width), lambda b, c: (b, c, blk))

    gn_spec = pl.BlockSpec((None, 1, GLA_V_DIM), lambda b, c: (layer, 0, 0))
    sgn_spec = pl.BlockSpec(sgn.shape, lambda b, c: (0, 0))
    yb_spec = pl.BlockSpec((n_seq, chunk, GLA_V_DIM), lambda b, c: (b, c, 0))
    s_spec = pl.BlockSpec((n_seq, GLA_HEADS, GLA_DK, GLA_DV), lambda b, c: (b, 0, 0, 0))
    yb, s_fin = pl.pallas_call(
        _gla_prompt_kernel, grid=(batch // n_seq, nc),
        in_specs=[col_spec(GLA_QK_DIM, 0), col_spec(GLA_QK_DIM, 1), col_spec(GLA_V_DIM, 1),
                  col_spec(GLA_V_DIM, 2)]
                 + [col_spec(GLA_QK_DIM, part) for part in range(1 + len(GLA_FINE_LEVELS))]
                 + [col_spec(GLA_DK, h) for h in range(GLA_HEADS)] + [gn_spec, sgn_spec],
        out_specs=[yb_spec, s_spec],
        out_shape=[jax.ShapeDtypeStruct((batch, seq, GLA_V_DIM), BF16),
                   jax.ShapeDtypeStruct((batch, GLA_HEADS, GLA_DK, GLA_DV), F32)],
        scratch_shapes=[pltpu.VMEM((n_seq, GLA_HEADS, GLA_DK, GLA_DV), F32)],
        compiler_params=_params(2), name="gla_prompt")(
            qkvg, qkvg, qkvg, qkvg, *([b_all] * (1 + len(GLA_FINE_LEVELS) + GLA_HEADS)), gla_norm_g, sgn)
    return yb.reshape(batch * seq, GLA_V_DIM), s_fin


def _columns(slab):
    pad = jnp.zeros((GLA_DK - GLA_STEP_ROWS, GLA_DK), F32)
    return jnp.concatenate([slab, pad], axis=0).T


def _gla_step_kernel(q_ref, k_ref, v_ref, og_ref, glr_ref, w2_ref, gb_ref, gn_ref, s_ref, *rest):
    yb_ref, snew_ref = rest[-2:]
    rows = GLA_STEP_ROWS
    decay_all = jnp.exp(_gla_log_decay(glr_ref[...], w2_ref[...], gb_ref[...]))
    row_id = lax.broadcasted_iota(jnp.int32, (rows, rows * GLA_DV), 0)
    lane_blk = lax.broadcasted_iota(jnp.int32, (rows, rows * GLA_DV), 1) // GLA_DV

    for h in range(GLA_HEADS):
        ks = slice(h * GLA_DK, (h + 1) * GLA_DK)
        vs = slice(h * GLA_DV, (h + 1) * GLA_DV)
        decay = decay_all[:, ks]
        q = q_ref[:, ks] * (GLA_DK ** -0.5)
        k = k_ref[:, ks]
        v = v_ref[:, vs]
        qk = jnp.sum(q * k, axis=-1, keepdims=True)

        v_diag = jnp.where(row_id == lane_blk, jnp.concatenate([v] * rows, axis=1), 0.0)
        v_diag = jnp.concatenate([v_diag, jnp.zeros_like(v_diag)], axis=0)
        k_t = _columns(k)[:, 0:2 * rows]
        kv = _dot(k_t.astype(BF16), v_diag.astype(BF16))

        decay_c = _columns(decay)
        qd = (q * decay).astype(BF16)
        inter_rows = []
        for i in range(rows):
            s_prev = s_ref[i, h]
            inter_rows.append(_dot(qd, s_prev.astype(BF16))[i:i + 1, :])
            snew_ref[i, h] = decay_c[:, i:i + 1] * s_prev + kv[:, i * GLA_DV:(i + 1) * GLA_DV]
        o = jnp.concatenate(inter_rows, axis=0) + qk * v
        yb_ref[:, vs] = _gla_out(o, gn_ref[:, vs], og_ref[:, vs]).astype(BF16)


def _gla_step(qkvg, glr, state, gate_w2, gate_b, gla_norm_g, layer, stacked):
    nb = qkvg.shape[0]
    rows = GLA_STEP_ROWS
    s_spec = pl.BlockSpec((None, rows, GLA_HEADS, GLA_DK, GLA_DV), lambda t: (layer, t, 0, 0, 0))
    in_specs = [pl.BlockSpec((rows, GLA_QK_DIM), lambda t: (t, 0)),
                pl.BlockSpec((rows, GLA_QK_DIM), lambda t: (t, 1)),
                pl.BlockSpec((rows, GLA_V_DIM), lambda t: (t, 1)),
                pl.BlockSpec((rows, GLA_V_DIM), lambda t: (t, 2)),
                pl.BlockSpec((rows, GATE_RANK), lambda t: (t, 0)),
                pl.BlockSpec((None, GATE_RANK, GLA_QK_DIM), lambda t: (layer, 0, 0)),
                pl.BlockSpec((None, 1, GLA_QK_DIM), lambda t: (layer, 0, 0)),
                pl.BlockSpec((None, 1, GLA_V_DIM), lambda t: (layer, 0, 0)),
                s_spec]
    args = [qkvg, qkvg, qkvg, qkvg, glr, gate_w2, gate_b, gla_norm_g, state]
    aliases = _stacked_operand(in_specs, args, stacked, 1)
    return pl.pallas_call(
        _gla_step_kernel, grid=(nb // rows,), in_specs=in_specs,
        out_specs=[pl.BlockSpec((rows, GLA_V_DIM), lambda t: (t, 0)), s_spec],
        out_shape=[jax.ShapeDtypeStruct((nb, GLA_V_DIM), BF16),
                   jax.ShapeDtypeStruct(state.shape, F32)],
        input_output_aliases=aliases,
        compiler_params=_params(1), name="gla_step")(*args)


def kernel(x_prompt, x_sample, state_conv, state_gla, state_ffn_conv, norm_mix_g, w_in, conv_w, gate_w2, gate_b, gla_norm_g, w_out, norm_ffn_g, w_up, ffn_conv_w, ffn_conv_b, w_down, final_norm_g):
    batch, seq, _ = x_prompt.shape
    nb = x_sample.shape[0]
    norm_mix_g, norm_ffn_g = norm_mix_g[:, None, :], norm_ffn_g[:, None, :]
    gate_b, gla_norm_g, ffn_conv_b = gate_b[:, None, :], gla_norm_g[:, None, :], ffn_conv_b[:, None, :]
    final_norm_g = final_norm_g[None, :]
    w_in_t = jnp.swapaxes(w_in, 1, 2)

    xp = x_prompt.reshape(batch * seq, D_MODEL)
    xs = x_sample.reshape(nb, D_MODEL)
    conv_p, gla_p, ffn_p = [], [], []
    conv_s, gla_s, ffn_s = None, None, None
    for l in range(DEPTH):
        ya, ya_s, qkvg, qkvg_s, glr, glr_s, c_new, conv_s = _mixer_in(
            xp, xs, state_conv, norm_mix_g, w_in_t, conv_w, l, seq, MIXER_IN_TN, conv_s)

        yb, s_new = _gla_prompt(qkvg, glr, gate_w2, gate_b, gla_norm_g, l, batch, seq, batch)
        xp = _out_proj(ya, yb, w_out, xp, l, OUT_PROJ_TM)

        yb_s, gla_s = _gla_step(qkvg_s, glr_s, state_gla, gate_w2, gate_b, gla_norm_g, l, gla_s)
        xs = _out_proj(ya_s, yb_s, w_out, xs, l, nb)

        xp, xs, f_new, ffn_s = _ffn(xp, xs, state_ffn_conv, norm_ffn_g, w_up, ffn_conv_w, ffn_conv_b, w_down,
                                    final_norm_g, l, seq, FFN_TM, FFN_TF, l == DEPTH - 1, ffn_s)
        conv_p.append(c_new)
        gla_p.append(s_new)
        ffn_p.append(f_new)

    return (xp.reshape(batch, seq, D_MODEL), xs.reshape(nb, 1, D_MODEL),
            jnp.stack(conv_p), jnp.stack(gla_p), jnp.stack(ffn_p), conv_s, gla_s, ffn_s)
```

```python
import functools

import numpy as np
import jax
import jax.numpy as jnp
from jax import lax
from jax.experimental import pallas as pl
from jax.experimental.pallas import tpu as pltpu

F32 = jnp.float32
BF16 = jnp.bfloat16

D_MODEL = 2048
DEPTH = 2
CONV_DIM = 1024
CONV_WIDTH = 3
GLA_HEADS = 4
GLA_QK_DIM = 512
GLA_V_DIM = 1024
GLA_DK = 128
GLA_DV = 256
GATE_RANK = 16
GATE_TAU = 16.0
D_FF = 5632
EPS = 1e-6
MAIN_COLS = 3 * CONV_DIM + 2 * GLA_QK_DIM + 2 * GLA_V_DIM

GLA_CHUNK = 128
GLA_LEVELS = tuple(1 << i for i in range(GLA_CHUNK.bit_length() - 1))
GLA_FINE_LEVELS = tuple(s for s in GLA_LEVELS if s < 4)
GLA_DECAY_ROWS = 8 * GLA_CHUNK

MIXER_IN_TN = 256
OUT_PROJ_TM = 512
FFN_TM, FFN_TF = 1024, 512

VMEM_BYTES_V7X = 64 * 1024 * 1024
VMEM_LIMIT = VMEM_BYTES_V7X - 2 * 1024 * 1024


def _params(n_axes):
    return pltpu.CompilerParams(dimension_semantics=("arbitrary",) * n_axes,
                                vmem_limit_bytes=VMEM_LIMIT)


def _dot(a, b):
    return jnp.dot(a, b, preferred_element_type=F32)


def _dot_nt(a, b):
    return lax.dot_general(a, b, (((1,), (1,)), ((), ())), preferred_element_type=F32)


def _split3(x):
    hi = x.astype(BF16)
    r1 = x - hi.astype(F32)
    mid = r1.astype(BF16)
    lo = (r1 - mid.astype(F32)).astype(BF16)
    return hi, mid, lo


def _exact_dot01(m01, x):
    x1, x2, x3 = _split3(x)
    return _dot(m01, x1) + _dot(m01, x2) + _dot(m01, x3)


def _sigmoid(x):
    return 1.0 / (1.0 + jnp.exp(-x))


def _log_sigmoid(x):
    return jnp.minimum(x, 0.0) - jnp.log1p(jnp.exp(-jnp.abs(x)))


def _rms(x, g):
    var = jnp.mean(x * x, axis=-1, keepdims=True)
    return x * lax.rsqrt(var + EPS) * g


def _delayed_rows(u, prev):
    sub = lax.broadcasted_iota(jnp.int32, prev.shape, 0)
    r1 = pltpu.roll(u, 1, axis=0)
    r2 = pltpu.roll(u, 2, axis=0)
    head1 = jnp.where(sub < 1, pltpu.roll(prev, 1, axis=0), r1[0:8, :])
    head2 = jnp.where(sub < 2, pltpu.roll(prev, 2, axis=0), r2[0:8, :])
    return (jnp.concatenate([head1, r1[8:, :]], axis=0), jnp.concatenate([head2, r2[8:, :]], axis=0))


def _prefetched_rows(x_hbm, xbuf_ref, sem_ref, tile, step, n_tiles):
    rows = xbuf_ref.shape[0]

    def copy(t):
        return pltpu.make_async_copy(x_hbm.at[pl.ds(t * rows, rows), :], xbuf_ref, sem_ref.at[0])

    @pl.when(jnp.logical_and(tile == 0, step == 0))
    def _():
        copy(0).start()

    @pl.when(step == 0)
    def _():
        copy(tile).wait()

    @pl.when(jnp.logical_and(step == 1, tile + 1 < n_tiles))
    def _():
        copy(tile + 1).start()


def _stacked_operand(in_specs, args, stacked, out_index):
    if stacked is None:
        return {}
    in_specs.append(pl.BlockSpec(memory_space=pl.ANY))
    args.append(stacked)
    return {len(args) - 1: out_index}


def _conv_tap_step(new_row, st_ref, w_ref, new_st_ref):
    p1 = st_ref[:, 1, :]
    cu = st_ref[:, 0, :] * w_ref[0:1, :]
    cu = cu + p1 * w_ref[1:2, :]
    cu = cu + new_row * w_ref[2:3, :]
    new_st_ref[:, 0, :] = p1
    new_st_ref[:, 1, :] = new_row
    return cu


def _mixer_in_kernel(x_hbm, xs_ref, sconv_ref, g_ref, wa_ref, wb_ref, wc_ref, wgt_ref, cw_ref, *rest, n_conv):
    (ya_ref, yas_ref, o_ref, os_ref, og_ref, ogs_ref, st_ref, sconv_new_ref,
     xn_ref, xbuf_ref, sem_ref) = rest[-11:]
    j = pl.program_id(1)
    tm = xbuf_ref.shape[0]
    ns = xs_ref.shape[0]
    tn = wa_ref.shape[0]
    _prefetched_rows(x_hbm, xbuf_ref, sem_ref, pl.program_id(0), j, pl.num_programs(0))

    @pl.when(j == 0)
    def _():
        xn_ref[0:tm, :] = _rms(xbuf_ref[...], g_ref[...]).astype(BF16)
        xn_ref[tm:tm + ns, :] = _rms(xs_ref[...], g_ref[...]).astype(BF16)
        gate = _dot_nt(xn_ref[...], wgt_ref[...].astype(BF16))
        og_ref[...] = gate[0:tm, :]
        ogs_ref[...] = gate[tm:tm + ns, :]

    def projected():
        xn = xn_ref[...]
        return [_dot_nt(xn, w_ref[...].astype(BF16)) for w_ref in (wa_ref, wb_ref, wc_ref)]

    @pl.when(j < n_conv)
    def _():
        bg, cg, hin = projected()
        z = cg[0:tm, :] * hin[0:tm, :]
        z1, z2 = _delayed_rows(z, jnp.zeros((8, tn), F32))
        cu = z2 * cw_ref[0:1, :]
        cu = cu + z1 * cw_ref[1:2, :]
        cu = cu + z * cw_ref[2:3, :]
        ya_ref[...] = (bg[0:tm, :] * cu).astype(BF16)
        st_ref[...] = z[tm - (CONV_WIDTH - 1):tm, :]
        cu_s = _conv_tap_step(cg[tm:tm + ns, :] * hin[tm:tm + ns, :], sconv_ref, cw_ref, sconv_new_ref)
        yas_ref[...] = (bg[tm:tm + ns, :] * cu_s).astype(BF16)

    @pl.when(j >= n_conv)
    def _():
        for idx, tile in enumerate(projected()):
            cols = slice(idx * tn, (idx + 1) * tn)
            o_ref[:, cols] = tile[0:tm, :].astype(BF16)
            os_ref[:, cols] = tile[tm:tm + ns, :]


def _mixer_in(x, xs, state_conv, g, w_in_t, conv_w, layer, seq, tn, stacked):
    m, nb = x.shape[0], xs.shape[0]
    batch = m // seq
    ns = nb // batch
    n_conv = CONV_DIM // tn
    qkvg = MAIN_COLS - 3 * CONV_DIM
    n_rest = qkvg // (3 * tn)
    last_conv = n_conv - 1

    def w_spec(group):
        def index(i, j):
            conv_tile = group * n_conv + j
            rest_tile = 3 * n_conv + 3 * (j - n_conv) + group
            return (layer, jnp.where(j < n_conv, conv_tile, rest_tile), 0)
        return pl.BlockSpec((None, tn, D_MODEL), index)

    conv_col = lambda i, j: jnp.minimum(j, last_conv)
    rest_col = lambda i, j: jnp.maximum(j - n_conv, 0)
    x_spec = pl.BlockSpec(memory_space=pl.ANY)
    xs_spec = pl.BlockSpec((ns, D_MODEL), lambda i, j: (i, 0))
    sconv_spec = pl.BlockSpec((None, ns, CONV_WIDTH - 1, tn), lambda i, j: (layer, i, 0, conv_col(i, j)))
    g_spec = pl.BlockSpec((None, 1, D_MODEL), lambda i, j: (layer, 0, 0))
    wgt_spec = pl.BlockSpec((None, GATE_RANK, D_MODEL), lambda i, j: (layer, MAIN_COLS // GATE_RANK, 0))
    cw_spec = pl.BlockSpec((None, CONV_WIDTH, tn), lambda i, j: (layer, 0, conv_col(i, j)))
    in_specs = [x_spec, xs_spec, sconv_spec, g_spec, w_spec(0), w_spec(1), w_spec(2), wgt_spec, cw_spec]
    args = [x, xs, state_conv, g, w_in_t, w_in_t, w_in_t, w_in_t, conv_w]
    aliases = _stacked_operand(in_specs, args, stacked, 7)
    out_specs = [pl.BlockSpec((seq, tn), lambda i, j: (i, conv_col(i, j))),
                 pl.BlockSpec((ns, tn), lambda i, j: (i, conv_col(i, j))),
                 pl.BlockSpec((seq, 3 * tn), lambda i, j: (i, rest_col(i, j))),
                 pl.BlockSpec((ns, 3 * tn), lambda i, j: (i, rest_col(i, j))),
                 pl.BlockSpec((seq, GATE_RANK), lambda i, j: (i, 0)),
                 pl.BlockSpec((ns, GATE_RANK), lambda i, j: (i, 0)),
                 pl.BlockSpec((None, CONV_WIDTH - 1, tn), lambda i, j: (i, 0, conv_col(i, j))),
                 sconv_spec]
    out_shape = [jax.ShapeDtypeStruct((m, CONV_DIM), BF16), jax.ShapeDtypeStruct((nb, CONV_DIM), BF16),
                 jax.ShapeDtypeStruct((m, qkvg), BF16), jax.ShapeDtypeStruct((nb, qkvg), F32),
                 jax.ShapeDtypeStruct((m, GATE_RANK), F32), jax.ShapeDtypeStruct((nb, GATE_RANK), F32),
                 jax.ShapeDtypeStruct((batch, CONV_WIDTH - 1, CONV_DIM), F32),
                 jax.ShapeDtypeStruct(state_conv.shape, F32)]
    return pl.pallas_call(
        functools.partial(_mixer_in_kernel, n_conv=n_conv), grid=(batch, n_conv + n_rest),
        in_specs=in_specs, out_specs=out_specs, out_shape=out_shape,
        scratch_shapes=[pltpu.VMEM((seq + ns, D_MODEL), BF16), pltpu.VMEM((seq, D_MODEL), F32),
                        pltpu.SemaphoreType.DMA((1,))],
        input_output_aliases=aliases, compiler_params=_params(2), name="mixer_in")(*args)


def _out_proj_kernel(ya_ref, yb_ref, wa_ref, wb_ref, x_ref, o_ref):
    acc = _dot(ya_ref[...], wa_ref[...].astype(BF16))
    acc = acc + _dot(yb_ref[...], wb_ref[...].astype(BF16))
    o_ref[...] = x_ref[...] + acc


def _out_proj(ya, yb, w_out, x, layer, tm):
    m = x.shape[0]
    a_spec = pl.BlockSpec((tm, CONV_DIM), lambda i: (i, 0))
    wa_spec = pl.BlockSpec((None, CONV_DIM, D_MODEL), lambda i: (layer, 0, 0), pipeline_mode=pl.Buffered(1))
    wb_spec = pl.BlockSpec((None, GLA_V_DIM, D_MODEL), lambda i: (layer, 1, 0), pipeline_mode=pl.Buffered(1))
    x_spec = pl.BlockSpec((tm, D_MODEL), lambda i: (i, 0))
    return pl.pallas_call(
        _out_proj_kernel, grid=(m // tm,), in_specs=[a_spec, a_spec, wa_spec, wb_spec, x_spec],
        out_specs=x_spec, out_shape=jax.ShapeDtypeStruct((m, D_MODEL), F32),
        compiler_params=_params(1), name="out_proj")(ya, yb, w_out, w_out, x)


def _ffn_kernel(x_hbm, xs_ref, sst_ref, g_ref, wu_ref, wv_ref, cw_ref, cb_ref, wd_ref, fg_ref, *rest,
                tiles_per_seq, final_norm):
    o_ref, os_ref, st_ref, sst_new_ref, xn_ref, prev_ref, carry_ref, xbuf_ref, sem_ref = rest[-9:]
    i = pl.program_id(0)
    f = pl.program_id(1)
    tm = xbuf_ref.shape[0]
    ns = xs_ref.shape[0]
    _prefetched_rows(x_hbm, xbuf_ref, sem_ref, i, f, pl.num_programs(0))

    @pl.when(f == 0)
    def _():
        x = xbuf_ref[...]
        xs = xs_ref[...]
        xn_ref[0:tm, :] = _rms(x, g_ref[...]).astype(BF16)
        xn_ref[tm:tm + ns, :] = _rms(xs, g_ref[...]).astype(BF16)
        o_ref[...] = x
        os_ref[...] = xs

    first = lax.rem(i, tiles_per_seq) == 0

    @pl.when(first)
    def _():
        prev_ref[...] = jnp.zeros(prev_ref.shape, F32)

    @pl.when(jnp.logical_not(first))
    def _():
        prev_ref[...] = carry_ref[f]

    xn = xn_ref[...]
    u_all = _dot(xn, wu_ref[...].astype(BF16))
    v_all = _dot(xn, wv_ref[...].astype(BF16))
    u = u_all[0:tm, :]
    u1, u2 = _delayed_rows(u, prev_ref[...])
    cu = u2 * cw_ref[0:1, :]
    cu = cu + u1 * cw_ref[1:2, :]
    cu = cu + u * cw_ref[2:3, :]
    carry_ref[f] = u[tm - 8:tm, :]
    st_ref[...] = u[tm - (CONV_WIDTH - 1):tm, :]
    cu_s = _conv_tap_step(u_all[tm:tm + ns, :], sst_ref, cw_ref, sst_new_ref)

    pre = jnp.concatenate([cu, cu_s], axis=0) + cb_ref[...]
    h = (pre * _sigmoid(pre) * v_all).astype(BF16)
    acc = _dot(h, wd_ref[...].astype(BF16))
    o_ref[...] += acc[0:tm, :]
    os_ref[...] += acc[tm:tm + ns, :]

    if final_norm:
        @pl.when(f == pl.num_programs(1) - 1)
        def _():
            o_ref[...] = _rms(o_ref[...], fg_ref[...])
            os_ref[...] = _rms(os_ref[...], fg_ref[...])


def _ffn(x, xs, state_s, g, w_up, conv_w, conv_b, w_down, final_g, layer, seq, tm, tf, final_norm, stacked):
    m = x.shape[0]
    n_tiles = m // tm
    ns = xs.shape[0] // n_tiles
    tiles_per_seq = seq // tm
    nf = D_FF // tf
    row_spec = pl.BlockSpec((tm, D_MODEL), lambda i, f: (i, 0))
    srow_spec = pl.BlockSpec((ns, D_MODEL), lambda i, f: (i, 0))
    sst_spec = pl.BlockSpec((None, ns, CONV_WIDTH - 1, tf), lambda i, f: (layer, i, 0, f))
    g_spec = pl.BlockSpec((None, 1, D_MODEL), lambda i, f: (layer, 0, 0))
    wu_spec = pl.BlockSpec((None, D_MODEL, tf), lambda i, f: (layer, 0, f))
    wv_spec = pl.BlockSpec((None, D_MODEL, tf), lambda i, f: (layer, 0, nf + f))
    cw_spec = pl.BlockSpec((None, CONV_WIDTH, tf), lambda i, f: (layer, 0, f))
    cb_spec = pl.BlockSpec((None, 1, tf), lambda i, f: (layer, 0, f))
    wd_spec = pl.BlockSpec((None, tf, D_MODEL), lambda i, f: (layer, f, 0))
    fg_spec = pl.BlockSpec((1, D_MODEL), lambda i, f: (0, 0))
    st_spec = pl.BlockSpec((None, CONV_WIDTH - 1, tf), lambda i, f: (i, 0, f))
    x_spec = pl.BlockSpec(memory_space=pl.ANY)
    in_specs = [x_spec, srow_spec, sst_spec, g_spec, wu_spec, wv_spec, cw_spec, cb_spec, wd_spec, fg_spec]
    args = [x, xs, state_s, g, w_up, w_up, conv_w, conv_b, w_down, final_g]
    aliases = _stacked_operand(in_specs, args, stacked, 3)
    body = functools.partial(_ffn_kernel, tiles_per_seq=tiles_per_seq, final_norm=final_norm)
    out, out_s, tails, state_new = pl.pallas_call(
        body, grid=(n_tiles, nf), in_specs=in_specs,
        out_specs=[row_spec, srow_spec, st_spec, sst_spec],
        out_shape=[jax.ShapeDtypeStruct((m, D_MODEL), F32),
                   jax.ShapeDtypeStruct(xs.shape, F32),
                   jax.ShapeDtypeStruct((n_tiles, CONV_WIDTH - 1, D_FF), F32),
                   jax.ShapeDtypeStruct(state_s.shape, F32)],
        scratch_shapes=[pltpu.VMEM((tm + ns, D_MODEL), BF16), pltpu.VMEM((8, tf), F32),
                        pltpu.VMEM((nf, 8, tf), F32), pltpu.VMEM((tm, D_MODEL), F32),
                        pltpu.SemaphoreType.DMA((1,))],
        input_output_aliases=aliases, compiler_params=_params(2), name="ffn")(*args)
    return out, out_s, tails[tiles_per_seq - 1::tiles_per_seq], state_new


def _ref_rows(s, chunk):
    rows = np.arange(chunk)
    return (rows // (2 * s)) * (2 * s) + s - 1


def _gla_constants(chunk):
    rows = np.arange(chunk)
    tri = (rows[:, None] >= rows[None, :]).astype(np.float32)
    cums = [tri] + [tri[_ref_rows(s, chunk)] for s in GLA_FINE_LEVELS]
    sgns = [np.where(rows > _ref_rows(s, chunk), 1.0, -1.0) * np.log2(np.e) for s in GLA_LEVELS]
    sgn = np.broadcast_to(np.concatenate(sgns)[:, None], (len(GLA_LEVELS) * chunk, GLA_DK))
    return jnp.asarray(np.concatenate(cums, axis=0), BF16), jnp.asarray(sgn, F32)


def _gla_log_decay(glr, w2, gb):
    gate = _dot(glr.astype(BF16), w2.astype(BF16)) + gb
    return _log_sigmoid(gate) * (1.0 / GATE_TAU)


def _gla_decay_kernel(glr_ref, w2_ref, gb_ref, cum_ref, b_ref):
    chunk = GLA_CHUNK
    g = _gla_log_decay(glr_ref[...], w2_ref[...], gb_ref[...])
    cum = cum_ref[...]
    for c in range(glr_ref.shape[0] // chunk):
        rows = slice(c * chunk, (c + 1) * chunk)
        g1, g2, g3 = _split3(g[rows, :])
        cums = _dot(cum, g1) + _dot(cum, g2)
        cums = jnp.concatenate([cums[0:chunk, :] + _dot(cum[0:chunk, :], g3), cums[chunk:, :]], axis=0)
        for part in range(1 + len(GLA_FINE_LEVELS)):
            b_ref[rows, part * GLA_QK_DIM:(part + 1) * GLA_QK_DIM] = cums[part * chunk:(part + 1) * chunk, :]


def _gla_decay(glr, gate_w2, gate_b, cum, layer, tg):
    m = glr.shape[0]
    width = (1 + len(GLA_FINE_LEVELS)) * GLA_QK_DIM
    return pl.pallas_call(
        _gla_decay_kernel, grid=(m // tg,),
        in_specs=[pl.BlockSpec((tg, GATE_RANK), lambda i: (i, 0)),
                  pl.BlockSpec((None, GATE_RANK, GLA_QK_DIM), lambda i: (layer, 0, 0)),
                  pl.BlockSpec((None, 1, GLA_QK_DIM), lambda i: (layer, 0, 0)),
                  pl.BlockSpec(cum.shape, lambda i: (0, 0))],
        out_specs=pl.BlockSpec((tg, width), lambda i: (i, 0)),
        out_shape=jax.ShapeDtypeStruct((m, width), F32),
        compiler_params=_params(1), name="gla_decay")(glr, gate_w2, gate_b, cum)


def _gla_out(o, gn, og):
    var = jnp.mean(o * o, axis=-1, keepdims=True)
    return o * lax.rsqrt(var + EPS) * gn * (og * _sigmoid(og))


def _gla_prompt_kernel(q_ref, k_ref, v_ref, og_ref, b_ref, bf0_ref, bf1_ref, bh0_ref, bh1_ref, bh2_ref, bh3_ref,
                       gn_ref, sgn_ref, qs_ref, ks_ref, vs_ref, ogs_ref, glrs_ref, w2_ref, gb_ref, sst_ref, *rest):
    yb_ref, sfin_ref, ybs_ref, sst_new_ref, s_ref = rest[-5:]
    bf_refs = (bf0_ref, bf1_ref)
    bh_refs = (bh0_ref, bh1_ref, bh2_ref, bh3_ref)
    c_idx = pl.program_id(1)
    n_seq, chunk = q_ref.shape[0], q_ref.shape[1]

    @pl.when(c_idx == 0)
    def _():
        s_ref[...] = jnp.zeros(s_ref.shape, F32)

    row = lax.broadcasted_iota(jnp.int32, (chunk, chunk), 0)
    col = lax.broadcasted_iota(jnp.int32, (chunk, chunk), 1)
    diff_bits = row ^ col

    for n, h in [(n, h) for n in range(n_seq) for h in range(GLA_HEADS)]:
        ks = slice(h * GLA_DK, (h + 1) * GLA_DK)
        vs = slice(h * GLA_DV, (h + 1) * GLA_DV)
        b = b_ref[n, :, ks]
        b_last = b_ref[n, chunk - 1:chunk, ks]
        q = q_ref[n, :, ks]
        k = k_ref[n, :, ks]
        v = v_ref[n, :, vs]
        s_prev = s_ref[n, h]

        inter = _dot(q * jnp.exp(b).astype(BF16), s_prev.astype(BF16))

        acc = _dot_nt(q, k)
        for li, s in enumerate(GLA_LEVELS):
            if s in GLA_FINE_LEVELS:
                beta = bf_refs[GLA_FINE_LEVELS.index(s)][n, :, ks]
            else:
                beta = jnp.concatenate(
                    [bh_refs[h][n, pl.ds(int(r), 8, stride=0), :]
                     for r in _ref_rows(s, chunk)[::2 * s] for _ in range(2 * s // 8)], axis=0)
            e = jnp.exp2((b - beta) * sgn_ref[li * chunk:(li + 1) * chunk, :]).astype(BF16)
            p = _dot_nt(q * e, k * e)
            acc = jnp.where(diff_bits >= s, p, acc)
        a = jnp.where(row >= col, acc, 0.0)
        o = (inter + _dot(a.astype(BF16), v)) * (GLA_DK ** -0.5)

        kdec_t = (k.astype(F32) * jnp.exp(b_last - b)).T
        decay_t = jnp.broadcast_to(jnp.exp(b_last), (GLA_DK, GLA_DK)).T
        s_new = jnp.concatenate([decay_t, decay_t], axis=1) * s_prev + _dot(kdec_t.astype(BF16), v)
        s_ref[n, h] = s_new

        yb_ref[n, :, vs] = _gla_out(o, gn_ref[:, vs], og_ref[n, :, vs].astype(F32)).astype(BF16)

    @pl.when(c_idx == pl.num_programs(1) - 1)
    def _():
        sfin_ref[...] = s_ref[...]

    _gla_step_body(qs_ref, ks_ref, vs_ref, ogs_ref, glrs_ref, w2_ref, gb_ref, gn_ref, sst_ref, ybs_ref, sst_new_ref)


def _gla(qkvg, glr, qkvg_s, glr_s, state_s, gate_w2, gate_b, gla_norm_g, layer, batch, seq, stacked):
    n_seq = batch
    chunk = GLA_CHUNK
    nc = seq // chunk
    cum, sgn = _gla_constants(chunk)
    b_all = _gla_decay(glr, gate_w2, gate_b, cum, layer, GLA_DECAY_ROWS).reshape(batch, seq, -1)
    qkvg = qkvg.reshape(batch, seq, -1)

    def col_spec(width, blk):
        return pl.BlockSpec((n_seq, chunk, width), lambda b, c: (b, c, blk))

    gn_spec = pl.BlockSpec((None, 1, GLA_V_DIM), lambda b, c: (layer, 0, 0))
    sgn_spec = pl.BlockSpec(sgn.shape, lambda b, c: (0, 0))
    yb_spec = pl.BlockSpec((n_seq, chunk, GLA_V_DIM), lambda b, c: (b, c, 0))
    s_spec = pl.BlockSpec((n_seq, GLA_HEADS, GLA_DK, GLA_DV), lambda b, c: (b, 0, 0, 0))
    nb = qkvg_s.shape[0]
    rows = nb // nc

    def srow_spec(width, blk):
        return pl.BlockSpec((rows, width), lambda b, c: (c, blk))

    sst_spec = pl.BlockSpec((None, rows, GLA_HEADS, GLA_DK, GLA_DV), lambda b, c: (layer, c, 0, 0, 0))
    in_specs = ([col_spec(GLA_QK_DIM, 0), col_spec(GLA_QK_DIM, 1), col_spec(GLA_V_DIM, 1), col_spec(GLA_V_DIM, 2)]
                + [col_spec(GLA_QK_DIM, part) for part in range(1 + len(GLA_FINE_LEVELS))]
                + [col_spec(GLA_DK, h) for h in range(GLA_HEADS)] + [gn_spec, sgn_spec]
                + [srow_spec(GLA_QK_DIM, 0), srow_spec(GLA_QK_DIM, 1), srow_spec(GLA_V_DIM, 1),
                   srow_spec(GLA_V_DIM, 2), srow_spec(GATE_RANK, 0),
                   pl.BlockSpec((None, GATE_RANK, GLA_QK_DIM), lambda b, c: (layer, 0, 0)),
                   pl.BlockSpec((None, 1, GLA_QK_DIM), lambda b, c: (layer, 0, 0)), sst_spec])
    args = [qkvg, qkvg, qkvg, qkvg, *([b_all] * (1 + len(GLA_FINE_LEVELS) + GLA_HEADS)), gla_norm_g, sgn,
            qkvg_s, qkvg_s, qkvg_s, qkvg_s, glr_s, gate_w2, gate_b, state_s]
    aliases = _stacked_operand(in_specs, args, stacked, 3)
    yb, s_fin, yb_s, state_new = pl.pallas_call(
        _gla_prompt_kernel, grid=(1, nc), in_specs=in_specs,
        out_specs=[yb_spec, s_spec, srow_spec(GLA_V_DIM, 0), sst_spec],
        out_shape=[jax.ShapeDtypeStruct((batch, seq, GLA_V_DIM), BF16),
                   jax.ShapeDtypeStruct((batch, GLA_HEADS, GLA_DK, GLA_DV), F32),
                   jax.ShapeDtypeStruct((nb, GLA_V_DIM), BF16),
                   jax.ShapeDtypeStruct(state_s.shape, F32)],
        scratch_shapes=[pltpu.VMEM((n_seq, GLA_HEADS, GLA_DK, GLA_DV), F32)],
        input_output_aliases=aliases, compiler_params=_params(2), name="gla")(*args)
    return yb.reshape(batch * seq, GLA_V_DIM), s_fin, yb_s, state_new


def _columns(slab):
    pad = jnp.zeros((GLA_DK - slab.shape[0], GLA_DK), F32)
    return jnp.concatenate([slab, pad], axis=0).T


def _gla_step_body(q_ref, k_ref, v_ref, og_ref, glr_ref, w2_ref, gb_ref, gn_ref, s_ref, yb_ref, snew_ref):
    rows = q_ref.shape[0]
    decay_all = jnp.exp(_gla_log_decay(glr_ref[...], w2_ref[...], gb_ref[...]))
    row_id = lax.broadcasted_iota(jnp.int32, (rows, rows * GLA_DV), 0)
    lane_blk = lax.broadcasted_iota(jnp.int32, (rows, rows * GLA_DV), 1) // GLA_DV

    for h in range(GLA_HEADS):
        ks = slice(h * GLA_DK, (h + 1) * GLA_DK)
        vs = slice(h * GLA_DV, (h + 1) * GLA_DV)
        decay = decay_all[:, ks]
        q = q_ref[:, ks] * (GLA_DK ** -0.5)
        k = k_ref[:, ks]
        v = v_ref[:, vs]
        qk = jnp.sum(q * k, axis=-1, keepdims=True)

        v_diag = jnp.where(row_id == lane_blk, jnp.concatenate([v] * rows, axis=1), 0.0)
        v_diag = jnp.concatenate([v_diag, jnp.zeros_like(v_diag)], axis=0)
        k_t = _columns(k)[:, 0:2 * rows]
        kv = _dot(k_t.astype(BF16), v_diag.astype(BF16))

        decay_c = _columns(decay)
        qd = (q * decay).astype(BF16)
        inter_rows = []
        for i in range(rows):
            s_prev = s_ref[i, h]
            inter_rows.append(_dot(qd, s_prev.astype(BF16))[i:i + 1, :])
            snew_ref[i, h] = decay_c[:, i:i + 1] * s_prev + kv[:, i * GLA_DV:(i + 1) * GLA_DV]
        o = jnp.concatenate(inter_rows, axis=0) + qk * v
        yb_ref[:, vs] = _gla_out(o, gn_ref[:, vs], og_ref[:, vs]).astype(BF16)


def kernel(x_prompt, x_sample, state_conv, state_gla, state_ffn_conv, norm_mix_g, w_in, conv_w, gate_w2, gate_b, gla_norm_g, w_out, norm_ffn_g, w_up, ffn_conv_w, ffn_conv_b, w_down, final_norm_g):
    batch, seq, _ = x_prompt.shape
    nb = x_sample.shape[0]
    norm_mix_g, norm_ffn_g = norm_mix_g[:, None, :], norm_ffn_g[:, None, :]
    gate_b, gla_norm_g, ffn_conv_b = gate_b[:, None, :], gla_norm_g[:, None, :], ffn_conv_b[:, None, :]
    final_norm_g = final_norm_g[None, :]
    w_in_t = jnp.swapaxes(w_in, 1, 2)

    xp = x_prompt.reshape(batch * seq, D_MODEL)
    xs = x_sample.reshape(nb, D_MODEL)
    conv_p, gla_p, ffn_p = [], [], []
    conv_s, gla_s, ffn_s = None, None, None
    for l in range(DEPTH):
        ya, ya_s, qkvg, qkvg_s, glr, glr_s, c_new, conv_s = _mixer_in(
            xp, xs, state_conv, norm_mix_g, w_in_t, conv_w, l, seq, MIXER_IN_TN, conv_s)

        yb, s_new, yb_s, gla_s = _gla(qkvg, glr, qkvg_s, glr_s, state_gla, gate_w2, gate_b, gla_norm_g,
                                      l, batch, seq, gla_s)
        xp = _out_proj(ya, yb, w_out, xp, l, OUT_PROJ_TM)
        xs = _out_proj(ya_s, yb_s, w_out, xs, l, nb)

        xp, xs, f_new, ffn_s = _ffn(xp, xs, state_ffn_conv, norm_ffn_g, w_up, ffn_conv_w, ffn_conv_b, w_down,
                                    final_norm_g, l, seq, FFN_TM, FFN_TF, l == DEPTH - 1, ffn_s)
        conv_p.append(c_new)
        gla_p.append(s_new)
        ffn_p.append(f_new)

    return (xp.reshape(batch, seq, D_MODEL), xs.reshape(nb, 1, D_MODEL),
            jnp.stack(conv_p), jnp.stack(gla_p), jnp.stack(ffn_p), conv_s, gla_s, ffn_s)
```

```python
import functools

import numpy as np
import jax
import jax.numpy as jnp
from jax import lax
from jax.experimental import pallas as pl
from jax.experimental.pallas import tpu as pltpu

F32 = jnp.float32
BF16 = jnp.bfloat16

D_MODEL = 2048
DEPTH = 2
CONV_DIM = 1024
CONV_WIDTH = 3
GLA_HEADS = 4
GLA_QK_DIM = 512
GLA_V_DIM = 1024
GLA_DK = 128
GLA_DV = 256
GATE_RANK = 16
GATE_TAU = 16.0
D_FF = 5632
EPS = 1e-6
MAIN_COLS = 3 * CONV_DIM + 2 * GLA_QK_DIM + 2 * GLA_V_DIM

GLA_CHUNK = 128
GLA_LEVELS = tuple(1 << i for i in range(GLA_CHUNK.bit_length() - 1))
GLA_FINE_LEVELS = tuple(s for s in GLA_LEVELS if s < 4)
GLA_DECAY_ROWS = 8 * GLA_CHUNK

MIXER_IN_TN = 256
OUT_PROJ_TM = 512
FFN_TM, FFN_TF = 1024, 512

VMEM_BYTES_V7X = 64 * 1024 * 1024
VMEM_LIMIT = VMEM_BYTES_V7X - 2 * 1024 * 1024


def _params(n_axes):
    return pltpu.CompilerParams(dimension_semantics=("arbitrary",) * n_axes,
                                vmem_limit_bytes=VMEM_LIMIT)


def _dot(a, b):
    return jnp.dot(a, b, preferred_element_type=F32)


def _dot_nt(a, b):
    return lax.dot_general(a, b, (((1,), (1,)), ((), ())), preferred_element_type=F32)


def _split3(x):
    hi = x.astype(BF16)
    r1 = x - hi.astype(F32)
    mid = r1.astype(BF16)
    lo = (r1 - mid.astype(F32)).astype(BF16)
    return hi, mid, lo


def _exact_dot01(m01, x):
    x1, x2, x3 = _split3(x)
    return _dot(m01, x1) + _dot(m01, x2) + _dot(m01, x3)


def _sigmoid(x):
    return 1.0 / (1.0 + jnp.exp(-x))


def _log_sigmoid(x):
    return jnp.minimum(x, 0.0) - jnp.log1p(jnp.exp(-jnp.abs(x)))


def _rms(x, g):
    var = jnp.mean(x * x, axis=-1, keepdims=True)
    return x * lax.rsqrt(var + EPS) * g


def _delayed_rows(u, prev):
    sub = lax.broadcasted_iota(jnp.int32, prev.shape, 0)
    r1 = pltpu.roll(u, 1, axis=0)
    r2 = pltpu.roll(u, 2, axis=0)
    head1 = jnp.where(sub < 1, pltpu.roll(prev, 1, axis=0), r1[0:8, :])
    head2 = jnp.where(sub < 2, pltpu.roll(prev, 2, axis=0), r2[0:8, :])
    return (jnp.concatenate([head1, r1[8:, :]], axis=0), jnp.concatenate([head2, r2[8:, :]], axis=0))


def _prefetched_rows(x_hbm, xbuf_ref, sem_ref, tile, step, n_tiles):
    rows = xbuf_ref.shape[0]

    def copy(t):
        return pltpu.make_async_copy(x_hbm.at[pl.ds(t * rows, rows), :], xbuf_ref, sem_ref.at[0])

    @pl.when(jnp.logical_and(tile == 0, step == 0))
    def _():
        copy(0).start()

    @pl.when(step == 0)
    def _():
        copy(tile).wait()

    @pl.when(jnp.logical_and(step == 1, tile + 1 < n_tiles))
    def _():
        copy(tile + 1).start()


def _stacked_operand(in_specs, args, stacked, out_index):
    if stacked is None:
        return {}
    in_specs.append(pl.BlockSpec(memory_space=pl.ANY))
    args.append(stacked)
    return {len(args) - 1: out_index}


def _conv_tap_step(new_row, st_ref, w_ref, new_st_ref):
    p1 = st_ref[:, 1, :]
    cu = st_ref[:, 0, :] * w_ref[0:1, :]
    cu = cu + p1 * w_ref[1:2, :]
    cu = cu + new_row * w_ref[2:3, :]
    new_st_ref[:, 0, :] = p1
    new_st_ref[:, 1, :] = new_row
    return cu


def _mixer_in_kernel(x_hbm, xs_ref, sconv_ref, g_ref, wa_ref, wb_ref, wc_ref, wgt_ref, cw_ref, *rest, n_conv):
    (ya_ref, yas_ref, o_ref, os_ref, og_ref, ogs_ref, st_ref, sconv_new_ref,
     xn_ref, xbuf_ref, sem_ref) = rest[-11:]
    j = pl.program_id(1)
    tm = xbuf_ref.shape[0]
    ns = xs_ref.shape[0]
    tn = wa_ref.shape[0]
    _prefetched_rows(x_hbm, xbuf_ref, sem_ref, pl.program_id(0), j, pl.num_programs(0))

    @pl.when(j == 0)
    def _():
        xn_ref[0:tm, :] = _rms(xbuf_ref[...], g_ref[...]).astype(BF16)
        xn_ref[tm:tm + ns, :] = _rms(xs_ref[...], g_ref[...]).astype(BF16)
        gate = _dot_nt(xn_ref[...], wgt_ref[...].astype(BF16))
        og_ref[...] = gate[0:tm, :]
        ogs_ref[...] = gate[tm:tm + ns, :]

    def projected():
        xn = xn_ref[...]
        return [_dot_nt(xn, w_ref[...].astype(BF16)) for w_ref in (wa_ref, wb_ref, wc_ref)]

    @pl.when(j < n_conv)
    def _():
        bg, cg, hin = projected()
        z = cg[0:tm, :] * hin[0:tm, :]
        z1, z2 = _delayed_rows(z, jnp.zeros((8, tn), F32))
        cu = z2 * cw_ref[0:1, :]
        cu = cu + z1 * cw_ref[1:2, :]
        cu = cu + z * cw_ref[2:3, :]
        ya_ref[...] = (bg[0:tm, :] * cu).astype(BF16)
        st_ref[...] = z[tm - (CONV_WIDTH - 1):tm, :]
        cu_s = _conv_tap_step(cg[tm:tm + ns, :] * hin[tm:tm + ns, :], sconv_ref, cw_ref, sconv_new_ref)
        yas_ref[...] = (bg[tm:tm + ns, :] * cu_s).astype(BF16)

    @pl.when(j >= n_conv)
    def _():
        for idx, tile in enumerate(projected()):
            cols = slice(idx * tn, (idx + 1) * tn)
            o_ref[:, cols] = tile[0:tm, :].astype(BF16)
            os_ref[:, cols] = tile[tm:tm + ns, :]


def _mixer_in(x, xs, state_conv, g, w_in_t, conv_w, layer, seq, tn, stacked):
    m, nb = x.shape[0], xs.shape[0]
    batch = m // seq
    ns = nb // batch
    n_conv = CONV_DIM // tn
    qkvg = MAIN_COLS - 3 * CONV_DIM
    n_rest = qkvg // (3 * tn)
    last_conv = n_conv - 1

    def w_spec(group):
        def index(i, j):
            conv_tile = group * n_conv + j
            rest_tile = 3 * n_conv + 3 * (j - n_conv) + group
            return (layer, jnp.where(j < n_conv, conv_tile, rest_tile), 0)
        return pl.BlockSpec((None, tn, D_MODEL), index)

    conv_col = lambda i, j: jnp.minimum(j, last_conv)
    rest_col = lambda i, j: jnp.maximum(j - n_conv, 0)
    x_spec = pl.BlockSpec(memory_space=pl.ANY)
    xs_spec = pl.BlockSpec((ns, D_MODEL), lambda i, j: (i, 0))
    sconv_spec = pl.BlockSpec((None, ns, CONV_WIDTH - 1, tn), lambda i, j: (layer, i, 0, conv_col(i, j)))
    g_spec = pl.BlockSpec((None, 1, D_MODEL), lambda i, j: (layer, 0, 0))
    wgt_spec = pl.BlockSpec((None, GATE_RANK, D_MODEL), lambda i, j: (layer, MAIN_COLS // GATE_RANK, 0))
    cw_spec = pl.BlockSpec((None, CONV_WIDTH, tn), lambda i, j: (layer, 0, conv_col(i, j)))
    in_specs = [x_spec, xs_spec, sconv_spec, g_spec, w_spec(0), w_spec(1), w_spec(2), wgt_spec, cw_spec]
    args = [x, xs, state_conv, g, w_in_t, w_in_t, w_in_t, w_in_t, conv_w]
    aliases = _stacked_operand(in_specs, args, stacked, 7)
    out_specs = [pl.BlockSpec((seq, tn), lambda i, j: (i, conv_col(i, j))),
                 pl.BlockSpec((ns, tn), lambda i, j: (i, conv_col(i, j))),
                 pl.BlockSpec((seq, 3 * tn), lambda i, j: (i, rest_col(i, j))),
                 pl.BlockSpec((ns, 3 * tn), lambda i, j: (i, rest_col(i, j))),
                 pl.BlockSpec((seq, GATE_RANK), lambda i, j: (i, 0)),
                 pl.BlockSpec((ns, GATE_RANK), lambda i, j: (i, 0)),
                 pl.BlockSpec((None, CONV_WIDTH - 1, tn), lambda i, j: (i, 0, conv_col(i, j))),
                 sconv_spec]
    out_shape = [jax.ShapeDtypeStruct((m, CONV_DIM), BF16), jax.ShapeDtypeStruct((nb, CONV_DIM), BF16),
                 jax.ShapeDtypeStruct((m, qkvg), BF16), jax.ShapeDtypeStruct((nb, qkvg), F32),
                 jax.ShapeDtypeStruct((m, GATE_RANK), F32), jax.ShapeDtypeStruct((nb, GATE_RANK), F32),
                 jax.ShapeDtypeStruct((batch, CONV_WIDTH - 1, CONV_DIM), F32),
                 jax.ShapeDtypeStruct(state_conv.shape, F32)]
    return pl.pallas_call(
        functools.partial(_mixer_in_kernel, n_conv=n_conv), grid=(batch, n_conv + n_rest),
        in_specs=in_specs, out_specs=out_specs, out_shape=out_shape,
        scratch_shapes=[pltpu.VMEM((seq + ns, D_MODEL), BF16), pltpu.VMEM((seq, D_MODEL), F32),
                        pltpu.SemaphoreType.DMA((1,))],
        input_output_aliases=aliases, compiler_params=_params(2), name="mixer_in")(*args)


def _out_proj_kernel(ya_ref, yb_ref, yas_ref, ybs_ref, wa_ref, wb_ref, x_ref, xs_ref, o_ref, os_ref):
    tm = ya_ref.shape[0]
    ya = jnp.concatenate([ya_ref[...], yas_ref[...]], axis=0)
    yb = jnp.concatenate([yb_ref[...], ybs_ref[...]], axis=0)
    acc = _dot(ya, wa_ref[...].astype(BF16))
    acc = acc + _dot(yb, wb_ref[...].astype(BF16))
    o_ref[...] = x_ref[...] + acc[0:tm]
    os_ref[...] = xs_ref[...] + acc[tm:]


def _out_proj(ya, yb, ya_s, yb_s, w_out, x, xs, layer, tm):
    m, nb = x.shape[0], xs.shape[0]
    n_tiles = m // tm
    ns = nb // n_tiles
    rows = lambda r, cols: pl.BlockSpec((r, cols), lambda i: (i, 0))
    wa_spec = pl.BlockSpec((None, CONV_DIM, D_MODEL), lambda i: (layer, 0, 0), pipeline_mode=pl.Buffered(1))
    wb_spec = pl.BlockSpec((None, GLA_V_DIM, D_MODEL), lambda i: (layer, 1, 0), pipeline_mode=pl.Buffered(1))
    return pl.pallas_call(
        _out_proj_kernel, grid=(n_tiles,),
        in_specs=[rows(tm, CONV_DIM), rows(tm, GLA_V_DIM), rows(ns, CONV_DIM), rows(ns, GLA_V_DIM),
                  wa_spec, wb_spec, rows(tm, D_MODEL), rows(ns, D_MODEL)],
        out_specs=[rows(tm, D_MODEL), rows(ns, D_MODEL)],
        out_shape=[jax.ShapeDtypeStruct((m, D_MODEL), F32), jax.ShapeDtypeStruct((nb, D_MODEL), F32)],
        compiler_params=_params(1), name="out_proj")(ya, yb, ya_s, yb_s, w_out, w_out, x, xs)


def _ffn_kernel(x_hbm, xs_ref, sst_ref, g_ref, wu_ref, wv_ref, cw_ref, cb_ref, wd_ref, fg_ref, *rest,
                tiles_per_seq, final_norm):
    o_ref, os_ref, st_ref, sst_new_ref, xn_ref, prev_ref, carry_ref, xbuf_ref, sem_ref = rest[-9:]
    i = pl.program_id(0)
    f = pl.program_id(1)
    tm = xbuf_ref.shape[0]
    ns = xs_ref.shape[0]
    _prefetched_rows(x_hbm, xbuf_ref, sem_ref, i, f, pl.num_programs(0))

    @pl.when(f == 0)
    def _():
        x = xbuf_ref[...]
        xs = xs_ref[...]
        xn_ref[0:tm, :] = _rms(x, g_ref[...]).astype(BF16)
        xn_ref[tm:tm + ns, :] = _rms(xs, g_ref[...]).astype(BF16)
        o_ref[...] = x
        os_ref[...] = xs

    first = lax.rem(i, tiles_per_seq) == 0

    @pl.when(first)
    def _():
        prev_ref[...] = jnp.zeros(prev_ref.shape, F32)

    @pl.when(jnp.logical_not(first))
    def _():
        prev_ref[...] = carry_ref[f]

    xn = xn_ref[...]
    u_all = _dot(xn, wu_ref[...].astype(BF16))
    v_all = _dot(xn, wv_ref[...].astype(BF16))
    u = u_all[0:tm, :]
    u1, u2 = _delayed_rows(u, prev_ref[...])
    cu = u2 * cw_ref[0:1, :]
    cu = cu + u1 * cw_ref[1:2, :]
    cu = cu + u * cw_ref[2:3, :]
    carry_ref[f] = u[tm - 8:tm, :]
    st_ref[...] = u[tm - (CONV_WIDTH - 1):tm, :]
    cu_s = _conv_tap_step(u_all[tm:tm + ns, :], sst_ref, cw_ref, sst_new_ref)

    pre = jnp.concatenate([cu, cu_s], axis=0) + cb_ref[...]
    h = (pre * _sigmoid(pre) * v_all).astype(BF16)
    acc = _dot(h, wd_ref[...].astype(BF16))
    o_ref[...] += acc[0:tm, :]
    os_ref[...] += acc[tm:tm + ns, :]

    if final_norm:
        @pl.when(f == pl.num_programs(1) - 1)
        def _():
            o_ref[...] = _rms(o_ref[...], fg_ref[...])
            os_ref[...] = _rms(os_ref[...], fg_ref[...])


def _ffn(x, xs, state_s, g, w_up, conv_w, conv_b, w_down, final_g, layer, seq, tm, tf, final_norm, stacked):
    m = x.shape[0]
    n_tiles = m // tm
    ns = xs.shape[0] // n_tiles
    tiles_per_seq = seq // tm
    nf = D_FF // tf
    row_spec = pl.BlockSpec((tm, D_MODEL), lambda i, f: (i, 0))
    srow_spec = pl.BlockSpec((ns, D_MODEL), lambda i, f: (i, 0))
    sst_spec = pl.BlockSpec((None, ns, CONV_WIDTH - 1, tf), lambda i, f: (layer, i, 0, f))
    g_spec = pl.BlockSpec((None, 1, D_MODEL), lambda i, f: (layer, 0, 0))
    wu_spec = pl.BlockSpec((None, D_MODEL, tf), lambda i, f: (layer, 0, f))
    wv_spec = pl.BlockSpec((None, D_MODEL, tf), lambda i, f: (layer, 0, nf + f))
    cw_spec = pl.BlockSpec((None, CONV_WIDTH, tf), lambda i, f: (layer, 0, f))
    cb_spec = pl.BlockSpec((None, 1, tf), lambda i, f: (layer, 0, f))
    wd_spec = pl.BlockSpec((None, tf, D_MODEL), lambda i, f: (layer, f, 0))
    fg_spec = pl.BlockSpec((1, D_MODEL), lambda i, f: (0, 0))
    st_spec = pl.BlockSpec((None, CONV_WIDTH - 1, tf), lambda i, f: (i, 0, f))
    x_spec = pl.BlockSpec(memory_space=pl.ANY)
    in_specs = [x_spec, srow_spec, sst_spec, g_spec, wu_spec, wv_spec, cw_spec, cb_spec, wd_spec, fg_spec]
    args = [x, xs, state_s, g, w_up, w_up, conv_w, conv_b, w_down, final_g]
    aliases = _stacked_operand(in_specs, args, stacked, 3)
    body = functools.partial(_ffn_kernel, tiles_per_seq=tiles_per_seq, final_norm=final_norm)
    out, out_s, tails, state_new = pl.pallas_call(
        body, grid=(n_tiles, nf), in_specs=in_specs,
        out_specs=[row_spec, srow_spec, st_spec, sst_spec],
        out_shape=[jax.ShapeDtypeStruct((m, D_MODEL), F32),
                   jax.ShapeDtypeStruct(xs.shape, F32),
                   jax.ShapeDtypeStruct((n_tiles, CONV_WIDTH - 1, D_FF), F32),
                   jax.ShapeDtypeStruct(state_s.shape, F32)],
        scratch_shapes=[pltpu.VMEM((tm + ns, D_MODEL), BF16), pltpu.VMEM((8, tf), F32),
                        pltpu.VMEM((nf, 8, tf), F32), pltpu.VMEM((tm, D_MODEL), F32),
                        pltpu.SemaphoreType.DMA((1,))],
        input_output_aliases=aliases, compiler_params=_params(2), name="ffn")(*args)
    return out, out_s, tails[tiles_per_seq - 1::tiles_per_seq], state_new


def _ref_rows(s, chunk):
    rows = np.arange(chunk)
    return (rows // (2 * s)) * (2 * s) + s - 1


def _gla_constants(chunk):
    rows = np.arange(chunk)
    tri = (rows[:, None] >= rows[None, :]).astype(np.float32)
    cums = [tri] + [tri[_ref_rows(s, chunk)] for s in GLA_FINE_LEVELS]
    sgns = [np.where(rows > _ref_rows(s, chunk), 1.0, -1.0) * np.log2(np.e) for s in GLA_LEVELS]
    sgn = np.broadcast_to(np.concatenate(sgns)[:, None], (len(GLA_LEVELS) * chunk, GLA_DK))
    return jnp.asarray(np.concatenate(cums, axis=0), BF16), jnp.asarray(sgn, F32)


def _gla_log_decay(glr, w2, gb):
    gate = _dot(glr.astype(BF16), w2.astype(BF16)) + gb
    return _log_sigmoid(gate) * (1.0 / GATE_TAU)


def _gla_decay_kernel(glr_ref, w2_ref, gb_ref, cum_ref, b_ref):
    chunk = GLA_CHUNK
    g = _gla_log_decay(glr_ref[...], w2_ref[...], gb_ref[...])
    cum = cum_ref[...]
    for c in range(glr_ref.shape[0] // chunk):
        rows = slice(c * chunk, (c + 1) * chunk)
        g1, g2, g3 = _split3(g[rows, :])
        cums = _dot(cum, g1) + _dot(cum, g2)
        cums = jnp.concatenate([cums[0:chunk, :] + _dot(cum[0:chunk, :], g3), cums[chunk:, :]], axis=0)
        for part in range(1 + len(GLA_FINE_LEVELS)):
            b_ref[rows, part * GLA_QK_DIM:(part + 1) * GLA_QK_DIM] = cums[part * chunk:(part + 1) * chunk, :]


def _gla_decay(glr, gate_w2, gate_b, cum, layer, tg):
    m = glr.shape[0]
    width = (1 + len(GLA_FINE_LEVELS)) * GLA_QK_DIM
    return pl.pallas_call(
        _gla_decay_kernel, grid=(m // tg,),
        in_specs=[pl.BlockSpec((tg, GATE_RANK), lambda i: (i, 0)),
                  pl.BlockSpec((None, GATE_RANK, GLA_QK_DIM), lambda i: (layer, 0, 0)),
                  pl.BlockSpec((None, 1, GLA_QK_DIM), lambda i: (layer, 0, 0)),
                  pl.BlockSpec(cum.shape, lambda i: (0, 0))],
        out_specs=pl.BlockSpec((tg, width), lambda i: (i, 0)),
        out_shape=jax.ShapeDtypeStruct((m, width), F32),
        compiler_params=_params(1), name="gla_decay")(glr, gate_w2, gate_b, cum)


def _gla_out(o, gn, og):
    var = jnp.mean(o * o, axis=-1, keepdims=True)
    return o * lax.rsqrt(var + EPS) * gn * (og * _sigmoid(og))


def _gla_prompt_kernel(q_ref, k_ref, v_ref, og_ref, b_ref, bf0_ref, bf1_ref, bh0_ref, bh1_ref, bh2_ref, bh3_ref,
                       gn_ref, sgn_ref, qs_ref, ks_ref, vs_ref, ogs_ref, glrs_ref, w2_ref, gb_ref, sst_ref, *rest):
    yb_ref, sfin_ref, ybs_ref, sst_new_ref, s_ref = rest[-5:]
    bf_refs = (bf0_ref, bf1_ref)
    bh_refs = (bh0_ref, bh1_ref, bh2_ref, bh3_ref)
    c_idx = pl.program_id(1)
    n_seq, chunk = q_ref.shape[0], q_ref.shape[1]

    @pl.when(c_idx == 0)
    def _():
        s_ref[...] = jnp.zeros(s_ref.shape, F32)

    row = lax.broadcasted_iota(jnp.int32, (chunk, chunk), 0)
    col = lax.broadcasted_iota(jnp.int32, (chunk, chunk), 1)
    diff_bits = row ^ col

    for n, h in [(n, h) for n in range(n_seq) for h in range(GLA_HEADS)]:
        ks = slice(h * GLA_DK, (h + 1) * GLA_DK)
        vs = slice(h * GLA_DV, (h + 1) * GLA_DV)
        b = b_ref[n, :, ks]
        b_last = b_ref[n, chunk - 1:chunk, ks]
        q = q_ref[n, :, ks]
        k = k_ref[n, :, ks]
        v = v_ref[n, :, vs]
        s_prev = s_ref[n, h]

        inter = _dot(q * jnp.exp(b).astype(BF16), s_prev.astype(BF16))

        acc = _dot_nt(q, k)
        for li, s in enumerate(GLA_LEVELS):
            if s in GLA_FINE_LEVELS:
                beta = bf_refs[GLA_FINE_LEVELS.index(s)][n, :, ks]
            else:
                beta = jnp.concatenate(
                    [bh_refs[h][n, pl.ds(int(r), 8, stride=0), :]
                     for r in _ref_rows(s, chunk)[::2 * s] for _ in range(2 * s // 8)], axis=0)
            e = jnp.exp2((b - beta) * sgn_ref[li * chunk:(li + 1) * chunk, :]).astype(BF16)
            p = _dot_nt(q * e, k * e)
            acc = jnp.where(diff_bits >= s, p, acc)
        a = jnp.where(row >= col, acc, 0.0)
        o = (inter + _dot(a.astype(BF16), v)) * (GLA_DK ** -0.5)

        kdec_t = (k.astype(F32) * jnp.exp(b_last - b)).T
        decay_t = jnp.broadcast_to(jnp.exp(b_last), (GLA_DK, GLA_DK)).T
        s_new = jnp.concatenate([decay_t, decay_t], axis=1) * s_prev + _dot(kdec_t.astype(BF16), v)
        s_ref[n, h] = s_new

        yb_ref[n, :, vs] = _gla_out(o, gn_ref[:, vs], og_ref[n, :, vs].astype(F32)).astype(BF16)

    @pl.when(c_idx == pl.num_programs(1) - 1)
    def _():
        sfin_ref[...] = s_ref[...]

    _gla_step_body(qs_ref, ks_ref, vs_ref, ogs_ref, glrs_ref, w2_ref, gb_ref, gn_ref, sst_ref, ybs_ref, sst_new_ref)


def _gla(qkvg, glr, qkvg_s, glr_s, state_s, gate_w2, gate_b, gla_norm_g, layer, batch, seq, stacked):
    n_seq = batch
    chunk = GLA_CHUNK
    nc = seq // chunk
    cum, sgn = _gla_constants(chunk)
    b_all = _gla_decay(glr, gate_w2, gate_b, cum, layer, GLA_DECAY_ROWS).reshape(batch, seq, -1)
    qkvg = qkvg.reshape(batch, seq, -1)

    def col_spec(width, blk):
        return pl.BlockSpec((n_seq, chunk, width), lambda b, c: (b, c, blk))

    gn_spec = pl.BlockSpec((None, 1, GLA_V_DIM), lambda b, c: (layer, 0, 0))
    sgn_spec = pl.BlockSpec(sgn.shape, lambda b, c: (0, 0))
    yb_spec = pl.BlockSpec((n_seq, chunk, GLA_V_DIM), lambda b, c: (b, c, 0))
    s_spec = pl.BlockSpec((n_seq, GLA_HEADS, GLA_DK, GLA_DV), lambda b, c: (b, 0, 0, 0))
    nb = qkvg_s.shape[0]
    rows = nb // nc

    def srow_spec(width, blk):
        return pl.BlockSpec((rows, width), lambda b, c: (c, blk))

    sst_spec = pl.BlockSpec((None, rows, GLA_HEADS, GLA_DK, GLA_DV), lambda b, c: (layer, c, 0, 0, 0))
    in_specs = ([col_spec(GLA_QK_DIM, 0), col_spec(GLA_QK_DIM, 1), col_spec(GLA_V_DIM, 1), col_spec(GLA_V_DIM, 2)]
                + [col_spec(GLA_QK_DIM, part) for part in range(1 + len(GLA_FINE_LEVELS))]
                + [col_spec(GLA_DK, h) for h in range(GLA_HEADS)] + [gn_spec, sgn_spec]
                + [srow_spec(GLA_QK_DIM, 0), srow_spec(GLA_QK_DIM, 1), srow_spec(GLA_V_DIM, 1),
                   srow_spec(GLA_V_DIM, 2), srow_spec(GATE_RANK, 0),
                   pl.BlockSpec((None, GATE_RANK, GLA_QK_DIM), lambda b, c: (layer, 0, 0)),
                   pl.BlockSpec((None, 1, GLA_QK_DIM), lambda b, c: (layer, 0, 0)), sst_spec])
    args = [qkvg, qkvg, qkvg, qkvg, *([b_all] * (1 + len(GLA_FINE_LEVELS) + GLA_HEADS)), gla_norm_g, sgn,
            qkvg_s, qkvg_s, qkvg_s, qkvg_s, glr_s, gate_w2, gate_b, state_s]
    aliases = _stacked_operand(in_specs, args, stacked, 3)
    yb, s_fin, yb_s, state_new = pl.pallas_call(
        _gla_prompt_kernel, grid=(1, nc), in_specs=in_specs,
        out_specs=[yb_spec, s_spec, srow_spec(GLA_V_DIM, 0), sst_spec],
        out_shape=[jax.ShapeDtypeStruct((batch, seq, GLA_V_DIM), BF16),
                   jax.ShapeDtypeStruct((batch, GLA_HEADS, GLA_DK, GLA_DV), F32),
                   jax.ShapeDtypeStruct((nb, GLA_V_DIM), BF16),
                   jax.ShapeDtypeStruct(state_s.shape, F32)],
        scratch_shapes=[pltpu.VMEM((n_seq, GLA_HEADS, GLA_DK, GLA_DV), F32)],
        input_output_aliases=aliases, compiler_params=_params(2), name="gla")(*args)
    return yb.reshape(batch * seq, GLA_V_DIM), s_fin, yb_s, state_new


def _columns(slab):
    pad = jnp.zeros((GLA_DK - slab.shape[0], GLA_DK), F32)
    return jnp.concatenate([slab, pad], axis=0).T


def _gla_step_body(q_ref, k_ref, v_ref, og_ref, glr_ref, w2_ref, gb_ref, gn_ref, s_ref, yb_ref, snew_ref):
    rows = q_ref.shape[0]
    decay_all = jnp.exp(_gla_log_decay(glr_ref[...], w2_ref[...], gb_ref[...]))
    row_id = lax.broadcasted_iota(jnp.int32, (rows, rows * GLA_DV), 0)
    lane_blk = lax.broadcasted_iota(jnp.int32, (rows, rows * GLA_DV), 1) // GLA_DV

    for h in range(GLA_HEADS):
        ks = slice(h * GLA_DK, (h + 1) * GLA_DK)
        vs = slice(h * GLA_DV, (h + 1) * GLA_DV)
        decay = decay_all[:, ks]
        q = q_ref[:, ks] * (GLA_DK ** -0.5)
        k = k_ref[:, ks]
        v = v_ref[:, vs]
        qk = jnp.sum(q * k, axis=-1, keepdims=True)

        v_diag = jnp.where(row_id == lane_blk, jnp.concatenate([v] * rows, axis=1), 0.0)
        v_diag = jnp.concatenate([v_diag, jnp.zeros_like(v_diag)], axis=0)
        k_t = _columns(k)[:, 0:2 * rows]
        kv = _dot(k_t.astype(BF16), v_diag.astype(BF16))

        decay_c = _columns(decay)
        qd = (q * decay).astype(BF16)
        inter_rows = []
        for i in range(rows):
            s_prev = s_ref[i, h]
            inter_rows.append(_dot(qd, s_prev.astype(BF16))[i:i + 1, :])
            snew_ref[i, h] = decay_c[:, i:i + 1] * s_prev + kv[:, i * GLA_DV:(i + 1) * GLA_DV]
        o = jnp.concatenate(inter_rows, axis=0) + qk * v
        yb_ref[:, vs] = _gla_out(o, gn_ref[:, vs], og_ref[:, vs]).astype(BF16)


def kernel(x_prompt, x_sample, state_conv, state_gla, state_ffn_conv, norm_mix_g, w_in, conv_w, gate_w2, gate_b, gla_norm_g, w_out, norm_ffn_g, w_up, ffn_conv_w, ffn_conv_b, w_down, final_norm_g):
    batch, seq, _ = x_prompt.shape
    nb = x_sample.shape[0]
    norm_mix_g, norm_ffn_g = norm_mix_g[:, None, :], norm_ffn_g[:, None, :]
    gate_b, gla_norm_g, ffn_conv_b = gate_b[:, None, :], gla_norm_g[:, None, :], ffn_conv_b[:, None, :]
    final_norm_g = final_norm_g[None, :]
    w_in_t = jnp.swapaxes(w_in, 1, 2)

    xp = x_prompt.reshape(batch * seq, D_MODEL)
    xs = x_sample.reshape(nb, D_MODEL)
    conv_p, gla_p, ffn_p = [], [], []
    conv_s, gla_s, ffn_s = None, None, None
    for l in range(DEPTH):
        ya, ya_s, qkvg, qkvg_s, glr, glr_s, c_new, conv_s = _mixer_in(
            xp, xs, state_conv, norm_mix_g, w_in_t, conv_w, l, seq, MIXER_IN_TN, conv_s)

        yb, s_new, yb_s, gla_s = _gla(qkvg, glr, qkvg_s, glr_s, state_gla, gate_w2, gate_b, gla_norm_g,
                                      l, batch, seq, gla_s)
        xp, xs = _out_proj(ya, yb, ya_s, yb_s, w_out, xp, xs, l, OUT_PROJ_TM)

        xp, xs, f_new, ffn_s = _ffn(xp, xs, state_ffn_conv, norm_ffn_g, w_up, ffn_conv_w, ffn_conv_b, w_down,
                                    final_norm_g, l, seq, FFN_TM, FFN_TF, l == DEPTH - 1, ffn_s)
        conv_p.append(c_new)
        gla_p.append(s_new)
        ffn_p.append(f_new)

    return (xp.reshape(batch, seq, D_MODEL), xs.reshape(nb, 1, D_MODEL),
            jnp.stack(conv_p), jnp.stack(gla_p), jnp.stack(ffn_p), conv_s, gla_s, ffn_s)
```

```python
import functools

import numpy as np
import jax
import jax.numpy as jnp
from jax import lax
from jax.experimental import pallas as pl
from jax.experimental.pallas import tpu as pltpu

F32 = jnp.float32
BF16 = jnp.bfloat16

D_MODEL = 2048
DEPTH = 2
CONV_DIM = 1024
CONV_WIDTH = 3
GLA_HEADS = 4
GLA_QK_DIM = 512
GLA_V_DIM = 1024
GLA_DK = 128
GLA_DV = 256
GATE_RANK = 16
GATE_TAU = 16.0
D_FF = 5632
EPS = 1e-6
MAIN_COLS = 3 * CONV_DIM + 2 * GLA_QK_DIM + 2 * GLA_V_DIM

GLA_CHUNK = 128
GLA_LEVELS = tuple(1 << i for i in range(GLA_CHUNK.bit_length() - 1))
GLA_FINE_LEVELS = tuple(s for s in GLA_LEVELS if s < 4)
GLA_DECAY_ROWS = 8 * GLA_CHUNK

MIXER_IN_TN = 256
OUT_PROJ_TM = 512
FFN_TM, FFN_TF = 1024, 512

VMEM_BYTES_V7X = 64 * 1024 * 1024
VMEM_LIMIT = VMEM_BYTES_V7X - 2 * 1024 * 1024


def _params(n_axes):
    return pltpu.CompilerParams(dimension_semantics=("arbitrary",) * n_axes,
                                vmem_limit_bytes=VMEM_LIMIT)


def _dot(a, b):
    return jnp.dot(a, b, preferred_element_type=F32)


def _dot_nt(a, b):
    return lax.dot_general(a, b, (((1,), (1,)), ((), ())), preferred_element_type=F32)


def _split3(x):
    hi = x.astype(BF16)
    r1 = x - hi.astype(F32)
    mid = r1.astype(BF16)
    lo = (r1 - mid.astype(F32)).astype(BF16)
    return hi, mid, lo


def _exact_dot01(m01, x):
    x1, x2, x3 = _split3(x)
    return _dot(m01, x1) + _dot(m01, x2) + _dot(m01, x3)


def _sigmoid(x):
    return 1.0 / (1.0 + jnp.exp(-x))


def _log_sigmoid(x):
    return jnp.minimum(x, 0.0) - jnp.log1p(jnp.exp(-jnp.abs(x)))


def _rms(x, g):
    var = jnp.mean(x * x, axis=-1, keepdims=True)
    return x * lax.rsqrt(var + EPS) * g


def _delayed_rows(u, prev):
    sub = lax.broadcasted_iota(jnp.int32, prev.shape, 0)
    r1 = pltpu.roll(u, 1, axis=0)
    r2 = pltpu.roll(u, 2, axis=0)
    head1 = jnp.where(sub < 1, pltpu.roll(prev, 1, axis=0), r1[0:8, :])
    head2 = jnp.where(sub < 2, pltpu.roll(prev, 2, axis=0), r2[0:8, :])
    return (jnp.concatenate([head1, r1[8:, :]], axis=0), jnp.concatenate([head2, r2[8:, :]], axis=0))


def _prefetched_rows(x_hbm, xbuf_ref, sem_ref, tile, step, n_tiles):
    rows = xbuf_ref.shape[0]

    def copy(t):
        return pltpu.make_async_copy(x_hbm.at[pl.ds(t * rows, rows), :], xbuf_ref, sem_ref.at[0])

    @pl.when(jnp.logical_and(tile == 0, step == 0))
    def _():
        copy(0).start()

    @pl.when(step == 0)
    def _():
        copy(tile).wait()

    @pl.when(jnp.logical_and(step == 1, tile + 1 < n_tiles))
    def _():
        copy(tile + 1).start()


def _stacked_operand(in_specs, args, stacked, out_index):
    if stacked is None:
        return {}
    in_specs.append(pl.BlockSpec(memory_space=pl.ANY))
    args.append(stacked)
    return {len(args) - 1: out_index}


def _conv_tap_step(new_row, st_ref, w_ref, new_st_ref):
    p1 = st_ref[:, 1, :]
    cu = st_ref[:, 0, :] * w_ref[0:1, :]
    cu = cu + p1 * w_ref[1:2, :]
    cu = cu + new_row * w_ref[2:3, :]
    new_st_ref[:, 0, :] = p1
    new_st_ref[:, 1, :] = new_row
    return cu


def _mixer_in_kernel(x_hbm, xs_ref, sconv_ref, g_ref, wa_ref, wb_ref, wc_ref, wgt_ref, cw_ref, *rest, n_conv, layer):
    (ya_ref, yas_ref, o_ref, os_ref, og_ref, ogs_ref, st_ref, sconv_new_ref,
     xn_ref, xbuf_ref, sem_ref) = rest[-11:]
    j = pl.program_id(1)
    tm = xbuf_ref.shape[0]
    ns = xs_ref.shape[0]
    tn = wa_ref.shape[0]
    _prefetched_rows(x_hbm, xbuf_ref, sem_ref, pl.program_id(0), j, pl.num_programs(0))

    @pl.when(j == 0)
    def _():
        g = g_ref[layer:layer + 1, :]
        xn_ref[0:tm, :] = _rms(xbuf_ref[...], g).astype(BF16)
        xn_ref[tm:tm + ns, :] = _rms(xs_ref[...], g).astype(BF16)
        gate = _dot_nt(xn_ref[...], wgt_ref[...].astype(BF16))
        og_ref[...] = gate[0:tm, :]
        ogs_ref[...] = gate[tm:tm + ns, :]

    def projected():
        xn = xn_ref[...]
        return [_dot_nt(xn, w_ref[...].astype(BF16)) for w_ref in (wa_ref, wb_ref, wc_ref)]

    @pl.when(j < n_conv)
    def _():
        bg, cg, hin = projected()
        z = cg[0:tm, :] * hin[0:tm, :]
        z1, z2 = _delayed_rows(z, jnp.zeros((8, tn), F32))
        cu = z2 * cw_ref[0:1, :]
        cu = cu + z1 * cw_ref[1:2, :]
        cu = cu + z * cw_ref[2:3, :]
        ya_ref[...] = (bg[0:tm, :] * cu).astype(BF16)
        st_ref[...] = z[tm - (CONV_WIDTH - 1):tm, :]
        cu_s = _conv_tap_step(cg[tm:tm + ns, :] * hin[tm:tm + ns, :], sconv_ref, cw_ref, sconv_new_ref)
        yas_ref[...] = (bg[tm:tm + ns, :] * cu_s).astype(BF16)

    @pl.when(j >= n_conv)
    def _():
        for idx, tile in enumerate(projected()):
            cols = slice(idx * tn, (idx + 1) * tn)
            o_ref[:, cols] = tile[0:tm, :].astype(BF16)
            os_ref[:, cols] = tile[tm:tm + ns, :]


def _mixer_in(x, xs, state_conv, g, w_in_t, conv_w, layer, seq, tn, stacked):
    m, nb = x.shape[0], xs.shape[0]
    batch = m // seq
    ns = nb // batch
    n_conv = CONV_DIM // tn
    qkvg = MAIN_COLS - 3 * CONV_DIM
    n_rest = qkvg // (3 * tn)
    last_conv = n_conv - 1

    def w_spec(group):
        def index(i, j):
            conv_tile = group * n_conv + j
            rest_tile = 3 * n_conv + 3 * (j - n_conv) + group
            return (layer, jnp.where(j < n_conv, conv_tile, rest_tile), 0)
        return pl.BlockSpec((None, tn, D_MODEL), index)

    conv_col = lambda i, j: jnp.minimum(j, last_conv)
    rest_col = lambda i, j: jnp.maximum(j - n_conv, 0)
    x_spec = pl.BlockSpec(memory_space=pl.ANY)
    xs_spec = pl.BlockSpec((ns, D_MODEL), lambda i, j: (i, 0))
    sconv_spec = pl.BlockSpec((None, ns, CONV_WIDTH - 1, tn), lambda i, j: (layer, i, 0, conv_col(i, j)))
    g_spec = pl.BlockSpec(g.shape, lambda i, j: (0, 0))
    wgt_spec = pl.BlockSpec((None, GATE_RANK, D_MODEL), lambda i, j: (layer, MAIN_COLS // GATE_RANK, 0))
    cw_spec = pl.BlockSpec((None, CONV_WIDTH, tn), lambda i, j: (layer, 0, conv_col(i, j)))
    in_specs = [x_spec, xs_spec, sconv_spec, g_spec, w_spec(0), w_spec(1), w_spec(2), wgt_spec, cw_spec]
    args = [x, xs, state_conv, g, w_in_t, w_in_t, w_in_t, w_in_t, conv_w]
    aliases = _stacked_operand(in_specs, args, stacked, 7)
    out_specs = [pl.BlockSpec((seq, tn), lambda i, j: (i, conv_col(i, j))),
                 pl.BlockSpec((ns, tn), lambda i, j: (i, conv_col(i, j))),
                 pl.BlockSpec((seq, 3 * tn), lambda i, j: (i, rest_col(i, j))),
                 pl.BlockSpec((ns, 3 * tn), lambda i, j: (i, rest_col(i, j))),
                 pl.BlockSpec((seq, GATE_RANK), lambda i, j: (i, 0)),
                 pl.BlockSpec((ns, GATE_RANK), lambda i, j: (i, 0)),
                 pl.BlockSpec((None, CONV_WIDTH - 1, tn), lambda i, j: (i, 0, conv_col(i, j))),
                 sconv_spec]
    out_shape = [jax.ShapeDtypeStruct((m, CONV_DIM), BF16), jax.ShapeDtypeStruct((nb, CONV_DIM), BF16),
                 jax.ShapeDtypeStruct((m, qkvg), BF16), jax.ShapeDtypeStruct((nb, qkvg), F32),
                 jax.ShapeDtypeStruct((m, GATE_RANK), F32), jax.ShapeDtypeStruct((nb, GATE_RANK), F32),
                 jax.ShapeDtypeStruct((batch, CONV_WIDTH - 1, CONV_DIM), F32),
                 jax.ShapeDtypeStruct(state_conv.shape, F32)]
    return pl.pallas_call(
        functools.partial(_mixer_in_kernel, n_conv=n_conv, layer=layer), grid=(batch, n_conv + n_rest),
        in_specs=in_specs, out_specs=out_specs, out_shape=out_shape,
        scratch_shapes=[pltpu.VMEM((seq + ns, D_MODEL), BF16), pltpu.VMEM((seq, D_MODEL), F32),
                        pltpu.SemaphoreType.DMA((1,))],
        input_output_aliases=aliases, compiler_params=_params(2), name="mixer_in")(*args)


def _out_proj_kernel(ya_ref, yb_ref, yas_ref, ybs_ref, wa_ref, wb_ref, x_ref, xs_ref, o_ref, os_ref):
    tm = ya_ref.shape[0]
    ya = jnp.concatenate([ya_ref[...], yas_ref[...]], axis=0)
    yb = jnp.concatenate([yb_ref[...], ybs_ref[...]], axis=0)
    acc = _dot(ya, wa_ref[...].astype(BF16))
    acc = acc + _dot(yb, wb_ref[...].astype(BF16))
    o_ref[...] = x_ref[...] + acc[0:tm]
    os_ref[...] = xs_ref[...] + acc[tm:]


def _out_proj(ya, yb, ya_s, yb_s, w_out, x, xs, layer, tm):
    m, nb = x.shape[0], xs.shape[0]
    n_tiles = m // tm
    ns = nb // n_tiles
    rows = lambda r, cols: pl.BlockSpec((r, cols), lambda i: (i, 0))
    wa_spec = pl.BlockSpec((None, CONV_DIM, D_MODEL), lambda i: (layer, 0, 0), pipeline_mode=pl.Buffered(1))
    wb_spec = pl.BlockSpec((None, GLA_V_DIM, D_MODEL), lambda i: (layer, 1, 0), pipeline_mode=pl.Buffered(1))
    return pl.pallas_call(
        _out_proj_kernel, grid=(n_tiles,),
        in_specs=[rows(tm, CONV_DIM), rows(tm, GLA_V_DIM), rows(ns, CONV_DIM), rows(ns, GLA_V_DIM),
                  wa_spec, wb_spec, rows(tm, D_MODEL), rows(ns, D_MODEL)],
        out_specs=[rows(tm, D_MODEL), rows(ns, D_MODEL)],
        out_shape=[jax.ShapeDtypeStruct((m, D_MODEL), F32), jax.ShapeDtypeStruct((nb, D_MODEL), F32)],
        compiler_params=_params(1), name="out_proj")(ya, yb, ya_s, yb_s, w_out, w_out, x, xs)


def _ffn_kernel(x_hbm, xs_ref, sst_ref, g_ref, wu_ref, wv_ref, cw_ref, cb_ref, wd_ref, fg_ref, *rest,
                tiles_per_seq, final_norm, layer):
    o_ref, os_ref, st_ref, sst_new_ref, xn_ref, prev_ref, carry_ref, xbuf_ref, sem_ref = rest[-9:]
    i = pl.program_id(0)
    f = pl.program_id(1)
    tm = xbuf_ref.shape[0]
    ns = xs_ref.shape[0]
    _prefetched_rows(x_hbm, xbuf_ref, sem_ref, i, f, pl.num_programs(0))

    @pl.when(f == 0)
    def _():
        x = xbuf_ref[...]
        xs = xs_ref[...]
        g = g_ref[layer:layer + 1, :]
        xn_ref[0:tm, :] = _rms(x, g).astype(BF16)
        xn_ref[tm:tm + ns, :] = _rms(xs, g).astype(BF16)
        o_ref[...] = x
        os_ref[...] = xs

    first = lax.rem(i, tiles_per_seq) == 0

    @pl.when(first)
    def _():
        prev_ref[...] = jnp.zeros(prev_ref.shape, F32)

    @pl.when(jnp.logical_not(first))
    def _():
        prev_ref[...] = carry_ref[f]

    xn = xn_ref[...]
    u_all = _dot(xn, wu_ref[...].astype(BF16))
    v_all = _dot(xn, wv_ref[...].astype(BF16))
    u = u_all[0:tm, :]
    u1, u2 = _delayed_rows(u, prev_ref[...])
    cu = u2 * cw_ref[0:1, :]
    cu = cu + u1 * cw_ref[1:2, :]
    cu = cu + u * cw_ref[2:3, :]
    carry_ref[f] = u[tm - 8:tm, :]
    st_ref[...] = u[tm - (CONV_WIDTH - 1):tm, :]
    cu_s = _conv_tap_step(u_all[tm:tm + ns, :], sst_ref, cw_ref, sst_new_ref)

    pre = jnp.concatenate([cu, cu_s], axis=0) + cb_ref[layer:layer + 1, :]
    h = (pre * _sigmoid(pre) * v_all).astype(BF16)
    acc = _dot(h, wd_ref[...].astype(BF16))
    o_ref[...] += acc[0:tm, :]
    os_ref[...] += acc[tm:tm + ns, :]

    if final_norm:
        @pl.when(f == pl.num_programs(1) - 1)
        def _():
            o_ref[...] = _rms(o_ref[...], fg_ref[...])
            os_ref[...] = _rms(os_ref[...], fg_ref[...])


def _ffn(x, xs, state_s, g, w_up, conv_w, conv_b, w_down, final_g, layer, seq, tm, tf, final_norm, stacked):
    m = x.shape[0]
    n_tiles = m // tm
    ns = xs.shape[0] // n_tiles
    tiles_per_seq = seq // tm
    nf = D_FF // tf
    row_spec = pl.BlockSpec((tm, D_MODEL), lambda i, f: (i, 0))
    srow_spec = pl.BlockSpec((ns, D_MODEL), lambda i, f: (i, 0))
    sst_spec = pl.BlockSpec((None, ns, CONV_WIDTH - 1, tf), lambda i, f: (layer, i, 0, f))
    g_spec = pl.BlockSpec(g.shape, lambda i, f: (0, 0))
    wu_spec = pl.BlockSpec((None, D_MODEL, tf), lambda i, f: (layer, 0, f))
    wv_spec = pl.BlockSpec((None, D_MODEL, tf), lambda i, f: (layer, 0, nf + f))
    cw_spec = pl.BlockSpec((None, CONV_WIDTH, tf), lambda i, f: (layer, 0, f))
    cb_spec = pl.BlockSpec((conv_b.shape[0], tf), lambda i, f: (0, f))
    wd_spec = pl.BlockSpec((None, tf, D_MODEL), lambda i, f: (layer, f, 0))
    fg_spec = pl.BlockSpec((1, D_MODEL), lambda i, f: (0, 0))
    st_spec = pl.BlockSpec((None, CONV_WIDTH - 1, tf), lambda i, f: (i, 0, f))
    x_spec = pl.BlockSpec(memory_space=pl.ANY)
    in_specs = [x_spec, srow_spec, sst_spec, g_spec, wu_spec, wv_spec, cw_spec, cb_spec, wd_spec, fg_spec]
    args = [x, xs, state_s, g, w_up, w_up, conv_w, conv_b, w_down, final_g]
    aliases = _stacked_operand(in_specs, args, stacked, 3)
    body = functools.partial(_ffn_kernel, tiles_per_seq=tiles_per_seq, final_norm=final_norm, layer=layer)
    out, out_s, tails, state_new = pl.pallas_call(
        body, grid=(n_tiles, nf), in_specs=in_specs,
        out_specs=[row_spec, srow_spec, st_spec, sst_spec],
        out_shape=[jax.ShapeDtypeStruct((m, D_MODEL), F32),
                   jax.ShapeDtypeStruct(xs.shape, F32),
                   jax.ShapeDtypeStruct((n_tiles, CONV_WIDTH - 1, D_FF), F32),
                   jax.ShapeDtypeStruct(state_s.shape, F32)],
        scratch_shapes=[pltpu.VMEM((tm + ns, D_MODEL), BF16), pltpu.VMEM((8, tf), F32),
                        pltpu.VMEM((nf, 8, tf), F32), pltpu.VMEM((tm, D_MODEL), F32),
                        pltpu.SemaphoreType.DMA((1,))],
        input_output_aliases=aliases, compiler_params=_params(2), name="ffn")(*args)
    return out, out_s, tails[tiles_per_seq - 1::tiles_per_seq], state_new


def _ref_rows(s, chunk):
    rows = np.arange(chunk)
    return (rows // (2 * s)) * (2 * s) + s - 1


def _gla_constants(chunk):
    rows = np.arange(chunk)
    tri = (rows[:, None] >= rows[None, :]).astype(np.float32)
    cums = [tri] + [tri[_ref_rows(s, chunk)] for s in GLA_FINE_LEVELS]
    sgns = [np.where(rows > _ref_rows(s, chunk), 1.0, -1.0) * np.log2(np.e) for s in GLA_LEVELS]
    sgn = np.broadcast_to(np.concatenate(sgns)[:, None], (len(GLA_LEVELS) * chunk, GLA_DK))
    return jnp.asarray(np.concatenate(cums, axis=0), BF16), jnp.asarray(sgn, F32)


def _gla_log_decay(glr, w2, gb):
    gate = _dot(glr.astype(BF16), w2.astype(BF16)) + gb
    return _log_sigmoid(gate) * (1.0 / GATE_TAU)


def _gla_decay_kernel(glr_ref, w2_ref, gb_ref, cum_ref, b_ref, *, layer):
    chunk = GLA_CHUNK
    g = _gla_log_decay(glr_ref[...], w2_ref[...], gb_ref[layer:layer + 1, :])
    cum = cum_ref[...]
    for c in range(glr_ref.shape[0] // chunk):
        rows = slice(c * chunk, (c + 1) * chunk)
        g1, g2, g3 = _split3(g[rows, :])
        cums = _dot(cum, g1) + _dot(cum, g2)
        cums = jnp.concatenate([cums[0:chunk, :] + _dot(cum[0:chunk, :], g3), cums[chunk:, :]], axis=0)
        for part in range(1 + len(GLA_FINE_LEVELS)):
            b_ref[rows, part * GLA_QK_DIM:(part + 1) * GLA_QK_DIM] = cums[part * chunk:(part + 1) * chunk, :]


def _gla_decay(glr, gate_w2, gate_b, cum, layer, tg):
    m = glr.shape[0]
    width = (1 + len(GLA_FINE_LEVELS)) * GLA_QK_DIM
    return pl.pallas_call(
        functools.partial(_gla_decay_kernel, layer=layer), grid=(m // tg,),
        in_specs=[pl.BlockSpec((tg, GATE_RANK), lambda i: (i, 0)),
                  pl.BlockSpec((None, GATE_RANK, GLA_QK_DIM), lambda i: (layer, 0, 0)),
                  pl.BlockSpec(gate_b.shape, lambda i: (0, 0)),
                  pl.BlockSpec(cum.shape, lambda i: (0, 0))],
        out_specs=pl.BlockSpec((tg, width), lambda i: (i, 0)),
        out_shape=jax.ShapeDtypeStruct((m, width), F32),
        compiler_params=_params(1), name="gla_decay")(glr, gate_w2, gate_b, cum)


def _gla_out(o, gn, og):
    var = jnp.mean(o * o, axis=-1, keepdims=True)
    return o * lax.rsqrt(var + EPS) * gn * (og * _sigmoid(og))


def _gla_prompt_kernel(q_ref, k_ref, v_ref, og_ref, b_ref, bf0_ref, bf1_ref, bh0_ref, bh1_ref, bh2_ref, bh3_ref,
                       gn_ref, sgn_ref, qs_ref, ks_ref, vs_ref, ogs_ref, glrs_ref, w2_ref, gb_ref, sst_ref, *rest,
                       layer):
    yb_ref, sfin_ref, ybs_ref, sst_new_ref, s_ref = rest[-5:]
    gn = gn_ref[layer:layer + 1, :]
    gb = gb_ref[layer:layer + 1, :]
    bf_refs = (bf0_ref, bf1_ref)
    bh_refs = (bh0_ref, bh1_ref, bh2_ref, bh3_ref)
    c_idx = pl.program_id(1)
    n_seq, chunk = q_ref.shape[0], q_ref.shape[1]

    @pl.when(c_idx == 0)
    def _():
        s_ref[...] = jnp.zeros(s_ref.shape, F32)

    row = lax.broadcasted_iota(jnp.int32, (chunk, chunk), 0)
    col = lax.broadcasted_iota(jnp.int32, (chunk, chunk), 1)
    diff_bits = row ^ col

    for n, h in [(n, h) for n in range(n_seq) for h in range(GLA_HEADS)]:
        ks = slice(h * GLA_DK, (h + 1) * GLA_DK)
        vs = slice(h * GLA_DV, (h + 1) * GLA_DV)
        b = b_ref[n, :, ks]
        b_last = b_ref[n, chunk - 1:chunk, ks]
        q = q_ref[n, :, ks]
        k = k_ref[n, :, ks]
        v = v_ref[n, :, vs]
        s_prev = s_ref[n, h]

        inter = _dot(q * jnp.exp(b).astype(BF16), s_prev.astype(BF16))

        acc = _dot_nt(q, k)
        for li, s in enumerate(GLA_LEVELS):
            if s in GLA_FINE_LEVELS:
                beta = bf_refs[GLA_FINE_LEVELS.index(s)][n, :, ks]
            else:
                beta = jnp.concatenate(
                    [bh_refs[h][n, pl.ds(int(r), 8, stride=0), :]
                     for r in _ref_rows(s, chunk)[::2 * s] for _ in range(2 * s // 8)], axis=0)
            e = jnp.exp2((b - beta) * sgn_ref[li * chunk:(li + 1) * chunk, :]).astype(BF16)
            p = _dot_nt(q * e, k * e)
            acc = jnp.where(diff_bits >= s, p, acc)
        a = jnp.where(row >= col, acc, 0.0)
        o = (inter + _dot(a.astype(BF16), v)) * (GLA_DK ** -0.5)

        kdec_t = (k.astype(F32) * jnp.exp(b_last - b)).T
        decay_t = jnp.broadcast_to(jnp.exp(b_last), (GLA_DK, GLA_DK)).T
        s_new = jnp.concatenate([decay_t, decay_t], axis=1) * s_prev + _dot(kdec_t.astype(BF16), v)
        s_ref[n, h] = s_new

        yb_ref[n, :, vs] = _gla_out(o, gn[:, vs], og_ref[n, :, vs].astype(F32)).astype(BF16)

    @pl.when(c_idx == pl.num_programs(1) - 1)
    def _():
        sfin_ref[...] = s_ref[...]

    _gla_step_body(qs_ref, ks_ref, vs_ref, ogs_ref, glrs_ref, w2_ref, gb, gn, sst_ref, ybs_ref, sst_new_ref)


def _gla(qkvg, glr, qkvg_s, glr_s, state_s, gate_w2, gate_b, gla_norm_g, layer, batch, seq, stacked):
    n_seq = batch
    chunk = GLA_CHUNK
    nc = seq // chunk
    cum, sgn = _gla_constants(chunk)
    b_all = _gla_decay(glr, gate_w2, gate_b, cum, layer, GLA_DECAY_ROWS).reshape(batch, seq, -1)
    qkvg = qkvg.reshape(batch, seq, -1)

    def col_spec(width, blk):
        return pl.BlockSpec((n_seq, chunk, width), lambda b, c: (b, c, blk))

    gn_spec = pl.BlockSpec(gla_norm_g.shape, lambda b, c: (0, 0))
    sgn_spec = pl.BlockSpec(sgn.shape, lambda b, c: (0, 0))
    yb_spec = pl.BlockSpec((n_seq, chunk, GLA_V_DIM), lambda b, c: (b, c, 0))
    s_spec = pl.BlockSpec((n_seq, GLA_HEADS, GLA_DK, GLA_DV), lambda b, c: (b, 0, 0, 0))
    nb = qkvg_s.shape[0]
    rows = nb // nc

    def srow_spec(width, blk):
        return pl.BlockSpec((rows, width), lambda b, c: (c, blk))

    sst_spec = pl.BlockSpec((None, rows, GLA_HEADS, GLA_DK, GLA_DV), lambda b, c: (layer, c, 0, 0, 0))
    in_specs = ([col_spec(GLA_QK_DIM, 0), col_spec(GLA_QK_DIM, 1), col_spec(GLA_V_DIM, 1), col_spec(GLA_V_DIM, 2)]
                + [col_spec(GLA_QK_DIM, part) for part in range(1 + len(GLA_FINE_LEVELS))]
                + [col_spec(GLA_DK, h) for h in range(GLA_HEADS)] + [gn_spec, sgn_spec]
                + [srow_spec(GLA_QK_DIM, 0), srow_spec(GLA_QK_DIM, 1), srow_spec(GLA_V_DIM, 1),
                   srow_spec(GLA_V_DIM, 2), srow_spec(GATE_RANK, 0),
                   pl.BlockSpec((None, GATE_RANK, GLA_QK_DIM), lambda b, c: (layer, 0, 0)),
                   pl.BlockSpec(gate_b.shape, lambda b, c: (0, 0)), sst_spec])
    args = [qkvg, qkvg, qkvg, qkvg, *([b_all] * (1 + len(GLA_FINE_LEVELS) + GLA_HEADS)), gla_norm_g, sgn,
            qkvg_s, qkvg_s, qkvg_s, qkvg_s, glr_s, gate_w2, gate_b, state_s]
    aliases = _stacked_operand(in_specs, args, stacked, 3)
    yb, s_fin, yb_s, state_new = pl.pallas_call(
        functools.partial(_gla_prompt_kernel, layer=layer), grid=(1, nc), in_specs=in_specs,
        out_specs=[yb_spec, s_spec, srow_spec(GLA_V_DIM, 0), sst_spec],
        out_shape=[jax.ShapeDtypeStruct((batch, seq, GLA_V_DIM), BF16),
                   jax.ShapeDtypeStruct((batch, GLA_HEADS, GLA_DK, GLA_DV), F32),
                   jax.ShapeDtypeStruct((nb, GLA_V_DIM), BF16),
                   jax.ShapeDtypeStruct(state_s.shape, F32)],
        scratch_shapes=[pltpu.VMEM((n_seq, GLA_HEADS, GLA_DK, GLA_DV), F32)],
        input_output_aliases=aliases, compiler_params=_params(2), name="gla")(*args)
    return yb.reshape(batch * seq, GLA_V_DIM), s_fin, yb_s, state_new


def _columns(slab):
    pad = jnp.zeros((GLA_DK - slab.shape[0], GLA_DK), F32)
    return jnp.concatenate([slab, pad], axis=0).T


def _gla_step_body(q_ref, k_ref, v_ref, og_ref, glr_ref, w2_ref, gb, gn, s_ref, yb_ref, snew_ref):
    rows = q_ref.shape[0]
    decay_all = jnp.exp(_gla_log_decay(glr_ref[...], w2_ref[...], gb))
    row_id = lax.broadcasted_iota(jnp.int32, (rows, rows * GLA_DV), 0)
    lane_blk = lax.broadcasted_iota(jnp.int32, (rows, rows * GLA_DV), 1) // GLA_DV

    for h in range(GLA_HEADS):
        ks = slice(h * GLA_DK, (h + 1) * GLA_DK)
        vs = slice(h * GLA_DV, (h + 1) * GLA_DV)
        decay = decay_all[:, ks]
        q = q_ref[:, ks] * (GLA_DK ** -0.5)
        k = k_ref[:, ks]
        v = v_ref[:, vs]
        qk = jnp.sum(q * k, axis=-1, keepdims=True)

        v_diag = jnp.where(row_id == lane_blk, jnp.concatenate([v] * rows, axis=1), 0.0)
        v_diag = jnp.concatenate([v_diag, jnp.zeros_like(v_diag)], axis=0)
        k_t = _columns(k)[:, 0:2 * rows]
        kv = _dot(k_t.astype(BF16), v_diag.astype(BF16))

        decay_c = _columns(decay)
        qd = (q * decay).astype(BF16)
        inter_rows = []
        for i in range(rows):
            s_prev = s_ref[i, h]
            inter_rows.append(_dot(qd, s_prev.astype(BF16))[i:i + 1, :])
            snew_ref[i, h] = decay_c[:, i:i + 1] * s_prev + kv[:, i * GLA_DV:(i + 1) * GLA_DV]
        o = jnp.concatenate(inter_rows, axis=0) + qk * v
        yb_ref[:, vs] = _gla_out(o, gn[:, vs], og_ref[:, vs]).astype(BF16)


def kernel(x_prompt, x_sample, state_conv, state_gla, state_ffn_conv, norm_mix_g, w_in, conv_w, gate_w2, gate_b, gla_norm_g, w_out, norm_ffn_g, w_up, ffn_conv_w, ffn_conv_b, w_down, final_norm_g):
    batch, seq, _ = x_prompt.shape
    nb = x_sample.shape[0]
    final_norm_g = final_norm_g[None, :]
    w_in_t = jnp.swapaxes(w_in, 1, 2)

    xp = x_prompt.reshape(batch * seq, D_MODEL)
    xs = x_sample.reshape(nb, D_MODEL)
    conv_p, gla_p, ffn_p = [], [], []
    conv_s, gla_s, ffn_s = None, None, None
    for l in range(DEPTH):
        ya, ya_s, qkvg, qkvg_s, glr, glr_s, c_new, conv_s = _mixer_in(
            xp, xs, state_conv, norm_mix_g, w_in_t, conv_w, l, seq, MIXER_IN_TN, conv_s)

        yb, s_new, yb_s, gla_s = _gla(qkvg, glr, qkvg_s, glr_s, state_gla, gate_w2, gate_b, gla_norm_g,
                                      l, batch, seq, gla_s)
        xp, xs = _out_proj(ya, yb, ya_s, yb_s, w_out, xp, xs, l, OUT_PROJ_TM)

        xp, xs, f_new, ffn_s = _ffn(xp, xs, state_ffn_conv, norm_ffn_g, w_up, ffn_conv_w, ffn_conv_b, w_down,
                                    final_norm_g, l, seq, FFN_TM, FFN_TF, l == DEPTH - 1, ffn_s)
        conv_p.append(c_new)
        gla_p.append(s_new)
        ffn_p.append(f_new)

    return (xp.reshape(batch, seq, D_MODEL), xs.reshape(nb, 1, D_MODEL),
            jnp.stack(conv_p), jnp.stack(gla_p), jnp.stack(ffn_p), conv_s, gla_s, ffn_s)
```

```python
import functools

import numpy as np
import jax
import jax.numpy as jnp
from jax import lax
from jax.experimental import pallas as pl
from jax.experimental.pallas import tpu as pltpu

F32 = jnp.float32
BF16 = jnp.bfloat16

D_MODEL = 2048
DEPTH = 2
CONV_DIM = 1024
CONV_WIDTH = 3
GLA_HEADS = 4
GLA_QK_DIM = 512
GLA_V_DIM = 1024
GLA_DK = 128
GLA_DV = 256
GATE_RANK = 16
GATE_TAU = 16.0
D_FF = 5632
EPS = 1e-6
MAIN_COLS = 3 * CONV_DIM + 2 * GLA_QK_DIM + 2 * GLA_V_DIM

GLA_CHUNK = 128
GLA_LEVELS = tuple(1 << i for i in range(GLA_CHUNK.bit_length() - 1))
GLA_FINE_LEVELS = tuple(s for s in GLA_LEVELS if s < 4)
GLA_DECAY_ROWS = 8 * GLA_CHUNK

MIXER_IN_TN = 256
OUT_PROJ_TM = 512
FFN_TM, FFN_TF = 1024, 512

VMEM_BYTES_V7X = 64 * 1024 * 1024
VMEM_LIMIT = VMEM_BYTES_V7X - 2 * 1024 * 1024


def _params(n_axes):
    return pltpu.CompilerParams(dimension_semantics=("arbitrary",) * n_axes,
                                vmem_limit_bytes=VMEM_LIMIT)


def _dot(a, b):
    return jnp.dot(a, b, preferred_element_type=F32)


def _dot_nt(a, b):
    return lax.dot_general(a, b, (((1,), (1,)), ((), ())), preferred_element_type=F32)


def _split3(x):
    hi = x.astype(BF16)
    r1 = x - hi.astype(F32)
    mid = r1.astype(BF16)
    lo = (r1 - mid.astype(F32)).astype(BF16)
    return hi, mid, lo


def _exact_dot01(m01, x):
    x1, x2, x3 = _split3(x)
    return _dot(m01, x1) + _dot(m01, x2) + _dot(m01, x3)


def _sigmoid(x):
    return 1.0 / (1.0 + jnp.exp(-x))


def _log_sigmoid(x):
    return jnp.minimum(x, 0.0) - jnp.log1p(jnp.exp(-jnp.abs(x)))


def _rms(x, g):
    var = jnp.mean(x * x, axis=-1, keepdims=True)
    return x * lax.rsqrt(var + EPS) * g


def _delayed_rows(u, prev):
    sub = lax.broadcasted_iota(jnp.int32, prev.shape, 0)
    r1 = pltpu.roll(u, 1, axis=0)
    r2 = pltpu.roll(u, 2, axis=0)
    head1 = jnp.where(sub < 1, pltpu.roll(prev, 1, axis=0), r1[0:8, :])
    head2 = jnp.where(sub < 2, pltpu.roll(prev, 2, axis=0), r2[0:8, :])
    return (jnp.concatenate([head1, r1[8:, :]], axis=0), jnp.concatenate([head2, r2[8:, :]], axis=0))


def _prefetched_rows(x_hbm, xbuf_ref, sem_ref, tile, step, n_tiles):
    rows = xbuf_ref.shape[0]

    def copy(t):
        return pltpu.make_async_copy(x_hbm.at[pl.ds(t * rows, rows), :], xbuf_ref, sem_ref.at[0])

    @pl.when(jnp.logical_and(tile == 0, step == 0))
    def _():
        copy(0).start()

    @pl.when(step == 0)
    def _():
        copy(tile).wait()

    @pl.when(jnp.logical_and(step == 1, tile + 1 < n_tiles))
    def _():
        copy(tile + 1).start(priority=1)


def _stacked_operand(in_specs, args, stacked, out_index):
    if stacked is None:
        return {}
    in_specs.append(pl.BlockSpec(memory_space=pl.ANY))
    args.append(stacked)
    return {len(args) - 1: out_index}


def _conv_tap_step(new_row, st_ref, w_ref, new_st_ref):
    p1 = st_ref[:, 1, :]
    cu = st_ref[:, 0, :] * w_ref[0:1, :]
    cu = cu + p1 * w_ref[1:2, :]
    cu = cu + new_row * w_ref[2:3, :]
    new_st_ref[:, 0, :] = p1
    new_st_ref[:, 1, :] = new_row
    return cu


def _mixer_in_kernel(x_hbm, xs_ref, sconv_ref, g_ref, wa_ref, wb_ref, wc_ref, wgt_ref, cw_ref, *rest, n_conv, layer):
    (ya_ref, yas_ref, o_ref, os_ref, og_ref, ogs_ref, st_ref, sconv_new_ref,
     xn_ref, xbuf_ref, sem_ref) = rest[-11:]
    j = pl.program_id(1)
    tm = xbuf_ref.shape[0]
    ns = xs_ref.shape[0]
    tn = wa_ref.shape[0]
    _prefetched_rows(x_hbm, xbuf_ref, sem_ref, pl.program_id(0), j, pl.num_programs(0))

    @pl.when(j == 0)
    def _():
        g = g_ref[layer:layer + 1, :]
        xn_ref[0:tm, :] = _rms(xbuf_ref[...], g).astype(BF16)
        xn_ref[tm:tm + ns, :] = _rms(xs_ref[...], g).astype(BF16)
        gate = _dot_nt(xn_ref[...], wgt_ref[...].astype(BF16))
        og_ref[...] = gate[0:tm, :]
        ogs_ref[...] = gate[tm:tm + ns, :]

    def projected():
        xn = xn_ref[...]
        return [_dot_nt(xn, w_ref[...].astype(BF16)) for w_ref in (wa_ref, wb_ref, wc_ref)]

    @pl.when(j < n_conv)
    def _():
        bg, cg, hin = projected()
        z = cg[0:tm, :] * hin[0:tm, :]
        z1, z2 = _delayed_rows(z, jnp.zeros((8, tn), F32))
        cu = z2 * cw_ref[0:1, :]
        cu = cu + z1 * cw_ref[1:2, :]
        cu = cu + z * cw_ref[2:3, :]
        ya_ref[...] = (bg[0:tm, :] * cu).astype(BF16)
        st_ref[...] = z[tm - (CONV_WIDTH - 1):tm, :]
        cu_s = _conv_tap_step(cg[tm:tm + ns, :] * hin[tm:tm + ns, :], sconv_ref, cw_ref, sconv_new_ref)
        yas_ref[...] = (bg[tm:tm + ns, :] * cu_s).astype(BF16)

    @pl.when(j >= n_conv)
    def _():
        for idx, tile in enumerate(projected()):
            cols = slice(idx * tn, (idx + 1) * tn)
            o_ref[:, cols] = tile[0:tm, :].astype(BF16)
            os_ref[:, cols] = tile[tm:tm + ns, :]


def _mixer_in(x, xs, state_conv, g, w_in_t, conv_w, layer, seq, tn, stacked):
    m, nb = x.shape[0], xs.shape[0]
    batch = m // seq
    ns = nb // batch
    n_conv = CONV_DIM // tn
    qkvg = MAIN_COLS - 3 * CONV_DIM
    n_rest = qkvg // (3 * tn)
    last_conv = n_conv - 1

    def w_spec(group):
        def index(i, j):
            conv_tile = group * n_conv + j
            rest_tile = 3 * n_conv + 3 * (j - n_conv) + group
            return (layer, jnp.where(j < n_conv, conv_tile, rest_tile), 0)
        return pl.BlockSpec((None, tn, D_MODEL), index)

    conv_col = lambda i, j: jnp.minimum(j, last_conv)
    rest_col = lambda i, j: jnp.maximum(j - n_conv, 0)
    x_spec = pl.BlockSpec(memory_space=pl.ANY)
    xs_spec = pl.BlockSpec((ns, D_MODEL), lambda i, j: (i, 0))
    sconv_spec = pl.BlockSpec((None, ns, CONV_WIDTH - 1, tn), lambda i, j: (layer, i, 0, conv_col(i, j)))
    g_spec = pl.BlockSpec(g.shape, lambda i, j: (0, 0))
    wgt_spec = pl.BlockSpec((None, GATE_RANK, D_MODEL), lambda i, j: (layer, MAIN_COLS // GATE_RANK, 0))
    cw_spec = pl.BlockSpec((None, CONV_WIDTH, tn), lambda i, j: (layer, 0, conv_col(i, j)))
    in_specs = [x_spec, xs_spec, sconv_spec, g_spec, w_spec(0), w_spec(1), w_spec(2), wgt_spec, cw_spec]
    args = [x, xs, state_conv, g, w_in_t, w_in_t, w_in_t, w_in_t, conv_w]
    aliases = _stacked_operand(in_specs, args, stacked, 7)
    out_specs = [pl.BlockSpec((seq, tn), lambda i, j: (i, conv_col(i, j))),
                 pl.BlockSpec((ns, tn), lambda i, j: (i, conv_col(i, j))),
                 pl.BlockSpec((seq, 3 * tn), lambda i, j: (i, rest_col(i, j))),
                 pl.BlockSpec((ns, 3 * tn), lambda i, j: (i, rest_col(i, j))),
                 pl.BlockSpec((seq, GATE_RANK), lambda i, j: (i, 0)),
                 pl.BlockSpec((ns, GATE_RANK), lambda i, j: (i, 0)),
                 pl.BlockSpec((None, CONV_WIDTH - 1, tn), lambda i, j: (i, 0, conv_col(i, j))),
                 sconv_spec]
    out_shape = [jax.ShapeDtypeStruct((m, CONV_DIM), BF16), jax.ShapeDtypeStruct((nb, CONV_DIM), BF16),
                 jax.ShapeDtypeStruct((m, qkvg), BF16), jax.ShapeDtypeStruct((nb, qkvg), F32),
                 jax.ShapeDtypeStruct((m, GATE_RANK), F32), jax.ShapeDtypeStruct((nb, GATE_RANK), F32),
                 jax.ShapeDtypeStruct((batch, CONV_WIDTH - 1, CONV_DIM), F32),
                 jax.ShapeDtypeStruct(state_conv.shape, F32)]
    return pl.pallas_call(
        functools.partial(_mixer_in_kernel, n_conv=n_conv, layer=layer), grid=(batch, n_conv + n_rest),
        in_specs=in_specs, out_specs=out_specs, out_shape=out_shape,
        scratch_shapes=[pltpu.VMEM((seq + ns, D_MODEL), BF16), pltpu.VMEM((seq, D_MODEL), F32),
                        pltpu.SemaphoreType.DMA((1,))],
        input_output_aliases=aliases, compiler_params=_params(2), name="mixer_in")(*args)


def _out_proj_kernel(ya_ref, yb_ref, yas_ref, ybs_ref, wa_ref, wb_ref, x_ref, xs_ref, o_ref, os_ref):
    tm = ya_ref.shape[0]
    ya = jnp.concatenate([ya_ref[...], yas_ref[...]], axis=0)
    yb = jnp.concatenate([yb_ref[...], ybs_ref[...]], axis=0)
    acc = _dot(ya, wa_ref[...].astype(BF16))
    acc = acc + _dot(yb, wb_ref[...].astype(BF16))
    o_ref[...] = x_ref[...] + acc[0:tm]
    os_ref[...] = xs_ref[...] + acc[tm:]


def _out_proj(ya, yb, ya_s, yb_s, w_out, x, xs, layer, tm):
    m, nb = x.shape[0], xs.shape[0]
    n_tiles = m // tm
    ns = nb // n_tiles
    rows = lambda r, cols: pl.BlockSpec((r, cols), lambda i: (i, 0))
    wa_spec = pl.BlockSpec((None, CONV_DIM, D_MODEL), lambda i: (layer, 0, 0), pipeline_mode=pl.Buffered(1))
    wb_spec = pl.BlockSpec((None, GLA_V_DIM, D_MODEL), lambda i: (layer, 1, 0), pipeline_mode=pl.Buffered(1))
    return pl.pallas_call(
        _out_proj_kernel, grid=(n_tiles,),
        in_specs=[rows(tm, CONV_DIM), rows(tm, GLA_V_DIM), rows(ns, CONV_DIM), rows(ns, GLA_V_DIM),
                  wa_spec, wb_spec, rows(tm, D_MODEL), rows(ns, D_MODEL)],
        out_specs=[rows(tm, D_MODEL), rows(ns, D_MODEL)],
        out_shape=[jax.ShapeDtypeStruct((m, D_MODEL), F32), jax.ShapeDtypeStruct((nb, D_MODEL), F32)],
        compiler_params=_params(1), name="out_proj")(ya, yb, ya_s, yb_s, w_out, w_out, x, xs)


def _ffn_kernel(x_hbm, xs_ref, sst_ref, g_ref, wu_ref, wv_ref, cw_ref, cb_ref, wd_ref, fg_ref, *rest,
                tiles_per_seq, final_norm, layer):
    o_ref, os_ref, st_ref, sst_new_ref, xn_ref, prev_ref, carry_ref, xbuf_ref, sem_ref = rest[-9:]
    i = pl.program_id(0)
    f = pl.program_id(1)
    tm = xbuf_ref.shape[0]
    ns = xs_ref.shape[0]
    _prefetched_rows(x_hbm, xbuf_ref, sem_ref, i, f, pl.num_programs(0))

    @pl.when(f == 0)
    def _():
        x = xbuf_ref[...]
        xs = xs_ref[...]
        g = g_ref[layer:layer + 1, :]
        xn_ref[0:tm, :] = _rms(x, g).astype(BF16)
        xn_ref[tm:tm + ns, :] = _rms(xs, g).astype(BF16)
        o_ref[...] = x
        os_ref[...] = xs

    first = lax.rem(i, tiles_per_seq) == 0

    @pl.when(first)
    def _():
        prev_ref[...] = jnp.zeros(prev_ref.shape, F32)

    @pl.when(jnp.logical_not(first))
    def _():
        prev_ref[...] = carry_ref[f]

    xn = xn_ref[...]
    u_all = _dot(xn, wu_ref[...].astype(BF16))
    v_all = _dot(xn, wv_ref[...].astype(BF16))
    u = u_all[0:tm, :]
    u1, u2 = _delayed_rows(u, prev_ref[...])
    cu = u2 * cw_ref[0:1, :]
    cu = cu + u1 * cw_ref[1:2, :]
    cu = cu + u * cw_ref[2:3, :]
    carry_ref[f] = u[tm - 8:tm, :]
    st_ref[...] = u[tm - (CONV_WIDTH - 1):tm, :]
    cu_s = _conv_tap_step(u_all[tm:tm + ns, :], sst_ref, cw_ref, sst_new_ref)

    pre = jnp.concatenate([cu, cu_s], axis=0) + cb_ref[layer:layer + 1, :]
    h = (pre * _sigmoid(pre) * v_all).astype(BF16)
    acc = _dot(h, wd_ref[...].astype(BF16))
    o_ref[...] += acc[0:tm, :]
    os_ref[...] += acc[tm:tm + ns, :]

    if final_norm:
        @pl.when(f == pl.num_programs(1) - 1)
        def _():
            o_ref[...] = _rms(o_ref[...], fg_ref[...])
            os_ref[...] = _rms(os_ref[...], fg_ref[...])


def _ffn(x, xs, state_s, g, w_up, conv_w, conv_b, w_down, final_g, layer, seq, tm, tf, final_norm, stacked):
    m = x.shape[0]
    n_tiles = m // tm
    ns = xs.shape[0] // n_tiles
    tiles_per_seq = seq // tm
    nf = D_FF // tf
    row_spec = pl.BlockSpec((tm, D_MODEL), lambda i, f: (i, 0))
    srow_spec = pl.BlockSpec((ns, D_MODEL), lambda i, f: (i, 0))
    sst_spec = pl.BlockSpec((None, ns, CONV_WIDTH - 1, tf), lambda i, f: (layer, i, 0, f))
    g_spec = pl.BlockSpec(g.shape, lambda i, f: (0, 0))
    wu_spec = pl.BlockSpec((None, D_MODEL, tf), lambda i, f: (layer, 0, f))
    wv_spec = pl.BlockSpec((None, D_MODEL, tf), lambda i, f: (layer, 0, nf + f))
    cw_spec = pl.BlockSpec((None, CONV_WIDTH, tf), lambda i, f: (layer, 0, f))
    cb_spec = pl.BlockSpec((conv_b.shape[0], tf), lambda i, f: (0, f))
    wd_spec = pl.BlockSpec((None, tf, D_MODEL), lambda i, f: (layer, f, 0))
    fg_spec = pl.BlockSpec((1, D_MODEL), lambda i, f: (0, 0))
    st_spec = pl.BlockSpec((None, CONV_WIDTH - 1, tf), lambda i, f: (i, 0, f))
    x_spec = pl.BlockSpec(memory_space=pl.ANY)
    in_specs = [x_spec, srow_spec, sst_spec, g_spec, wu_spec, wv_spec, cw_spec, cb_spec, wd_spec, fg_spec]
    args = [x, xs, state_s, g, w_up, w_up, conv_w, conv_b, w_down, final_g]
    aliases = _stacked_operand(in_specs, args, stacked, 3)
    body = functools.partial(_ffn_kernel, tiles_per_seq=tiles_per_seq, final_norm=final_norm, layer=layer)
    out, out_s, tails, state_new = pl.pallas_call(
        body, grid=(n_tiles, nf), in_specs=in_specs,
        out_specs=[row_spec, srow_spec, st_spec, sst_spec],
        out_shape=[jax.ShapeDtypeStruct((m, D_MODEL), F32),
                   jax.ShapeDtypeStruct(xs.shape, F32),
                   jax.ShapeDtypeStruct((n_tiles, CONV_WIDTH - 1, D_FF), F32),
                   jax.ShapeDtypeStruct(state_s.shape, F32)],
        scratch_shapes=[pltpu.VMEM((tm + ns, D_MODEL), BF16), pltpu.VMEM((8, tf), F32),
                        pltpu.VMEM((nf, 8, tf), F32), pltpu.VMEM((tm, D_MODEL), F32),
                        pltpu.SemaphoreType.DMA((1,))],
        input_output_aliases=aliases, compiler_params=_params(2), name="ffn")(*args)
    return out, out_s, tails[tiles_per_seq - 1::tiles_per_seq], state_new


def _ref_rows(s, chunk):
    rows = np.arange(chunk)
    return (rows // (2 * s)) * (2 * s) + s - 1


def _gla_constants(chunk):
    rows = np.arange(chunk)
    tri = (rows[:, None] >= rows[None, :]).astype(np.float32)
    cums = [tri] + [tri[_ref_rows(s, chunk)] for s in GLA_FINE_LEVELS]
    sgns = [np.where(rows > _ref_rows(s, chunk), 1.0, -1.0) * np.log2(np.e) for s in GLA_LEVELS]
    sgn = np.broadcast_to(np.concatenate(sgns)[:, None], (len(GLA_LEVELS) * chunk, GLA_DK))
    return jnp.asarray(np.concatenate(cums, axis=0), BF16), jnp.asarray(sgn, F32)


def _gla_log_decay(glr, w2, gb):
    gate = _dot(glr.astype(BF16), w2.astype(BF16)) + gb
    return _log_sigmoid(gate) * (1.0 / GATE_TAU)


def _gla_decay_kernel(glr_ref, w2_ref, gb_ref, cum_ref, b_ref, *, layer):
    chunk = GLA_CHUNK
    g = _gla_log_decay(glr_ref[...], w2_ref[...], gb_ref[layer:layer + 1, :])
    cum = cum_ref[...]
    for c in range(glr_ref.shape[0] // chunk):
        rows = slice(c * chunk, (c + 1) * chunk)
        g1, g2, g3 = _split3(g[rows, :])
        cums = _dot(cum, g1) + _dot(cum, g2)
        cums = jnp.concatenate([cums[0:chunk, :] + _dot(cum[0:chunk, :], g3), cums[chunk:, :]], axis=0)
        for part in range(1 + len(GLA_FINE_LEVELS)):
            b_ref[rows, part * GLA_QK_DIM:(part + 1) * GLA_QK_DIM] = cums[part * chunk:(part + 1) * chunk, :]


def _gla_decay(glr, gate_w2, gate_b, cum, layer, tg):
    m = glr.shape[0]
    width = (1 + len(GLA_FINE_LEVELS)) * GLA_QK_DIM
    return pl.pallas_call(
        functools.partial(_gla_decay_kernel, layer=layer), grid=(m // tg,),
        in_specs=[pl.BlockSpec((tg, GATE_RANK), lambda i: (i, 0)),
                  pl.BlockSpec((None, GATE_RANK, GLA_QK_DIM), lambda i: (layer, 0, 0)),
                  pl.BlockSpec(gate_b.shape, lambda i: (0, 0)),
                  pl.BlockSpec(cum.shape, lambda i: (0, 0))],
        out_specs=pl.BlockSpec((tg, width), lambda i: (i, 0)),
        out_shape=jax.ShapeDtypeStruct((m, width), F32),
        compiler_params=_params(1), name="gla_decay")(glr, gate_w2, gate_b, cum)


def _gla_out(o, gn, og):
    var = jnp.mean(o * o, axis=-1, keepdims=True)
    return o * lax.rsqrt(var + EPS) * gn * (og * _sigmoid(og))


def _gla_prompt_kernel(q_ref, k_ref, v_ref, og_ref, b_ref, bf0_ref, bf1_ref, bh0_ref, bh1_ref, bh2_ref, bh3_ref,
                       gn_ref, sgn_ref, qs_ref, ks_ref, vs_ref, ogs_ref, glrs_ref, w2_ref, gb_ref, sst_ref, *rest,
                       layer):
    yb_ref, sfin_ref, ybs_ref, sst_new_ref, s_ref = rest[-5:]
    gn = gn_ref[layer:layer + 1, :]
    gb = gb_ref[layer:layer + 1, :]
    bf_refs = (bf0_ref, bf1_ref)
    bh_refs = (bh0_ref, bh1_ref, bh2_ref, bh3_ref)
    c_idx = pl.program_id(1)
    n_seq, chunk = q_ref.shape[0], q_ref.shape[1]

    @pl.when(c_idx == 0)
    def _():
        s_ref[...] = jnp.zeros(s_ref.shape, F32)

    row = lax.broadcasted_iota(jnp.int32, (chunk, chunk), 0)
    col = lax.broadcasted_iota(jnp.int32, (chunk, chunk), 1)
    diff_bits = row ^ col

    for n, h in [(n, h) for n in range(n_seq) for h in range(GLA_HEADS)]:
        ks = slice(h * GLA_DK, (h + 1) * GLA_DK)
        vs = slice(h * GLA_DV, (h + 1) * GLA_DV)
        b = b_ref[n, :, ks]
        b_last = b_ref[n, chunk - 1:chunk, ks]
        q = q_ref[n, :, ks]
        k = k_ref[n, :, ks]
        v = v_ref[n, :, vs]
        s_prev = s_ref[n, h]

        inter = _dot(q * jnp.exp(b).astype(BF16), s_prev.astype(BF16))

        acc = _dot_nt(q, k)
        for li, s in enumerate(GLA_LEVELS):
            if s in GLA_FINE_LEVELS:
                beta = bf_refs[GLA_FINE_LEVELS.index(s)][n, :, ks]
            else:
                beta = jnp.concatenate(
                    [bh_refs[h][n, pl.ds(int(r), 8, stride=0), :]
                     for r in _ref_rows(s, chunk)[::2 * s] for _ in range(2 * s // 8)], axis=0)
            e = jnp.exp2((b - beta) * sgn_ref[li * chunk:(li + 1) * chunk, :]).astype(BF16)
            p = _dot_nt(q * e, k * e)
            acc = jnp.where(diff_bits >= s, p, acc)
        a = jnp.where(row >= col, acc, 0.0)
        o = (inter + _dot(a.astype(BF16), v)) * (GLA_DK ** -0.5)

        kdec_t = (k.astype(F32) * jnp.exp(b_last - b)).T
        decay_t = jnp.broadcast_to(jnp.exp(b_last), (GLA_DK, GLA_DK)).T
        s_new = jnp.concatenate([decay_t, decay_t], axis=1) * s_prev + _dot(kdec_t.astype(BF16), v)
        s_ref[n, h] = s_new

        yb_ref[n, :, vs] = _gla_out(o, gn[:, vs], og_ref[n, :, vs].astype(F32)).astype(BF16)

    @pl.when(c_idx == pl.num_programs(1) - 1)
    def _():
        sfin_ref[...] = s_ref[...]

    _gla_step_body(qs_ref, ks_ref, vs_ref, ogs_ref, glrs_ref, w2_ref, gb, gn, sst_ref, ybs_ref, sst_new_ref)


def _gla(qkvg, glr, qkvg_s, glr_s, state_s, gate_w2, gate_b, gla_norm_g, layer, batch, seq, stacked):
    n_seq = batch
    chunk = GLA_CHUNK
    nc = seq // chunk
    cum, sgn = _gla_constants(chunk)
    b_all = _gla_decay(glr, gate_w2, gate_b, cum, layer, GLA_DECAY_ROWS).reshape(batch, seq, -1)
    qkvg = qkvg.reshape(batch, seq, -1)

    def col_spec(width, blk):
        return pl.BlockSpec((n_seq, chunk, width), lambda b, c: (b, c, blk))

    gn_spec = pl.BlockSpec(gla_norm_g.shape, lambda b, c: (0, 0))
    sgn_spec = pl.BlockSpec(sgn.shape, lambda b, c: (0, 0))
    yb_spec = pl.BlockSpec((n_seq, chunk, GLA_V_DIM), lambda b, c: (b, c, 0))
    s_spec = pl.BlockSpec((n_seq, GLA_HEADS, GLA_DK, GLA_DV), lambda b, c: (b, 0, 0, 0))
    nb = qkvg_s.shape[0]
    rows = nb // nc

    def srow_spec(width, blk):
        return pl.BlockSpec((rows, width), lambda b, c: (c, blk))

    sst_spec = pl.BlockSpec((None, rows, GLA_HEADS, GLA_DK, GLA_DV), lambda b, c: (layer, c, 0, 0, 0))
    in_specs = ([col_spec(GLA_QK_DIM, 0), col_spec(GLA_QK_DIM, 1), col_spec(GLA_V_DIM, 1), col_spec(GLA_V_DIM, 2)]
                + [col_spec(GLA_QK_DIM, part) for part in range(1 + len(GLA_FINE_LEVELS))]
                + [col_spec(GLA_DK, h) for h in range(GLA_HEADS)] + [gn_spec, sgn_spec]
                + [srow_spec(GLA_QK_DIM, 0), srow_spec(GLA_QK_DIM, 1), srow_spec(GLA_V_DIM, 1),
                   srow_spec(GLA_V_DIM, 2), srow_spec(GATE_RANK, 0),
                   pl.BlockSpec((None, GATE_RANK, GLA_QK_DIM), lambda b, c: (layer, 0, 0)),
                   pl.BlockSpec(gate_b.shape, lambda b, c: (0, 0)), sst_spec])
    args = [qkvg, qkvg, qkvg, qkvg, *([b_all] * (1 + len(GLA_FINE_LEVELS) + GLA_HEADS)), gla_norm_g, sgn,
            qkvg_s, qkvg_s, qkvg_s, qkvg_s, glr_s, gate_w2, gate_b, state_s]
    aliases = _stacked_operand(in_specs, args, stacked, 3)
    yb, s_fin, yb_s, state_new = pl.pallas_call(
        functools.partial(_gla_prompt_kernel, layer=layer), grid=(1, nc), in_specs=in_specs,
        out_specs=[yb_spec, s_spec, srow_spec(GLA_V_DIM, 0), sst_spec],
        out_shape=[jax.ShapeDtypeStruct((batch, seq, GLA_V_DIM), BF16),
                   jax.ShapeDtypeStruct((batch, GLA_HEADS, GLA_DK, GLA_DV), F32),
                   jax.ShapeDtypeStruct((nb, GLA_V_DIM), BF16),
                   jax.ShapeDtypeStruct(state_s.shape, F32)],
        scratch_shapes=[pltpu.VMEM((n_seq, GLA_HEADS, GLA_DK, GLA_DV), F32)],
        input_output_aliases=aliases, compiler_params=_params(2), name="gla")(*args)
    return yb.reshape(batch * seq, GLA_V_DIM), s_fin, yb_s, state_new


def _columns(slab):
    pad = jnp.zeros((GLA_DK - slab.shape[0], GLA_DK), F32)
    return jnp.concatenate([slab, pad], axis=0).T


def _gla_step_body(q_ref, k_ref, v_ref, og_ref, glr_ref, w2_ref, gb, gn, s_ref, yb_ref, snew_ref):
    rows = q_ref.shape[0]
    decay_all = jnp.exp(_gla_log_decay(glr_ref[...], w2_ref[...], gb))
    row_id = lax.broadcasted_iota(jnp.int32, (rows, rows * GLA_DV), 0)
    lane_blk = lax.broadcasted_iota(jnp.int32, (rows, rows * GLA_DV), 1) // GLA_DV

    for h in range(GLA_HEADS):
        ks = slice(h * GLA_DK, (h + 1) * GLA_DK)
        vs = slice(h * GLA_DV, (h + 1) * GLA_DV)
        decay = decay_all[:, ks]
        q = q_ref[:, ks] * (GLA_DK ** -0.5)
        k = k_ref[:, ks]
        v = v_ref[:, vs]
        qk = jnp.sum(q * k, axis=-1, keepdims=True)

        v_diag = jnp.where(row_id == lane_blk, jnp.concatenate([v] * rows, axis=1), 0.0)
        v_diag = jnp.concatenate([v_diag, jnp.zeros_like(v_diag)], axis=0)
        k_t = _columns(k)[:, 0:2 * rows]
        kv = _dot(k_t.astype(BF16), v_diag.astype(BF16))

        decay_c = _columns(decay)
        qd = (q * decay).astype(BF16)
        inter_rows = []
        for i in range(rows):
            s_prev = s_ref[i, h]
            inter_rows.append(_dot(qd, s_prev.astype(BF16))[i:i + 1, :])
            snew_ref[i, h] = decay_c[:, i:i + 1] * s_prev + kv[:, i * GLA_DV:(i + 1) * GLA_DV]
        o = jnp.concatenate(inter_rows, axis=0) + qk * v
        yb_ref[:, vs] = _gla_out(o, gn[:, vs], og_ref[:, vs]).astype(BF16)


def kernel(x_prompt, x_sample, state_conv, state_gla, state_ffn_conv, norm_mix_g, w_in, conv_w, gate_w2, gate_b, gla_norm_g, w_out, norm_ffn_g, w_up, ffn_conv_w, ffn_conv_b, w_down, final_norm_g):
    batch, seq, _ = x_prompt.shape
    nb = x_sample.shape[0]
    final_norm_g = final_norm_g[None, :]
    w_in_t = jnp.swapaxes(w_in, 1, 2)

    xp = x_prompt.reshape(batch * seq, D_MODEL)
    xs = x_sample.reshape(nb, D_MODEL)
    conv_p, gla_p, ffn_p = [], [], []
    conv_s, gla_s, ffn_s = None, None, None
    for l in range(DEPTH):
        ya, ya_s, qkvg, qkvg_s, glr, glr_s, c_new, conv_s = _mixer_in(
            xp, xs, state_conv, norm_mix_g, w_in_t, conv_w, l, seq, MIXER_IN_TN, conv_s)

        yb, s_new, yb_s, gla_s = _gla(qkvg, glr, qkvg_s, glr_s, state_gla, gate_w2, gate_b, gla_norm_g,
                                      l, batch, seq, gla_s)
        xp, xs = _out_proj(ya, yb, ya_s, yb_s, w_out, xp, xs, l, OUT_PROJ_TM)

        xp, xs, f_new, ffn_s = _ffn(xp, xs, state_ffn_conv, norm_ffn_g, w_up, ffn_conv_w, ffn_conv_b, w_down,
                                    final_norm_g, l, seq, FFN_TM, FFN_TF, l == DEPTH - 1, ffn_s)
        conv_p.append(c_new)
        gla_p.append(s_new)
        ffn_p.append(f_new)

    return (xp.reshape(batch, seq, D_MODEL), xs.reshape(nb, 1, D_MODEL),
            jnp.stack(conv_p), jnp.stack(gla_p), jnp.stack(ffn_p), conv_s, gla_s, ffn_s)
```
